```python
import math
import jax, jax.numpy as jnp
from jax import lax
import numpy as np

D_MODEL = 1024
BATCH = 8
SEQ = 4096
DEPTH = 1

GDN_HEADS = 4
GDN_HEAD_DIM = 128
GDN_WIDTH = GDN_HEADS * GDN_HEAD_DIM
CONV_WIDTH = 4
CHUNK = 64
SWA_HEADS = 8
SWA_HEAD_DIM = 64
SWA_WIDTH = SWA_HEADS * SWA_HEAD_DIM
DILATED_PATTERNS = ((128, 1), (512, 4), (2048, 16))
NUM_BUCKETS = 32
MAX_DISTANCE = 2048
MIX_WIDTH = GDN_WIDTH + SWA_WIDTH
IN_SIZES = (GDN_WIDTH, GDN_WIDTH, GDN_WIDTH, GDN_WIDTH, GDN_HEADS, GDN_HEADS,
            SWA_WIDTH, SWA_WIDTH, SWA_WIDTH)
IN_COLS = sum(IN_SIZES)
D_FF = ((-(-8 * D_MODEL // 3)) + 255) // 256 * 256
RMS_EPS = 1e-6

kernel_name = 'hybrid_gdn_dilated_swa_block'


def rmsnorm(x, g):
    xf = x.astype(jnp.float32)
    y = xf * lax.rsqrt(jnp.mean(xf * xf, axis=-1, keepdims=True) + RMS_EPS)
    return (y * g.astype(jnp.float32)).astype(x.dtype)


def l2norm(x):
    return x * lax.rsqrt(jnp.sum(x * x, axis=-1, keepdims=True) + 1e-6)


def causal_depthwise_conv(x, w):
    K, C = w.shape
    return lax.conv_general_dilated(x, w[:, None, :].astype(x.dtype), (1,), ((K - 1, 0),),
                                    dimension_numbers=('NWC', 'WIO', 'NWC'),
                                    feature_group_count=C)


def chunk_gated_delta_rule(q, k, v, g, beta):
    Bn, S, H, Dk = q.shape
    Dv = v.shape[-1]
    N = S // CHUNK

    def chunks(t):
        t = t.reshape((Bn, N, CHUNK, H) + t.shape[3:])
        return jnp.moveaxis(jnp.moveaxis(t, 1, 0), 3, 2)

    qc, kc, vc = chunks(q), chunks(k), chunks(v)
    bc = chunks(beta)
    gc = jnp.cumsum(chunks(g), axis=-1)
    idx = jnp.arange(CHUNK)
    causal = idx[:, None] >= idx[None, :]
    strict = idx[:, None] > idx[None, :]
    decay = jnp.exp(jnp.where(causal, gc[..., :, None] - gc[..., None, :], -jnp.inf))
    kb = kc * bc[..., None]
    a = jnp.where(strict, jnp.einsum('nbhid,nbhjd->nbhij', kb, kc) * decay, 0.0)
    eye = jnp.eye(CHUNK, dtype=q.dtype)
    t_inv = lax.linalg.triangular_solve(eye + a, jnp.broadcast_to(eye, a.shape),
                                        left_side=True, lower=True)
    u = t_inv @ (vc * bc[..., None])
    w = t_inv @ (kb * jnp.exp(gc)[..., None])
    attn = jnp.einsum('nbhid,nbhjd->nbhij', qc, kc) * decay
    q_dec = qc * jnp.exp(gc)[..., None]
    k_dec = kc * jnp.exp(gc[..., -1:] - gc)[..., None]
    g_end = jnp.exp(gc[..., -1])

    def step(state, inp):
        u_i, w_i, attn_i, qd_i, kd_i, ge_i = inp
        v_new = u_i - jnp.einsum('bhck,bhkv->bhcv', w_i, state)
        o_i = (jnp.einsum('bhck,bhkv->bhcv', qd_i, state)
               + jnp.einsum('bhij,bhjv->bhiv', attn_i, v_new))
        state = state * ge_i[..., None, None] + jnp.einsum('bhck,bhcv->bhkv', kd_i, v_new)
        return state, o_i

    s0 = jnp.zeros((Bn, H, Dk, Dv), q.dtype)
    _, o = lax.scan(step, s0, (u, w, attn, q_dec, k_dec, g_end))
    return jnp.moveaxis(o, 0, 1).transpose(0, 1, 3, 2, 4).reshape(Bn, S, H, Dv)


def gated_deltanet(q, k, v, gate, beta_logit, a_logit, conv_w, a_log, dt_bias, onorm_g):
    f32 = jnp.float32
    Bn, S, _ = q.shape
    qkv = jax.nn.silu(causal_depthwise_conv(jnp.concatenate([q, k, v], axis=-1), conv_w)).astype(f32)
    q, k, v = jnp.split(qkv, 3, axis=-1)
    heads = lambda t: t.reshape(Bn, S, GDN_HEADS, GDN_HEAD_DIM)
    q = l2norm(heads(q)) * GDN_HEAD_DIM ** -0.5
    k = l2norm(heads(k))
    v = heads(v)
    beta = jax.nn.sigmoid(beta_logit.astype(f32))
    g = -jnp.exp(a_log.astype(f32)) * jax.nn.softplus(a_logit.astype(f32) + dt_bias.astype(f32))
    o = chunk_gated_delta_rule(q, k, v, g, beta)
    o = o * lax.rsqrt(jnp.mean(o * o, axis=-1, keepdims=True) + RMS_EPS) * onorm_g.astype(f32)
    o = o * jax.nn.silu(heads(gate).astype(f32))
    return o.reshape(Bn, S, GDN_WIDTH).astype(gate.dtype)


def t5_causal_bucket(dist):
    max_exact = NUM_BUCKETS // 2
    d = jnp.maximum(dist, 1).astype(jnp.float32)
    log_b = max_exact + (jnp.log(d / max_exact) / math.log(MAX_DISTANCE / max_exact)
                         * (NUM_BUCKETS - max_exact)).astype(jnp.int32)
    return jnp.where(dist < max_exact, dist, jnp.minimum(log_b, NUM_BUCKETS - 1))


def dilated_band(q, k, v, rel_bias, window, dilation):
    Bn, S, H, Dh = q.shape
    w = window // dilation
    L = S // dilation
    nb = -(-L // w)
    Lp = nb * w

    def to_sub(t):
        t = t.reshape(Bn, L, dilation, H, Dh).transpose(0, 3, 2, 1, 4)
        return jnp.pad(t, ((0, 0), (0, 0), (0, 0), (0, Lp - L), (0, 0)))

    blocks = lambda t: t.reshape(Bn, H, dilation, nb, w, Dh)

    def band(t):
        prev = jnp.pad(t, ((0, 0), (0, 0), (0, 0), (w, 0), (0, 0)))[:, :, :, :Lp]
        return jnp.concatenate([blocks(prev), blocks(t)], axis=-2)

    qb, kb, vb = blocks(to_sub(q)), band(to_sub(k)), band(to_sub(v))
    qi = jnp.arange(w)[:, None]
    kj = jnp.arange(2 * w)[None, :]
    rel = qi + w - kj
    blk = jnp.arange(nb)[:, None, None]
    valid = (rel >= 0) & (rel <= w) & (blk * w + kj - w >= 0)
    bias_steps = rel_bias[t5_causal_bucket(jnp.arange(w + 1) * dilation)]
    bias = jnp.moveaxis(bias_steps[jnp.clip(rel, 0, w)], -1, 0).astype(jnp.float32)
    logits = jnp.einsum('bhrnqd,bhrnkd->bhrnqk', qb, kb) * Dh ** -0.5 + bias[None, :, None, None]
    logits = jnp.where(valid, logits, -jnp.inf)
    m = jnp.max(logits, axis=-1)
    p = jnp.exp(logits - m[..., None])
    s = jnp.sum(p, axis=-1)
    num = jnp.einsum('bhrnqk,bhrnkd->bhrnqd', p, vb)

    def from_sub(t):
        rest = t.shape[5:]
        t = t.reshape((Bn, H, dilation, Lp) + rest)[:, :, :, :L]
        t = t.transpose((0, 3, 2, 1) + tuple(range(4, t.ndim)))
        return t.reshape((Bn, S, H) + rest)

    return from_sub(num), from_sub(m), from_sub(s)


def dilated_attention(q, k, v, rel_bias):
    Bn, S, _ = q.shape
    heads = lambda t: t.reshape(Bn, S, SWA_HEADS, SWA_HEAD_DIM).astype(jnp.float32)
    qh, kh, vh = heads(q), heads(k), heads(v)
    parts = [dilated_band(qh, kh, vh, rel_bias, win, dil) for win, dil in DILATED_PATTERNS]
    nums = jnp.stack([pt[0] for pt in parts])
    ms = jnp.stack([pt[1] for pt in parts])
    ss = jnp.stack([pt[2] for pt in parts])
    wts = jnp.exp(ms - jnp.max(ms, axis=0, keepdims=True))
    out = jnp.sum(wts[..., None] * nums, axis=0) / jnp.sum(wts * ss, axis=0)[..., None]
    return out.reshape(Bn, S, SWA_WIDTH).astype(q.dtype)


def setup_inputs(seed: int = 0) -> dict:
    key = jax.random.key(seed)
    ks = jax.random.split(key, 16)
    f32 = jnp.float32
    nrm = lambda k_, shape, scale: jax.random.normal(k_, shape, f32) * scale
    dt = jnp.exp(jax.random.uniform(ks[4], (DEPTH, GDN_HEADS), f32, math.log(1e-3), math.log(1e-1)))
    return {
        'x': nrm(ks[0], (BATCH, SEQ, D_MODEL), 1.0),
        'w_in': nrm(ks[1], (DEPTH, D_MODEL, IN_COLS), D_MODEL ** -0.5),
        'conv_w': nrm(ks[2], (DEPTH, CONV_WIDTH, 3 * GDN_WIDTH), CONV_WIDTH ** -0.5),
        'a_log': jnp.log(jax.random.uniform(ks[3], (DEPTH, GDN_HEADS), f32, 1.0, 16.0)),
        'dt_bias': dt + jnp.log(-jnp.expm1(-dt)),
        'onorm_g': 1.0 + nrm(ks[5], (DEPTH, GDN_HEAD_DIM), 0.1),
        'rel_bias': nrm(ks[6], (NUM_BUCKETS, SWA_HEADS), 0.2),
        'w_out': nrm(ks[7], (DEPTH, MIX_WIDTH, D_MODEL), MIX_WIDTH ** -0.5),
        'g_mix_pre': 1.0 + nrm(ks[8], (DEPTH, D_MODEL), 0.1),
        'g_mix_post': 1.0 + nrm(ks[9], (DEPTH, D_MODEL), 0.1),
        'w_gate': nrm(ks[10], (DEPTH, D_MODEL, D_FF), D_MODEL ** -0.5),
        'w_up': nrm(ks[11], (DEPTH, D_MODEL, D_FF), D_MODEL ** -0.5),
        'w_down': nrm(ks[12], (DEPTH, D_FF, D_MODEL), D_FF ** -0.5),
        'g_ffn_pre': 1.0 + nrm(ks[13], (DEPTH, D_MODEL), 0.1),
        'g_ffn_post': 1.0 + nrm(ks[14], (DEPTH, D_MODEL), 0.1),
    }


def reference(x, w_in, conv_w, a_log, dt_bias, onorm_g, rel_bias, w_out, g_mix_pre, g_mix_post,
              w_gate, w_up, w_down, g_ffn_pre, g_ffn_post):
    split_at = [int(c) for c in np.cumsum(IN_SIZES)[:-1]]
    for l in range(DEPTH):
        h = rmsnorm(x, g_mix_pre[l])
        proj = h @ w_in[l]
        qa, ka, va, gate_a, beta_a, alpha_a, qb, kb, vb = jnp.split(proj, split_at, axis=-1)
        out_a = gated_deltanet(qa, ka, va, gate_a, beta_a, alpha_a,
                               conv_w[l], a_log[l], dt_bias[l], onorm_g[l])
        out_b = dilated_attention(qb, kb, vb, rel_bias)
        mix = jnp.concatenate([out_a, out_b], axis=-1) @ w_out[l]
        x = x + rmsnorm(mix, g_mix_post[l])
        h = rmsnorm(x, g_ffn_pre[l])
        f = (jax.nn.silu(h @ w_gate[l]) * (h @ w_up[l])) @ w_down[l]
        x = x + rmsnorm(f, g_ffn_post[l])
    return x
```

```python
import functools
import math

import jax
import jax.numpy as jnp
from jax import lax
from jax.experimental import pallas as pl
from jax.experimental.pallas import tpu as pltpu

F32 = jnp.float32
BF16 = jnp.bfloat16

GDN_HEADS = 4
GDN_HEAD_DIM = 128
GDN_WIDTH = GDN_HEADS * GDN_HEAD_DIM
CONV_WIDTH = 4
CHUNK = 64
SWA_HEADS = 8
SWA_HEAD_DIM = 64
SWA_WIDTH = SWA_HEADS * SWA_HEAD_DIM
DILATED_PATTERNS = ((128, 1), (512, 4), (2048, 16))
NUM_BUCKETS = 32
MAX_DISTANCE = 2048
RMS_EPS = 1e-6

LANES = 128
SUBLANES = 8
VMEM_LIMIT_BYTES = 56 * 1024 * 1024

NT_DIMS = (((1,), (1,)), ((), ()))
TN_DIMS = (((0,), (0,)), ((), ()))


def _rms(x, g):
    return x * lax.rsqrt(jnp.mean(x * x, axis=-1, keepdims=True) + RMS_EPS) * g


def _silu(x):
    return x * jax.nn.sigmoid(x)


def _softplus(x):
    return jnp.maximum(x, 0.0) + jnp.log1p(jnp.exp(-jnp.abs(x)))


def _resident(shape):
    zeros = (0,) * len(shape)
    return pl.BlockSpec(shape, lambda *_: zeros, pipeline_mode=pl.Buffered(1))


def _inproj_kernel(x_ref, g_ref, wqkv_ref, wgate_ref, wq_ref, wk_ref, wv_ref, ws_ref,
                   qkv_ref, gate_ref, qb_ref, kb_ref, vb_ref, sm_ref):
    h = _rms(x_ref[...], g_ref[...]).astype(BF16)
    for w_ref, o_ref in ((wqkv_ref, qkv_ref), (wgate_ref, gate_ref), (wq_ref, qb_ref),
                         (wk_ref, kb_ref), (wv_ref, vb_ref), (ws_ref, sm_ref)):
        o_ref[...] = jnp.dot(h, w_ref[...], preferred_element_type=F32).astype(o_ref.dtype)


def _inproj(x2d, g, wqkv, wgate, wq, wk, wv, ws, tm):
    T, D = x2d.shape
    row = lambda n: pl.BlockSpec((tm, n), lambda i: (i, 0))
    outs = ((3 * GDN_WIDTH, F32), (GDN_WIDTH, F32), (SWA_WIDTH, BF16), (SWA_WIDTH, BF16),
            (SWA_WIDTH, BF16), (LANES, F32))
    return pl.pallas_call(
        _inproj_kernel,
        grid=(T // tm,),
        in_specs=[row(D), _resident((1, D))] + [_resident(w.shape) for w in (wqkv, wgate, wq, wk, wv, ws)],
        out_specs=[row(n) for n, _ in outs],
        out_shape=[jax.ShapeDtypeStruct((T, n), dt) for n, dt in outs],
        compiler_params=pltpu.CompilerParams(dimension_semantics=("arbitrary",),
                                             vmem_limit_bytes=VMEM_LIMIT_BYTES),
        name="inproj",
    )(x2d, g, wqkv, wgate, wq, wk, wv, ws)


def _unit_lower_inverse(a, eye, xor_ij):
    n = a.shape[0]
    d = eye - jnp.where(xor_ij < 2, a, 0.0)
    s = 2
    while s < n:
        e = jnp.where(jnp.logical_and(xor_ij >= s, xor_ij < 2 * s), a, 0.0).astype(BF16)
        db = d.astype(BF16)
        ed = jnp.dot(e, db, preferred_element_type=F32).astype(BF16)
        d = d - jnp.dot(db, ed, preferred_element_type=F32)
        s *= 2
    return d


def _gdn_kernel(qkv_ref, gate_ref, sm_ref, convw_ref, alog_ref, dtb_ref, ong_ref, ltri_ref,
                o_ref, ext_scr, q_scr, k_scr, v_scr, g_scr, beta_scr, state_scr, *, tb, chunk):
    width = 3 * GDN_WIDTH
    hd = GDN_HEAD_DIM
    halo = SUBLANES

    @pl.when(pl.program_id(1) == 0)
    def _():
        ext_scr[0:halo, :] = jnp.zeros((halo, width), F32)
        state_scr[...] = jnp.zeros_like(state_scr)

    @pl.when(pl.program_id(1) > 0)
    def _():
        ext_scr[0:halo, :] = ext_scr[tb:tb + halo, :]

    ext_scr[halo:halo + tb, :] = qkv_ref[...]

    for j in range(width // hd):
        cols = slice(j * hd, (j + 1) * hd)
        acc = None
        for i in range(CONV_WIDTH):
            start = halo - (CONV_WIDTH - 1) + i
            term = convw_ref[i:i + 1, cols] * ext_scr[start:start + tb, cols]
            acc = term if acc is None else acc + term
        y = _silu(acc)
        kind, head = divmod(j, GDN_HEADS)
        hcols = slice(head * hd, (head + 1) * hd)
        if kind < 2:
            y = y * lax.rsqrt(jnp.sum(y * y, axis=-1, keepdims=True) + 1e-6)
        if kind == 0:
            q_scr[:, hcols] = y * (hd ** -0.5)
        elif kind == 1:
            k_scr[:, hcols] = y
        else:
            v_scr[:, hcols] = y

    sm = sm_ref[...]
    beta_scr[...] = jax.nn.sigmoid(sm)
    g_scr[...] = -jnp.exp(alog_ref[...]) * _softplus(sm + dtb_ref[...])

    ii = lax.broadcasted_iota(jnp.int32, (chunk, chunk), 0)
    jj = lax.broadcasted_iota(jnp.int32, (chunk, chunk), 1)
    xor_ij = jnp.bitwise_xor(ii, jj)
    lower = ii >= jj
    strict = ii > jj
    eye = jnp.where(ii == jj, 1.0, 0.0).astype(F32)
    ltri = ltri_ref[...]
    ong = ong_ref[...]

    def chunk_body(c, carry):
        r0 = pl.multiple_of(c * chunk, chunk)
        rows = pl.ds(r0, chunk)
        g = g_scr[rows, :]
        g_hi = g.astype(BF16)
        rem = g - g_hi.astype(F32)
        g_mid = rem.astype(BF16)
        g_lo = (rem - g_mid.astype(F32)).astype(BF16)
        gc = (jnp.dot(ltri, g_hi, preferred_element_type=F32)
              + jnp.dot(ltri, g_mid, preferred_element_type=F32)
              + jnp.dot(ltri, g_lo, preferred_element_type=F32))
        gc_t = gc.T
        beta = beta_scr[rows, :]
        for h in range(GDN_HEADS):
            hcols = slice(h * hd, (h + 1) * hd)
            gcol = gc[:, GDN_HEADS + h:GDN_HEADS + h + 1]
            grow = gc_t[GDN_HEADS + h:GDN_HEADS + h + 1, :]
            glast = gc[chunk - 1:chunk, GDN_HEADS + h:GDN_HEADS + h + 1]
            bcol = beta[:, h:h + 1]
            qc = q_scr[rows, hcols]
            kc = k_scr[rows, hcols]
            vc = v_scr[rows, hcols]
            decay = jnp.exp(jnp.where(lower, gcol - grow, -jnp.inf))
            kcb = kc.astype(BF16)
            qk = lax.dot_general(jnp.concatenate([qc.astype(BF16), kcb], axis=0), kcb, NT_DIMS,
                                 preferred_element_type=F32)
            attn = qk[:chunk] * decay
            a = jnp.where(strict, qk[chunk:] * decay * bcol, 0.0)
            t_inv = _unit_lower_inverse(a, eye, xor_ij)
            eg = jnp.exp(gcol)
            rhs = jnp.concatenate([vc * bcol, kc * (bcol * eg)], axis=1).astype(BF16)
            uw = jnp.dot(t_inv.astype(BF16), rhs, preferred_element_type=F32)
            u = uw[:, :hd]
            w = uw[:, hd:]
            q_dec = qc * eg
            k_dec = (kc * jnp.exp(glast - gcol)).astype(BF16)
            state = state_scr[h]
            ws = jnp.dot(jnp.concatenate([w, q_dec], axis=0).astype(BF16), state.astype(BF16),
                         preferred_element_type=F32)
            v_new = (u - ws[:chunk]).astype(BF16)
            o = ws[chunk:] + jnp.dot(attn.astype(BF16), v_new, preferred_element_type=F32)
            state_scr[h] = state * jnp.exp(glast) + lax.dot_general(
                k_dec, v_new, TN_DIMS, preferred_element_type=F32)
            o = _rms(o, ong)
            o_ref[rows, hcols] = (o * _silu(gate_ref[rows, hcols])).astype(o_ref.dtype)
        return carry

    lax.fori_loop(0, tb // chunk, chunk_body, 0)


def _gdn(qkv, gate, sm, conv_w, alog_row, dtb_row, onorm_g, batch, seq, tb):
    T = batch * seq
    nb = seq // tb
    width = 3 * GDN_WIDTH
    idx = jnp.arange(CHUNK)
    ltri = (idx[:, None] >= idx[None, :]).astype(BF16)
    row = lambda n: pl.BlockSpec((tb, n), lambda b, t: (b * nb + t, 0))
    kern = functools.partial(_gdn_kernel, tb=tb, chunk=CHUNK)
    return pl.pallas_call(
        kern,
        grid=(batch, nb),
        in_specs=[row(width), row(GDN_WIDTH), row(LANES), _resident(conv_w.shape),
                  _resident((1, LANES)), _resident((1, LANES)), _resident((1, GDN_HEAD_DIM)),
                  _resident((CHUNK, CHUNK))],
        out_specs=row(GDN_WIDTH),
        out_shape=jax.ShapeDtypeStruct((T, GDN_WIDTH), BF16),
        scratch_shapes=[pltpu.VMEM((tb + 2 * SUBLANES, width), F32),
                        pltpu.VMEM((tb, GDN_WIDTH), F32),
                        pltpu.VMEM((tb, GDN_WIDTH), F32),
                        pltpu.VMEM((tb, GDN_WIDTH), F32),
                        pltpu.VMEM((tb, LANES), F32),
                        pltpu.VMEM((tb, LANES), F32),
                        pltpu.VMEM((GDN_HEADS, GDN_HEAD_DIM, GDN_HEAD_DIM), F32)],
        compiler_params=pltpu.CompilerParams(dimension_semantics=("arbitrary", "arbitrary"),
                                             vmem_limit_bytes=VMEM_LIMIT_BYTES),
        name="gdn",
    )(qkv, gate, sm, conv_w, alog_row, dtb_row, onorm_g, ltri)


def _swa_kernel(q_ref, k_ref, v_ref, bias_ref, num_ref, m_ref, s_ref, *, w):
    n = pl.program_id(2)
    q = q_ref[...] * (SWA_HEAD_DIM ** -0.5)
    prev = pl.ds(pl.multiple_of(jnp.maximum(n - 1, 0) * w, w), w)
    cur = pl.ds(pl.multiple_of(n * w, w), w)
    kk = jnp.concatenate([k_ref[prev, :], k_ref[cur, :]], axis=0)
    vv = jnp.concatenate([v_ref[prev, :], v_ref[cur, :]], axis=0)
    col = lax.broadcasted_iota(jnp.int32, (w, 2 * w), 1)
    no_prev = jnp.logical_and(n == 0, col < w)
    lane = lax.broadcasted_iota(jnp.int32, (w, LANES), 1)
    low_half = lane < SWA_HEAD_DIM
    lane8 = lax.broadcasted_iota(jnp.int32, (w, SWA_HEADS), 1)
    m_all = jnp.zeros((w, SWA_HEADS), F32)
    s_all = jnp.zeros((w, SWA_HEADS), F32)
    heads_per_vreg = LANES // SWA_HEAD_DIM
    for pair in range(SWA_HEADS // heads_per_vreg):
        cols = slice(pair * LANES, (pair + 1) * LANES)
        qp, kp, vp = q[:, cols], kk[:, cols], vv[:, cols]
        outs = []
        for e in range(heads_per_vreg):
            h = pair * heads_per_vreg + e
            mine = low_half if e == 0 else jnp.logical_not(low_half)
            qh = jnp.where(mine, qp, jnp.zeros_like(qp))
            logits = lax.dot_general(qh, kp, NT_DIMS, preferred_element_type=F32) + bias_ref[h]
            logits = jnp.where(no_prev, -jnp.inf, logits)
            m = jnp.max(logits, axis=-1, keepdims=True)
            p = jnp.exp(logits - m)
            s = jnp.sum(p, axis=-1, keepdims=True)
            outs.append(jnp.dot(p.astype(BF16), vp, preferred_element_type=F32))
            m_all = jnp.where(lane8 == h, m, m_all)
            s_all = jnp.where(lane8 == h, s, s_all)
        num_ref[:, cols] = jnp.where(low_half, outs[0], outs[1])
    m_ref[...] = m_all
    s_ref[...] = s_all


def _swa_pattern(qb, kb, vb, bias, batch, seq, window, dilation):
    w = window // dilation
    L = seq // dilation
    nb = L // w
    width = SWA_WIDTH
    view = lambda t: t.reshape(batch, L, dilation * width)
    blk = pl.BlockSpec((None, w, width), lambda b, r, n: (b, n, r))
    seq_blk = pl.BlockSpec((None, L, width), lambda b, r, n: (b, 0, r))
    stat_blk = pl.BlockSpec((None, None, w, SWA_HEADS), lambda b, r, n: (b, r, n, 0))
    stat_shape = jax.ShapeDtypeStruct((batch, dilation, L, SWA_HEADS), F32)
    num, m, s = pl.pallas_call(
        functools.partial(_swa_kernel, w=w),
        grid=(batch, dilation, nb),
        in_specs=[blk, seq_blk, seq_blk, _resident(bias.shape)],
        out_specs=[blk, stat_blk, stat_blk],
        out_shape=[jax.ShapeDtypeStruct((batch, L, dilation * width), F32), stat_shape, stat_shape],
        compiler_params=pltpu.CompilerParams(
            dimension_semantics=("arbitrary", "arbitrary", "arbitrary"),
            vmem_limit_bytes=VMEM_LIMIT_BYTES),
        name=f"swa_d{dilation}",
    )(view(qb), view(kb), view(vb), bias)
    to_tokens = lambda t: t.transpose(0, 2, 1, 3).reshape(batch * seq, SWA_HEADS)
    return num.reshape(batch * seq, width), to_tokens(m), to_tokens(s)


def _t5_causal_bucket(dist):
    max_exact = NUM_BUCKETS // 2
    d = jnp.maximum(dist, 1).astype(F32)
    log_b = max_exact + (jnp.log(d / max_exact) / math.log(MAX_DISTANCE / max_exact)
                         * (NUM_BUCKETS - max_exact)).astype(jnp.int32)
    return jnp.where(dist < max_exact, dist, jnp.minimum(log_b, NUM_BUCKETS - 1))


def _band_bias(rel_bias, window, dilation):
    w = window // dilation
    rel = jnp.arange(w)[:, None] + w - jnp.arange(2 * w)[None, :]
    steps = rel_bias[_t5_causal_bucket(jnp.arange(w + 1) * dilation)].astype(F32)
    bias = jnp.moveaxis(steps[jnp.clip(rel, 0, w)], -1, 0)
    return jnp.where(jnp.logical_and(rel >= 0, rel <= w)[None], bias, -jnp.inf)


def _mixffn_kernel(x_ref, oa_ref, n0_ref, n1_ref, n2_ref, m0_ref, m1_ref, m2_ref,
                   s0_ref, s1_ref, s2_ref, woa_ref, wob_ref, gpost_ref, gpre_ref, gfpost_ref,
                   wg_ref, wu_ref, wd_ref, out_ref, act_scr, ob_scr, *, ff_chunk):
    tm = x_ref.shape[0]
    ms = (m0_ref[...], m1_ref[...], m2_ref[...])
    m_max = jnp.maximum(jnp.maximum(ms[0], ms[1]), ms[2])
    wts = [jnp.exp(m - m_max) for m in ms]
    den = wts[0] * s0_ref[...] + wts[1] * s1_ref[...] + wts[2] * s2_ref[...]
    coef = [wt / den for wt in wts]
    low_half = lax.broadcasted_iota(jnp.int32, (tm, LANES), 1) < SWA_HEAD_DIM
    heads_per_vreg = LANES // SWA_HEAD_DIM
    for pair in range(SWA_HEADS // heads_per_vreg):
        cols = slice(pair * LANES, (pair + 1) * LANES)
        h0 = pair * heads_per_vreg
        acc = None
        for c, n_ref in zip(coef, (n0_ref, n1_ref, n2_ref)):
            term = jnp.where(low_half, c[:, h0:h0 + 1], c[:, h0 + 1:h0 + 2]) * n_ref[:, cols]
            acc = term if acc is None else acc + term
        ob_scr[:, cols] = acc.astype(BF16)
    mix = (jnp.dot(oa_ref[...], woa_ref[...], preferred_element_type=F32)
           + jnp.dot(ob_scr[...], wob_ref[...], preferred_element_type=F32))
    x1 = x_ref[...] + _rms(mix, gpost_ref[...])
    h = _rms(x1, gpre_ref[...]).astype(BF16)
    d_ff = wg_ref.shape[1]
    for c in range(d_ff // ff_chunk):
        cols = slice(c * ff_chunk, (c + 1) * ff_chunk)
        gate = jnp.dot(h, wg_ref[:, cols], preferred_element_type=F32)
        up = jnp.dot(h, wu_ref[:, cols], preferred_element_type=F32)
        act_scr[:, cols] = (_silu(gate) * up).astype(BF16)
    f = jnp.dot(act_scr[...], wd_ref[...], preferred_element_type=F32)
    out_ref[...] = x1 + _rms(f, gfpost_ref[...])


def _mixffn(x2d, oa, nums, ms, ss, woa, wob, gpost, gpre, gfpost, wg, wu, wd, tm, ff_chunk):
    T, D = x2d.shape
    d_ff = wg.shape[1]
    row = lambda n: pl.BlockSpec((tm, n), lambda i: (i, 0))
    return pl.pallas_call(
        functools.partial(_mixffn_kernel, ff_chunk=ff_chunk),
        grid=(T // tm,),
        in_specs=([row(D), row(GDN_WIDTH)] + [row(SWA_WIDTH)] * 3 + [row(SWA_HEADS)] * 6
                  + [_resident(woa.shape), _resident(wob.shape)] + [_resident((1, D))] * 3
                  + [_resident(wg.shape), _resident(wu.shape), _resident(wd.shape)]),
        out_specs=row(D),
        out_shape=jax.ShapeDtypeStruct((T, D), F32),
        scratch_shapes=[pltpu.VMEM((tm, d_ff), BF16), pltpu.VMEM((tm, SWA_WIDTH), BF16)],
        compiler_params=pltpu.CompilerParams(dimension_semantics=("arbitrary",),
                                             vmem_limit_bytes=VMEM_LIMIT_BYTES),
        name="mixffn",
    )(x2d, oa, *nums, *ms, *ss, woa, wob, gpost, gpre, gfpost, wg, wu, wd)


def _layer(x2d, batch, seq, w_in, conv_w, a_log, dt_bias, onorm_g, rel_bias, w_out,
           g_mix_pre, g_mix_post, w_gate, w_up, w_down, g_ffn_pre, g_ffn_post):
    D = x2d.shape[1]
    gw, sw, nh = GDN_WIDTH, SWA_WIDTH, GDN_HEADS
    wb = w_in.astype(BF16)
    c_gate, c_small, c_q = 3 * gw, 4 * gw, 4 * gw + 2 * nh
    w_small = jnp.pad(wb[:, c_small:c_q], ((0, 0), (0, LANES - 2 * nh)))
    qkv, gate, qb, kb, vb, sm = _inproj(
        x2d, g_mix_pre.reshape(1, D), wb[:, :c_gate], wb[:, c_gate:c_small],
        wb[:, c_q:c_q + sw], wb[:, c_q + sw:c_q + 2 * sw], wb[:, c_q + 2 * sw:], w_small, tm=512)

    lane_pad = lambda v: jnp.pad(v.astype(F32).reshape(1, nh), ((0, 0), (nh, LANES - 2 * nh)))
    oa = _gdn(qkv, gate, sm, conv_w.astype(F32), lane_pad(a_log), lane_pad(dt_bias),
              onorm_g.astype(F32).reshape(1, GDN_HEAD_DIM), batch, seq, tb=512)

    parts = [_swa_pattern(qb, kb, vb, _band_bias(rel_bias, win, dil), batch, seq, win, dil)
             for win, dil in DILATED_PATTERNS]
    nums, ms, ss = zip(*parts)

    wo = w_out.astype(BF16)
    return _mixffn(x2d, oa, nums, ms, ss, wo[:gw], wo[gw:], g_mix_post.reshape(1, D),
                   g_ffn_pre.reshape(1, D), g_ffn_post.reshape(1, D), w_gate.astype(BF16),
                   w_up.astype(BF16), w_down.astype(BF16), tm=512, ff_chunk=256)


def kernel(x, w_in, conv_w, a_log, dt_bias, onorm_g, rel_bias, w_out, g_mix_pre, g_mix_post,
           w_gate, w_up, w_down, g_ffn_pre, g_ffn_post):
    batch, seq, d_model = x.shape
    x2d = x.reshape(batch * seq, d_model)
    for l in range(w_in.shape[0]):
        x2d = _layer(x2d, batch, seq, w_in[l], conv_w[l], a_log[l], dt_bias[l], onorm_g[l],
                     rel_bias, w_out[l], g_mix_pre[l], g_mix_post[l], w_gate[l], w_up[l],
                     w_down[l], g_ffn_pre[l], g_ffn_post[l])
    return x2d.reshape(batch, seq, d_model)
```

```python
import functools
import math

import jax
import jax.numpy as jnp
from jax import lax
from jax.experimental import pallas as pl
from jax.experimental.pallas import tpu as pltpu

F32 = jnp.float32
BF16 = jnp.bfloat16

GDN_HEADS = 4
GDN_HEAD_DIM = 128
GDN_WIDTH = GDN_HEADS * GDN_HEAD_DIM
CONV_WIDTH = 4
CHUNK = 64
SWA_HEADS = 8
SWA_HEAD_DIM = 64
SWA_WIDTH = SWA_HEADS * SWA_HEAD_DIM
DILATED_PATTERNS = ((128, 1), (512, 4), (2048, 16))
NUM_BUCKETS = 32
MAX_DISTANCE = 2048
RMS_EPS = 1e-6

LANES = 128
SUBLANES = 8
VMEM_LIMIT_BYTES = 56 * 1024 * 1024

NT_DIMS = (((1,), (1,)), ((), ()))
TN_DIMS = (((0,), (0,)), ((), ()))


def _rms(x, g):
    return x * lax.rsqrt(jnp.mean(x * x, axis=-1, keepdims=True) + RMS_EPS) * g


def _silu(x):
    return x * jax.nn.sigmoid(x)


def _softplus(x):
    return jnp.maximum(x, 0.0) + jnp.log1p(jnp.exp(-jnp.abs(x)))


def _resident(shape):
    zeros = (0,) * len(shape)
    return pl.BlockSpec(shape, lambda *_: zeros, pipeline_mode=pl.Buffered(1))


def _inproj_kernel(x_ref, g_ref, wqkv_ref, wgate_ref, wq_ref, wk_ref, wv_ref, ws_ref,
                   qkv_ref, gate_ref, qb_ref, kb_ref, vb_ref, sm_ref):
    h = _rms(x_ref[...], g_ref[...]).astype(BF16)
    for w_ref, o_ref in ((wqkv_ref, qkv_ref), (wgate_ref, gate_ref), (wq_ref, qb_ref),
                         (wk_ref, kb_ref), (wv_ref, vb_ref), (ws_ref, sm_ref)):
        o_ref[...] = jnp.dot(h, w_ref[...], preferred_element_type=F32).astype(o_ref.dtype)


def _inproj(x2d, g, wqkv, wgate, wq, wk, wv, ws, tm):
    T, D = x2d.shape
    row = lambda n: pl.BlockSpec((tm, n), lambda i: (i, 0))
    outs = ((3 * GDN_WIDTH, F32), (GDN_WIDTH, F32), (SWA_WIDTH, BF16), (SWA_WIDTH, BF16),
            (SWA_WIDTH, BF16), (LANES, F32))
    return pl.pallas_call(
        _inproj_kernel,
        grid=(T // tm,),
        in_specs=[row(D), _resident((1, D))] + [_resident(w.shape) for w in (wqkv, wgate, wq, wk, wv, ws)],
        out_specs=[row(n) for n, _ in outs],
        out_shape=[jax.ShapeDtypeStruct((T, n), dt) for n, dt in outs],
        compiler_params=pltpu.CompilerParams(dimension_semantics=("arbitrary",),
                                             vmem_limit_bytes=VMEM_LIMIT_BYTES),
        name="inproj",
    )(x2d, g, wqkv, wgate, wq, wk, wv, ws)


def _unit_lower_inverses(mats, eye, xor_ij):
    n = mats[0].shape[0]
    ds = [eye - jnp.where(xor_ij < 2, a, 0.0) for a in mats]
    s = 2
    while s < n:
        band = jnp.logical_and(xor_ij >= s, xor_ij < 2 * s)
        dbs = [d.astype(BF16) for d in ds]
        eds = [jnp.dot(jnp.where(band, a, 0.0).astype(BF16), db,
                       preferred_element_type=F32).astype(BF16) for a, db in zip(mats, dbs)]
        ds = [d - jnp.dot(db, ed, preferred_element_type=F32) for d, db, ed in zip(ds, dbs, eds)]
        s *= 2
    return ds


def _gdn_kernel(qkv_ref, gate_ref, sm_ref, convw_ref, alog_ref, dtb_ref, ong_ref, ltri_ref,
                o_ref, ext_scr, q_scr, k_scr, v_scr, g_scr, beta_scr, state_scr, *, tb, chunk):
    width = 3 * GDN_WIDTH
    hd = GDN_HEAD_DIM
    halo = SUBLANES

    @pl.when(pl.program_id(1) == 0)
    def _():
        ext_scr[0:halo, :] = jnp.zeros((halo, width), F32)
        state_scr[...] = jnp.zeros_like(state_scr)

    @pl.when(pl.program_id(1) > 0)
    def _():
        ext_scr[0:halo, :] = ext_scr[tb:tb + halo, :]

    ext_scr[halo:halo + tb, :] = qkv_ref[...]

    for j in range(width // hd):
        cols = slice(j * hd, (j + 1) * hd)
        acc = None
        for i in range(CONV_WIDTH):
            start = halo - (CONV_WIDTH - 1) + i
            term = convw_ref[i:i + 1, cols] * ext_scr[start:start + tb, cols]
            acc = term if acc is None else acc + term
        y = _silu(acc)
        kind, head = divmod(j, GDN_HEADS)
        hcols = slice(head * hd, (head + 1) * hd)
        if kind < 2:
            y = y * lax.rsqrt(jnp.sum(y * y, axis=-1, keepdims=True) + 1e-6)
        if kind == 0:
            q_scr[:, hcols] = y * (hd ** -0.5)
        elif kind == 1:
            k_scr[:, hcols] = y
        else:
            v_scr[:, hcols] = y

    sm = sm_ref[...]
    beta_scr[...] = jax.nn.sigmoid(sm)
    g_scr[...] = -jnp.exp(alog_ref[...]) * _softplus(sm + dtb_ref[...])

    ii = lax.broadcasted_iota(jnp.int32, (chunk, chunk), 0)
    jj = lax.broadcasted_iota(jnp.int32, (chunk, chunk), 1)
    xor_ij = jnp.bitwise_xor(ii, jj)
    lower = ii >= jj
    strict = ii > jj
    eye = jnp.where(ii == jj, 1.0, 0.0).astype(F32)
    ltri = ltri_ref[...]
    ong = ong_ref[...]

    nc = tb // chunk
    heads = range(GDN_HEADS)
    rows = [slice(c * chunk, (c + 1) * chunk) for c in range(nc)]
    hcols = [slice(h * hd, (h + 1) * hd) for h in heads]
    probs = [(c, h) for c in range(nc) for h in heads]

    gcs = []
    for c in range(nc):
        g = g_scr[rows[c], :]
        g_hi = g.astype(BF16)
        rem = g - g_hi.astype(F32)
        g_mid = rem.astype(BF16)
        g_lo = (rem - g_mid.astype(F32)).astype(BF16)
        gcs.append(jnp.dot(ltri, g_hi, preferred_element_type=F32)
                   + jnp.dot(ltri, g_mid, preferred_element_type=F32)
                   + jnp.dot(ltri, g_lo, preferred_element_type=F32))
    gcts = [gc.T for gc in gcs]
    lane_of = lambda h: slice(GDN_HEADS + h, GDN_HEADS + h + 1)
    gcol = {(c, h): gcs[c][:, lane_of(h)] for c, h in probs}
    glast = {(c, h): gcs[c][chunk - 1:chunk, lane_of(h)] for c, h in probs}
    bcol = {(c, h): beta_scr[rows[c], h:h + 1] for c, h in probs}

    qks = {}
    for c, h in probs:
        kcb = k_scr[rows[c], hcols[h]].astype(BF16)
        qkb = jnp.concatenate([q_scr[rows[c], hcols[h]].astype(BF16), kcb], axis=0)
        qks[c, h] = lax.dot_general(qkb, kcb, NT_DIMS, preferred_element_type=F32)
    attn, a_mats = {}, []
    for c, h in probs:
        grow = gcts[c][lane_of(h), :]
        decay = jnp.exp(jnp.where(lower, gcol[c, h] - grow, -jnp.inf))
        attn[c, h] = (qks[c, h][:chunk] * decay).astype(BF16)
        a_mats.append(jnp.where(strict, qks[c, h][chunk:] * decay * bcol[c, h], 0.0))
    t_invs = _unit_lower_inverses(a_mats, eye, xor_ij)

    u, wq, k_dec = {}, {}, {}
    for (c, h), t_inv in zip(probs, t_invs):
        kc = k_scr[rows[c], hcols[h]]
        eg = jnp.exp(gcol[c, h])
        rhs = jnp.concatenate([v_scr[rows[c], hcols[h]] * bcol[c, h], kc * (bcol[c, h] * eg)],
                              axis=1).astype(BF16)
        uw = jnp.dot(t_inv.astype(BF16), rhs, preferred_element_type=F32)
        u[c, h] = uw[:, :hd]
        wq[c, h] = jnp.concatenate([uw[:, hd:], q_scr[rows[c], hcols[h]] * eg], axis=0).astype(BF16)
        k_dec[c, h] = (kc * jnp.exp(glast[c, h] - gcol[c, h])).astype(BF16)

    state = [state_scr[h] for h in heads]
    for c in range(nc):
        ws = [jnp.dot(wq[c, h], state[h].astype(BF16), preferred_element_type=F32) for h in heads]
        v_new = [(u[c, h] - ws[h][:chunk]).astype(BF16) for h in heads]
        o = [ws[h][chunk:] + jnp.dot(attn[c, h], v_new[h], preferred_element_type=F32) for h in heads]
        state = [state[h] * jnp.exp(glast[c, h])
                 + lax.dot_general(k_dec[c, h], v_new[h], TN_DIMS, preferred_element_type=F32)
                 for h in heads]
        for h in heads:
            gated = _rms(o[h], ong) * _silu(gate_ref[rows[c], hcols[h]])
            o_ref[rows[c], hcols[h]] = gated.astype(o_ref.dtype)
    for h in heads:
        state_scr[h] = state[h]


def _gdn(qkv, gate, sm, conv_w, alog_row, dtb_row, onorm_g, batch, seq, tb):
    T = batch * seq
    nb = seq // tb
    width = 3 * GDN_WIDTH
    idx = jnp.arange(CHUNK)
    ltri = (idx[:, None] >= idx[None, :]).astype(BF16)
    row = lambda n: pl.BlockSpec((tb, n), lambda b, t: (b * nb + t, 0))
    kern = functools.partial(_gdn_kernel, tb=tb, chunk=CHUNK)
    return pl.pallas_call(
        kern,
        grid=(batch, nb),
        in_specs=[row(width), row(GDN_WIDTH), row(LANES), _resident(conv_w.shape),
                  _resident((1, LANES)), _resident((1, LANES)), _resident((1, GDN_HEAD_DIM)),
                  _resident((CHUNK, CHUNK))],
        out_specs=row(GDN_WIDTH),
        out_shape=jax.ShapeDtypeStruct((T, GDN_WIDTH), BF16),
        scratch_shapes=[pltpu.VMEM((tb + 2 * SUBLANES, width), F32),
                        pltpu.VMEM((tb, GDN_WIDTH), F32),
                        pltpu.VMEM((tb, GDN_WIDTH), F32),
                        pltpu.VMEM((tb, GDN_WIDTH), F32),
                        pltpu.VMEM((tb, LANES), F32),
                        pltpu.VMEM((tb, LANES), F32),
                        pltpu.VMEM((GDN_HEADS, GDN_HEAD_DIM, GDN_HEAD_DIM), F32)],
        compiler_params=pltpu.CompilerParams(dimension_semantics=("arbitrary", "arbitrary"),
                                             vmem_limit_bytes=VMEM_LIMIT_BYTES),
        name="gdn",
    )(qkv, gate, sm, conv_w, alog_row, dtb_row, onorm_g, ltri)


def _swa_kernel(q_ref, k_ref, v_ref, bias_ref, num_ref, m_ref, s_ref, *, w):
    n = pl.program_id(2)
    q = q_ref[...] * (SWA_HEAD_DIM ** -0.5)
    prev = pl.ds(pl.multiple_of(jnp.maximum(n - 1, 0) * w, w), w)
    cur = pl.ds(pl.multiple_of(n * w, w), w)
    kk = jnp.concatenate([k_ref[prev, :], k_ref[cur, :]], axis=0)
    vv = jnp.concatenate([v_ref[prev, :], v_ref[cur, :]], axis=0)
    col = lax.broadcasted_iota(jnp.int32, (w, 2 * w), 1)
    no_prev = jnp.logical_and(n == 0, col < w)
    lane = lax.broadcasted_iota(jnp.int32, (w, LANES), 1)
    low_half = lane < SWA_HEAD_DIM
    lane8 = lax.broadcasted_iota(jnp.int32, (w, SWA_HEADS), 1)
    m_all = jnp.zeros((w, SWA_HEADS), F32)
    s_all = jnp.zeros((w, SWA_HEADS), F32)
    heads_per_vreg = LANES // SWA_HEAD_DIM
    for pair in range(SWA_HEADS // heads_per_vreg):
        cols = slice(pair * LANES, (pair + 1) * LANES)
        qp, kp, vp = q[:, cols], kk[:, cols], vv[:, cols]
        outs = []
        for e in range(heads_per_vreg):
            h = pair * heads_per_vreg + e
            mine = low_half if e == 0 else jnp.logical_not(low_half)
            qh = jnp.where(mine, qp, jnp.zeros_like(qp))
            logits = lax.dot_general(qh, kp, NT_DIMS, preferred_element_type=F32) + bias_ref[h]
            logits = jnp.where(no_prev, -jnp.inf, logits)
            m = jnp.max(logits, axis=-1, keepdims=True)
            p = jnp.exp(logits - m)
            s = jnp.sum(p, axis=-1, keepdims=True)
            outs.append(jnp.dot(p.astype(BF16), vp, preferred_element_type=F32))
            m_all = jnp.where(lane8 == h, m, m_all)
            s_all = jnp.where(lane8 == h, s, s_all)
        num_ref[:, cols] = jnp.where(low_half, outs[0], outs[1])
    m_ref[...] = m_all
    s_ref[...] = s_all


def _swa_pattern(qb, kb, vb, bias, batch, seq, window, dilation):
    w = window // dilation
    L = seq // dilation
    nb = L // w
    width = SWA_WIDTH
    view = lambda t: t.reshape(batch, L, dilation * width)
    blk = pl.BlockSpec((None, w, width), lambda b, r, n: (b, n, r))
    seq_blk = pl.BlockSpec((None, L, width), lambda b, r, n: (b, 0, r))
    stat_blk = pl.BlockSpec((None, None, w, SWA_HEADS), lambda b, r, n: (b, r, n, 0))
    stat_shape = jax.ShapeDtypeStruct((batch, dilation, L, SWA_HEADS), F32)
    num, m, s = pl.pallas_call(
        functools.partial(_swa_kernel, w=w),
        grid=(batch, dilation, nb),
        in_specs=[blk, seq_blk, seq_blk, _resident(bias.shape)],
        out_specs=[blk, stat_blk, stat_blk],
        out_shape=[jax.ShapeDtypeStruct((batch, L, dilation * width), F32), stat_shape, stat_shape],
        compiler_params=pltpu.CompilerParams(
            dimension_semantics=("arbitrary", "arbitrary", "arbitrary"),
            vmem_limit_bytes=VMEM_LIMIT_BYTES),
        name=f"swa_d{dilation}",
    )(view(qb), view(kb), view(vb), bias)
    to_tokens = lambda t: t.transpose(0, 2, 1, 3).reshape(batch * seq, SWA_HEADS)
    return num.reshape(batch * seq, width), to_tokens(m), to_tokens(s)


def _t5_causal_bucket(dist):
    max_exact = NUM_BUCKETS // 2
    d = jnp.maximum(dist, 1).astype(F32)
    log_b = max_exact + (jnp.log(d / max_exact) / math.log(MAX_DISTANCE / max_exact)
                         * (NUM_BUCKETS - max_exact)).astype(jnp.int32)
    return jnp.where(dist < max_exact, dist, jnp.minimum(log_b, NUM_BUCKETS - 1))


def _band_bias(rel_bias, window, dilation):
    w = window // dilation
    rel = jnp.arange(w)[:, None] + w - jnp.arange(2 * w)[None, :]
    steps = rel_bias[_t5_causal_bucket(jnp.arange(w + 1) * dilation)].astype(F32)
    bias = jnp.moveaxis(steps[jnp.clip(rel, 0, w)], -1, 0)
    return jnp.where(jnp.logical_and(rel >= 0, rel <= w)[None], bias, -jnp.inf)


def _mixffn_kernel(x_ref, oa_ref, n0_ref, n1_ref, n2_ref, m0_ref, m1_ref, m2_ref,
                   s0_ref, s1_ref, s2_ref, woa_ref, wob_ref, gpost_ref, gpre_ref, gfpost_ref,
                   wg_ref, wu_ref, wd_ref, out_ref, act_scr, ob_scr, *, ff_chunk):
    tm = x_ref.shape[0]
    ms = (m0_ref[...], m1_ref[...], m2_ref[...])
    m_max = jnp.maximum(jnp.maximum(ms[0], ms[1]), ms[2])
    wts = [jnp.exp(m - m_max) for m in ms]
    den = wts[0] * s0_ref[...] + wts[1] * s1_ref[...] + wts[2] * s2_ref[...]
    coef = [wt / den for wt in wts]
    low_half = lax.broadcasted_iota(jnp.int32, (tm, LANES), 1) < SWA_HEAD_DIM
    heads_per_vreg = LANES // SWA_HEAD_DIM
    for pair in range(SWA_HEADS // heads_per_vreg):
        cols = slice(pair * LANES, (pair + 1) * LANES)
        h0 = pair * heads_per_vreg
        acc = None
        for c, n_ref in zip(coef, (n0_ref, n1_ref, n2_ref)):
            term = jnp.where(low_half, c[:, h0:h0 + 1], c[:, h0 + 1:h0 + 2]) * n_ref[:, cols]
            acc = term if acc is None else acc + term
        ob_scr[:, cols] = acc.astype(BF16)
    mix = (jnp.dot(oa_ref[...], woa_ref[...], preferred_element_type=F32)
           + jnp.dot(ob_scr[...], wob_ref[...], preferred_element_type=F32))
    x1 = x_ref[...] + _rms(mix, gpost_ref[...])
    h = _rms(x1, gpre_ref[...]).astype(BF16)
    d_ff = wg_ref.shape[1]
    for c in range(d_ff // ff_chunk):
        cols = slice(c * ff_chunk, (c + 1) * ff_chunk)
        gate = jnp.dot(h, wg_ref[:, cols], preferred_element_type=F32)
        up = jnp.dot(h, wu_ref[:, cols], preferred_element_type=F32)
        act_scr[:, cols] = (_silu(gate) * up).astype(BF16)
    f = jnp.dot(act_scr[...], wd_ref[...], preferred_element_type=F32)
    out_ref[...] = x1 + _rms(f, gfpost_ref[...])


def _mixffn(x2d, oa, nums, ms, ss, woa, wob, gpost, gpre, gfpost, wg, wu, wd, tm, ff_chunk):
    T, D = x2d.shape
    d_ff = wg.shape[1]
    row = lambda n: pl.BlockSpec((tm, n), lambda i: (i, 0))
    return pl.pallas_call(
        functools.partial(_mixffn_kernel, ff_chunk=ff_chunk),
        grid=(T // tm,),
        in_specs=([row(D), row(GDN_WIDTH)] + [row(SWA_WIDTH)] * 3 + [row(SWA_HEADS)] * 6
                  + [_resident(woa.shape), _resident(wob.shape)] + [_resident((1, D))] * 3
                  + [_resident(wg.shape), _resident(wu.shape), _resident(wd.shape)]),
        out_specs=row(D),
        out_shape=jax.ShapeDtypeStruct((T, D), F32),
        scratch_shapes=[pltpu.VMEM((tm, d_ff), BF16), pltpu.VMEM((tm, SWA_WIDTH), BF16)],
        compiler_params=pltpu.CompilerParams(dimension_semantics=("arbitrary",),
                                             vmem_limit_bytes=VMEM_LIMIT_BYTES),
        name="mixffn",
    )(x2d, oa, *nums, *ms, *ss, woa, wob, gpost, gpre, gfpost, wg, wu, wd)


def _layer(x2d, batch, seq, w_in, conv_w, a_log, dt_bias, onorm_g, rel_bias, w_out,
           g_mix_pre, g_mix_post, w_gate, w_up, w_down, g_ffn_pre, g_ffn_post):
    D = x2d.shape[1]
    gw, sw, nh = GDN_WIDTH, SWA_WIDTH, GDN_HEADS
    wb = w_in.astype(BF16)
    c_gate, c_small, c_q = 3 * gw, 4 * gw, 4 * gw + 2 * nh
    w_small = jnp.pad(wb[:, c_small:c_q], ((0, 0), (0, LANES - 2 * nh)))
    qkv, gate, qb, kb, vb, sm = _inproj(
        x2d, g_mix_pre.reshape(1, D), wb[:, :c_gate], wb[:, c_gate:c_small],
        wb[:, c_q:c_q + sw], wb[:, c_q + sw:c_q + 2 * sw], wb[:, c_q + 2 * sw:], w_small, tm=512)

    lane_pad = lambda v: jnp.pad(v.astype(F32).reshape(1, nh), ((0, 0), (nh, LANES - 2 * nh)))
    oa = _gdn(qkv, gate, sm, conv_w.astype(F32), lane_pad(a_log), lane_pad(dt_bias),
              onorm_g.astype(F32).reshape(1, GDN_HEAD_DIM), batch, seq, tb=512)

    parts = [_swa_pattern(qb, kb, vb, _band_bias(rel_bias, win, dil), batch, seq, win, dil)
             for win, dil in DILATED_PATTERNS]
    nums, ms, ss = zip(*parts)

    wo = w_out.astype(BF16)
    return _mixffn(x2d, oa, nums, ms, ss, wo[:gw], wo[gw:], g_mix_post.reshape(1, D),
                   g_ffn_pre.reshape(1, D), g_ffn_post.reshape(1, D), w_gate.astype(BF16),
                   w_up.astype(BF16), w_down.astype(BF16), tm=512, ff_chunk=256)


def kernel(x, w_in, conv_w, a_log, dt_bias, onorm_g, rel_bias, w_out, g_mix_pre, g_mix_post,
           w_gate, w_up, w_down, g_ffn_pre, g_ffn_post):
    batch, seq, d_model = x.shape
    x2d = x.reshape(batch * seq, d_model)
    for l in range(w_in.shape[0]):
        x2d = _layer(x2d, batch, seq, w_in[l], conv_w[l], a_log[l], dt_bias[l], onorm_g[l],
                     rel_bias, w_out[l], g_mix_pre[l], g_mix_post[l], w_gate[l], w_up[l],
                     w_down[l], g_ffn_pre[l], g_ffn_post[l])
    return x2d.reshape(batch, seq, d_model)
```

```python
import functools
import math

import jax
import jax.numpy as jnp
from jax import lax
from jax.experimental import pallas as pl
from jax.experimental.pallas import tpu as pltpu

F32 = jnp.float32
BF16 = jnp.bfloat16

GDN_HEADS = 4
GDN_HEAD_DIM = 128
GDN_WIDTH = GDN_HEADS * GDN_HEAD_DIM
CONV_WIDTH = 4
CHUNK = 64
SWA_HEADS = 8
SWA_HEAD_DIM = 64
SWA_WIDTH = SWA_HEADS * SWA_HEAD_DIM
DILATED_PATTERNS = ((128, 1), (512, 4), (2048, 16))
NUM_BUCKETS = 32
MAX_DISTANCE = 2048
RMS_EPS = 1e-6

LANES = 128
SUBLANES = 8
VMEM_LIMIT_BYTES = 56 * 1024 * 1024

NT_DIMS = (((1,), (1,)), ((), ()))
TN_DIMS = (((0,), (0,)), ((), ()))


def _rms(x, g):
    return x * lax.rsqrt(jnp.mean(x * x, axis=-1, keepdims=True) + RMS_EPS) * g


def _silu(x):
    return x * jax.nn.sigmoid(x)


def _softplus(x):
    return jnp.maximum(x, 0.0) + jnp.log1p(jnp.exp(-jnp.abs(x)))


def _resident(shape):
    zeros = (0,) * len(shape)
    return pl.BlockSpec(shape, lambda *_: zeros, pipeline_mode=pl.Buffered(1))


def _inproj_kernel(x_ref, g_ref, wqkv_ref, wgate_ref, wq_ref, wk_ref, wv_ref, ws_ref,
                   qkv_ref, gate_ref, qb_ref, kb_ref, vb_ref, sm_ref):
    h = _rms(x_ref[...], g_ref[...]).astype(BF16)
    for w_ref, o_ref in ((wqkv_ref, qkv_ref), (wgate_ref, gate_ref), (ws_ref, sm_ref)):
        o_ref[...] = jnp.dot(h, w_ref[...], preferred_element_type=F32)
    for w_ref, o_ref in ((wq_ref, qb_ref), (wk_ref, kb_ref), (wv_ref, vb_ref)):
        res = jnp.dot(h, w_ref[...], preferred_element_type=F32)
        for slab in range(SWA_WIDTH // LANES):
            o_ref[slab] = res[:, slab * LANES:(slab + 1) * LANES]


def _inproj(x2d, g, wqkv, wgate, wq, wk, wv, ws, tm):
    T, D = x2d.shape
    row = lambda n: pl.BlockSpec((tm, n), lambda i: (i, 0))
    n_slabs = SWA_WIDTH // LANES
    slabs = pl.BlockSpec((n_slabs, tm, LANES), lambda i: (0, i, 0))
    slab_shape = jax.ShapeDtypeStruct((n_slabs, T, LANES), F32)
    return pl.pallas_call(
        _inproj_kernel,
        grid=(T // tm,),
        in_specs=[row(D), _resident((1, D))] + [_resident(w.shape) for w in (wqkv, wgate, wq, wk, wv, ws)],
        out_specs=[row(3 * GDN_WIDTH), row(GDN_WIDTH), slabs, slabs, slabs, row(LANES)],
        out_shape=[jax.ShapeDtypeStruct((T, 3 * GDN_WIDTH), F32),
                   jax.ShapeDtypeStruct((T, GDN_WIDTH), F32), slab_shape, slab_shape, slab_shape,
                   jax.ShapeDtypeStruct((T, LANES), F32)],
        compiler_params=pltpu.CompilerParams(dimension_semantics=("arbitrary",),
                                             vmem_limit_bytes=VMEM_LIMIT_BYTES),
        name="inproj",
    )(x2d, g, wqkv, wgate, wq, wk, wv, ws)


def _unit_lower_inverses(mats, eye, xor_ij):
    n = mats[0].shape[0]
    ds = [eye - jnp.where(xor_ij < 2, a, 0.0) for a in mats]
    s = 2
    while s < n:
        band = jnp.logical_and(xor_ij >= s, xor_ij < 2 * s)
        dbs = [d.astype(BF16) for d in ds]
        eds = [jnp.dot(jnp.where(band, a, 0.0).astype(BF16), db,
                       preferred_element_type=F32).astype(BF16) for a, db in zip(mats, dbs)]
        ds = [d - jnp.dot(db, ed, preferred_element_type=F32) for d, db, ed in zip(ds, dbs, eds)]
        s *= 2
    return ds


def _gdn_kernel(qkv_ref, gate_ref, sm_ref, convw_ref, alog_ref, dtb_ref, ong_ref, ltri_ref,
                o_ref, ext_scr, q_scr, k_scr, v_scr, g_scr, beta_scr, state_scr, *, tb, chunk):
    width = 3 * GDN_WIDTH
    hd = GDN_HEAD_DIM
    halo = SUBLANES

    @pl.when(pl.program_id(1) == 0)
    def _():
        ext_scr[0:halo, :] = jnp.zeros((halo, width), F32)
        state_scr[...] = jnp.zeros_like(state_scr)

    @pl.when(pl.program_id(1) > 0)
    def _():
        ext_scr[0:halo, :] = ext_scr[tb:tb + halo, :]

    ext_scr[halo:halo + tb, :] = qkv_ref[...]

    for j in range(width // hd):
        cols = slice(j * hd, (j + 1) * hd)
        acc = None
        for i in range(CONV_WIDTH):
            start = halo - (CONV_WIDTH - 1) + i
            term = convw_ref[i:i + 1, cols] * ext_scr[start:start + tb, cols]
            acc = term if acc is None else acc + term
        y = _silu(acc)
        kind, head = divmod(j, GDN_HEADS)
        hcols = slice(head * hd, (head + 1) * hd)
        if kind < 2:
            y = y * lax.rsqrt(jnp.sum(y * y, axis=-1, keepdims=True) + 1e-6)
        if kind == 0:
            q_scr[:, hcols] = y * (hd ** -0.5)
        elif kind == 1:
            k_scr[:, hcols] = y
        else:
            v_scr[:, hcols] = y

    sm = sm_ref[...]
    beta_scr[...] = jax.nn.sigmoid(sm)
    g_scr[...] = -jnp.exp(alog_ref[...]) * _softplus(sm + dtb_ref[...])

    ii = lax.broadcasted_iota(jnp.int32, (chunk, chunk), 0)
    jj = lax.broadcasted_iota(jnp.int32, (chunk, chunk), 1)
    xor_ij = jnp.bitwise_xor(ii, jj)
    lower = ii >= jj
    strict = ii > jj
    eye = jnp.where(ii == jj, 1.0, 0.0).astype(F32)
    ltri = ltri_ref[...]
    ong = ong_ref[...]

    nc = tb // chunk
    heads = range(GDN_HEADS)
    rows = [slice(c * chunk, (c + 1) * chunk) for c in range(nc)]
    hcols = [slice(h * hd, (h + 1) * hd) for h in heads]
    probs = [(c, h) for c in range(nc) for h in heads]

    gcs = []
    for c in range(nc):
        g = g_scr[rows[c], :]
        g_hi = g.astype(BF16)
        rem = g - g_hi.astype(F32)
        g_mid = rem.astype(BF16)
        g_lo = (rem - g_mid.astype(F32)).astype(BF16)
        gcs.append(jnp.dot(ltri, g_hi, preferred_element_type=F32)
                   + jnp.dot(ltri, g_mid, preferred_element_type=F32)
                   + jnp.dot(ltri, g_lo, preferred_element_type=F32))
    gcts = [gc.T for gc in gcs]
    lane_of = lambda h: slice(GDN_HEADS + h, GDN_HEADS + h + 1)
    gcol = {(c, h): gcs[c][:, lane_of(h)] for c, h in probs}
    glast = {(c, h): gcs[c][chunk - 1:chunk, lane_of(h)] for c, h in probs}
    bcol = {(c, h): beta_scr[rows[c], h:h + 1] for c, h in probs}

    qks = {}
    for c, h in probs:
        kcb = k_scr[rows[c], hcols[h]].astype(BF16)
        qkb = jnp.concatenate([q_scr[rows[c], hcols[h]].astype(BF16), kcb], axis=0)
        qks[c, h] = lax.dot_general(qkb, kcb, NT_DIMS, preferred_element_type=F32)
    attn, a_mats = {}, []
    for c, h in probs:
        grow = gcts[c][lane_of(h), :]
        decay = jnp.exp(jnp.where(lower, gcol[c, h] - grow, -jnp.inf))
        attn[c, h] = (qks[c, h][:chunk] * decay).astype(BF16)
        a_mats.append(jnp.where(strict, qks[c, h][chunk:] * decay * bcol[c, h], 0.0))
    t_invs = _unit_lower_inverses(a_mats, eye, xor_ij)

    u, wq, k_dec = {}, {}, {}
    for (c, h), t_inv in zip(probs, t_invs):
        kc = k_scr[rows[c], hcols[h]]
        eg = jnp.exp(gcol[c, h])
        rhs = jnp.concatenate([v_scr[rows[c], hcols[h]] * bcol[c, h], kc * (bcol[c, h] * eg)],
                              axis=1).astype(BF16)
        uw = jnp.dot(t_inv.astype(BF16), rhs, preferred_element_type=F32)
        u[c, h] = uw[:, :hd]
        wq[c, h] = jnp.concatenate([uw[:, hd:], q_scr[rows[c], hcols[h]] * eg], axis=0).astype(BF16)
        k_dec[c, h] = (kc * jnp.exp(glast[c, h] - gcol[c, h])).astype(BF16)

    state = [state_scr[h] for h in heads]
    for c in range(nc):
        ws = [jnp.dot(wq[c, h], state[h].astype(BF16), preferred_element_type=F32) for h in heads]
        v_new = [(u[c, h] - ws[h][:chunk]).astype(BF16) for h in heads]
        o = [ws[h][chunk:] + jnp.dot(attn[c, h], v_new[h], preferred_element_type=F32) for h in heads]
        state = [state[h] * jnp.exp(glast[c, h])
                 + lax.dot_general(k_dec[c, h], v_new[h], TN_DIMS, preferred_element_type=F32)
                 for h in heads]
        for h in heads:
            gated = _rms(o[h], ong) * _silu(gate_ref[rows[c], hcols[h]])
            o_ref[rows[c], hcols[h]] = gated.astype(o_ref.dtype)
    for h in heads:
        state_scr[h] = state[h]


def _gdn(qkv, gate, sm, conv_w, alog_row, dtb_row, onorm_g, batch, seq, tb):
    T = batch * seq
    nb = seq // tb
    width = 3 * GDN_WIDTH
    idx = jnp.arange(CHUNK)
    ltri = (idx[:, None] >= idx[None, :]).astype(BF16)
    row = lambda n: pl.BlockSpec((tb, n), lambda b, t: (b * nb + t, 0))
    kern = functools.partial(_gdn_kernel, tb=tb, chunk=CHUNK)
    return pl.pallas_call(
        kern,
        grid=(batch, nb),
        in_specs=[row(width), row(GDN_WIDTH), row(LANES), _resident(conv_w.shape),
                  _resident((1, LANES)), _resident((1, LANES)), _resident((1, GDN_HEAD_DIM)),
                  _resident((CHUNK, CHUNK))],
        out_specs=row(GDN_WIDTH),
        out_shape=jax.ShapeDtypeStruct((T, GDN_WIDTH), BF16),
        scratch_shapes=[pltpu.VMEM((tb + 2 * SUBLANES, width), F32),
                        pltpu.VMEM((tb, GDN_WIDTH), F32),
                        pltpu.VMEM((tb, GDN_WIDTH), F32),
                        pltpu.VMEM((tb, GDN_WIDTH), F32),
                        pltpu.VMEM((tb, LANES), F32),
                        pltpu.VMEM((tb, LANES), F32),
                        pltpu.VMEM((GDN_HEADS, GDN_HEAD_DIM, GDN_HEAD_DIM), F32)],
        compiler_params=pltpu.CompilerParams(dimension_semantics=("arbitrary", "arbitrary"),
                                             vmem_limit_bytes=VMEM_LIMIT_BYTES),
        name="gdn",
    )(qkv, gate, sm, conv_w, alog_row, dtb_row, onorm_g, ltri)


def _swa_kernel(q_ref, k_ref, v_ref, bias_ref, sel_ref, o_ref, acc_scr, m_scr, s_scr, *, tq, w):
    base = pl.program_id(1) * tq
    n_slabs = SWA_WIDTH // LANES
    heads_per_slab = LANES // SWA_HEAD_DIM
    assert heads_per_slab == 2
    scale = SWA_HEAD_DIM ** -0.5
    col = lax.broadcasted_iota(jnp.int32, (w, 2 * w), 1)
    lane = lax.broadcasted_iota(jnp.int32, (w, LANES), 1)
    low_half = lane < SWA_HEAD_DIM

    def expand(packed):
        hi = packed.astype(BF16)
        lo = (packed - hi.astype(F32)).astype(BF16)
        return jnp.dot(jnp.concatenate([hi, lo], axis=1), sel_ref[...], preferred_element_type=F32)

    def task(pat, dil, t):
        first = pat == 0
        r = jnp.bitwise_and(t, dil - 1)
        blk = jnp.right_shift(t, dil.bit_length() - 1)
        span = w * dil
        q0 = r + blk * span
        n_glob = pl.program_id(1) * (tq // span) + blk
        cur0 = base + q0
        prev0 = jnp.where(n_glob > 0, cur0 - span, cur0)
        no_prev = jnp.logical_and(n_glob == 0, col < w)
        rows_at = lambda s: pl.ds(s, w, stride=dil) if dil > 1 else pl.ds(s, w)
        qrows = rows_at(q0)
        logits, vps = [], []
        for slab in range(n_slabs):
            qp = (q_ref[slab, qrows, :] * scale).astype(BF16)
            kp = jnp.concatenate([k_ref[slab, rows_at(prev0), :], k_ref[slab, rows_at(cur0), :]],
                                 axis=0).astype(BF16)
            vps.append(jnp.concatenate([v_ref[slab, rows_at(prev0), :],
                                        v_ref[slab, rows_at(cur0), :]], axis=0).astype(BF16))
            for e in range(heads_per_slab):
                mine = low_half if e == 0 else jnp.logical_not(low_half)
                qh = jnp.where(mine, qp, jnp.zeros_like(qp))
                lg = lax.dot_general(qh, kp, NT_DIMS, preferred_element_type=F32)
                lg = lg + bias_ref[pat, slab * heads_per_slab + e]
                logits.append(jnp.where(no_prev, -jnp.inf, lg))
        m_blk = jnp.zeros((w, LANES), F32)
        s_blk = jnp.zeros((w, LANES), F32)
        ps = []
        for h, lg in enumerate(logits):
            m_h = jnp.max(lg, axis=-1, keepdims=True)
            p = jnp.exp(lg - m_h)
            ps.append(p.astype(BF16))
            m_blk = jnp.where(lane == h, m_h, m_blk)
            s_blk = jnp.where(lane == h, jnp.sum(p, axis=-1, keepdims=True), s_blk)
        nums = []
        for slab in range(n_slabs):
            h0 = slab * heads_per_slab
            outs = [jnp.dot(ps[h0 + e], vps[slab], preferred_element_type=F32)
                    for e in range(heads_per_slab)]
            nums.append(jnp.where(low_half, outs[0], outs[1]))
        if first:
            m_new, s_new = m_blk, s_blk
        else:
            m_old = m_scr[qrows, :]
            m_new = jnp.maximum(m_old, m_blk)
            a_old = jnp.exp(m_old - m_new)
            a_blk = jnp.exp(m_blk - m_new)
            s_new = a_old * s_scr[qrows, :] + a_blk * s_blk
            w_old, w_blk = expand(a_old), expand(a_blk)
            for slab in range(n_slabs):
                cols = slice(slab * LANES, (slab + 1) * LANES)
                nums[slab] = w_old[:, cols] * acc_scr[slab, qrows, :] + w_blk[:, cols] * nums[slab]
        for slab in range(n_slabs):
            acc_scr[slab, qrows, :] = nums[slab]
        m_scr[qrows, :] = m_new
        s_scr[qrows, :] = s_new

    for pat, (window, dil) in enumerate(DILATED_PATTERNS):
        assert window // dil == w and tq % (w * dil) == 0

        def body(t, carry, pat=pat, dil=dil):
            task(pat, dil, t)
            return carry
        lax.fori_loop(0, tq // w, body, 0)

    def finish(j, carry):
        rows = pl.ds(pl.multiple_of(j * w, w), w)
        den = expand(s_scr[rows, :])
        for slab in range(n_slabs):
            cols = slice(slab * LANES, (slab + 1) * LANES)
            o_ref[rows, cols] = (acc_scr[slab, rows, :] / den[:, cols]).astype(o_ref.dtype)
        return carry
    lax.fori_loop(0, tq // w, finish, 0)


def _swa(qb, kb, vb, bias, batch, seq, tq):
    n_slabs = SWA_WIDTH // LANES
    w = DILATED_PATTERNS[0][0] // DILATED_PATTERNS[0][1]
    nq = seq // tq
    q_blk = pl.BlockSpec((n_slabs, tq, LANES), lambda b, i: (0, b * nq + i, 0))
    seq_blk = pl.BlockSpec((n_slabs, seq, LANES), lambda b, i: (0, b, 0),
                           pipeline_mode=pl.Buffered(1))
    head_of_col = jnp.arange(SWA_WIDTH) // SWA_HEAD_DIM
    sel = (jnp.arange(LANES)[:, None] == head_of_col[None, :]).astype(BF16)
    sel = jnp.concatenate([sel, sel], axis=0)
    return pl.pallas_call(
        functools.partial(_swa_kernel, tq=tq, w=w),
        grid=(batch, nq),
        in_specs=[q_blk, seq_blk, seq_blk, _resident(bias.shape), _resident(sel.shape)],
        out_specs=pl.BlockSpec((tq, SWA_WIDTH), lambda b, i: (b * nq + i, 0)),
        out_shape=jax.ShapeDtypeStruct((batch * seq, SWA_WIDTH), BF16),
        scratch_shapes=[pltpu.VMEM((n_slabs, tq, LANES), F32), pltpu.VMEM((tq, LANES), F32),
                        pltpu.VMEM((tq, LANES), F32)],
        compiler_params=pltpu.CompilerParams(dimension_semantics=("arbitrary", "arbitrary"),
                                             vmem_limit_bytes=VMEM_LIMIT_BYTES),
        name="swa",
    )(qb, kb, vb, bias, sel)


def _t5_causal_bucket(dist):
    max_exact = NUM_BUCKETS // 2
    d = jnp.maximum(dist, 1).astype(F32)
    log_b = max_exact + (jnp.log(d / max_exact) / math.log(MAX_DISTANCE / max_exact)
                         * (NUM_BUCKETS - max_exact)).astype(jnp.int32)
    return jnp.where(dist < max_exact, dist, jnp.minimum(log_b, NUM_BUCKETS - 1))


def _band_bias(rel_bias, window, dilation):
    w = window // dilation
    steps = rel_bias[_t5_causal_bucket(jnp.arange(w + 1) * dilation)].astype(F32).T
    n_heads = steps.shape[0]
    pad = jnp.full((n_heads, w - 1), -jnp.inf, F32)
    by_rel = jnp.concatenate([pad, steps, pad], axis=1)
    rev = jnp.concatenate([by_rel[:, ::-1], jnp.full((n_heads, 1), -jnp.inf, F32)], axis=1)
    skew = jnp.tile(rev, (1, w))[:, :w * (3 * w - 1)].reshape(n_heads, w, 3 * w - 1)
    return skew[:, :, w - 1:]


def _mixffn_kernel(x_ref, oa_ref, ob_ref, woa_ref, wob_ref, gpost_ref, gpre_ref, gfpost_ref,
                   wg_ref, wu_ref, wd_ref, out_ref, act_scr, *, ff_chunk):
    mix = (jnp.dot(oa_ref[...], woa_ref[...], preferred_element_type=F32)
           + jnp.dot(ob_ref[...], wob_ref[...], preferred_element_type=F32))
    x1 = x_ref[...] + _rms(mix, gpost_ref[...])
    h = _rms(x1, gpre_ref[...]).astype(BF16)
    d_ff = wg_ref.shape[1]
    for c in range(d_ff // ff_chunk):
        cols = slice(c * ff_chunk, (c + 1) * ff_chunk)
        gate = jnp.dot(h, wg_ref[:, cols], preferred_element_type=F32)
        up = jnp.dot(h, wu_ref[:, cols], preferred_element_type=F32)
        act_scr[:, cols] = (_silu(gate) * up).astype(BF16)
    f = jnp.dot(act_scr[...], wd_ref[...], preferred_element_type=F32)
    out_ref[...] = x1 + _rms(f, gfpost_ref[...])


def _mixffn(x2d, oa, ob, woa, wob, gpost, gpre, gfpost, wg, wu, wd, tm, ff_chunk):
    T, D = x2d.shape
    d_ff = wg.shape[1]
    row = lambda n: pl.BlockSpec((tm, n), lambda i: (i, 0))
    return pl.pallas_call(
        functools.partial(_mixffn_kernel, ff_chunk=ff_chunk),
        grid=(T // tm,),
        in_specs=([row(D), row(GDN_WIDTH), row(SWA_WIDTH)]
                  + [_resident(woa.shape), _resident(wob.shape)] + [_resident((1, D))] * 3
                  + [_resident(wg.shape), _resident(wu.shape), _resident(wd.shape)]),
        out_specs=row(D),
        out_shape=jax.ShapeDtypeStruct((T, D), F32),
        scratch_shapes=[pltpu.VMEM((tm, d_ff), BF16)],
        compiler_params=pltpu.CompilerParams(dimension_semantics=("arbitrary",),
                                             vmem_limit_bytes=VMEM_LIMIT_BYTES),
        name="mixffn",
    )(x2d, oa, ob, woa, wob, gpost, gpre, gfpost, wg, wu, wd)


def _layer(x2d, batch, seq, w_in, conv_w, a_log, dt_bias, onorm_g, rel_bias, w_out,
           g_mix_pre, g_mix_post, w_gate, w_up, w_down, g_ffn_pre, g_ffn_post):
    D = x2d.shape[1]
    gw, sw, nh = GDN_WIDTH, SWA_WIDTH, GDN_HEADS
    wb = w_in.astype(BF16)
    c_gate, c_small, c_q = 3 * gw, 4 * gw, 4 * gw + 2 * nh
    w_small = jnp.pad(wb[:, c_small:c_q], ((0, 0), (0, LANES - 2 * nh)))
    qkv, gate, qb, kb, vb, sm = _inproj(
        x2d, g_mix_pre.reshape(1, D), wb[:, :c_gate], wb[:, c_gate:c_small],
        wb[:, c_q:c_q + sw], wb[:, c_q + sw:c_q + 2 * sw], wb[:, c_q + 2 * sw:], w_small, tm=512)

    lane_pad = lambda v: jnp.pad(v.astype(F32).reshape(1, nh), ((0, 0), (nh, LANES - 2 * nh)))
    oa = _gdn(qkv, gate, sm, conv_w.astype(F32), lane_pad(a_log), lane_pad(dt_bias),
              onorm_g.astype(F32).reshape(1, GDN_HEAD_DIM), batch, seq, tb=512)

    bias = jnp.stack([_band_bias(rel_bias, win, dil) for win, dil in DILATED_PATTERNS])
    ob = _swa(qb, kb, vb, bias, batch, seq, tq=2048)

    wo = w_out.astype(BF16)
    return _mixffn(x2d, oa, ob, wo[:gw], wo[gw:], g_mix_post.reshape(1, D),
                   g_ffn_pre.reshape(1, D), g_ffn_post.reshape(1, D), w_gate.astype(BF16),
                   w_up.astype(BF16), w_down.astype(BF16), tm=512, ff_chunk=256)


def kernel(x, w_in, conv_w, a_log, dt_bias, onorm_g, rel_bias, w_out, g_mix_pre, g_mix_post,
           w_gate, w_up, w_down, g_ffn_pre, g_ffn_post):
    batch, seq, d_model = x.shape
    x2d = x.reshape(batch * seq, d_model)
    for l in range(w_in.shape[0]):
        x2d = _layer(x2d, batch, seq, w_in[l], conv_w[l], a_log[l], dt_bias[l], onorm_g[l],
                     rel_bias, w_out[l], g_mix_pre[l], g_mix_post[l], w_gate[l], w_up[l],
                     w_down[l], g_ffn_pre[l], g_ffn_post[l])
    return x2d.reshape(batch, seq, d_model)
```

```python
import functools
import math

import jax
import jax.numpy as jnp
from jax import lax
from jax.experimental import pallas as pl
from jax.experimental.pallas import tpu as pltpu

F32 = jnp.float32
BF16 = jnp.bfloat16

GDN_HEADS = 4
GDN_HEAD_DIM = 128
GDN_WIDTH = GDN_HEADS * GDN_HEAD_DIM
CONV_WIDTH = 4
CHUNK = 64
SWA_HEADS = 8
SWA_HEAD_DIM = 64
SWA_WIDTH = SWA_HEADS * SWA_HEAD_DIM
DILATED_PATTERNS = ((128, 1), (512, 4), (2048, 16))
SWA_RESIDUES = max(dil for _, dil in DILATED_PATTERNS)
NUM_BUCKETS = 32
MAX_DISTANCE = 2048
RMS_EPS = 1e-6

LANES = 128
SUBLANES = 8
VMEM_LIMIT_BYTES = 56 * 1024 * 1024

NT_DIMS = (((1,), (1,)), ((), ()))
TN_DIMS = (((0,), (0,)), ((), ()))


def _rms(x, g):
    return x * lax.rsqrt(jnp.mean(x * x, axis=-1, keepdims=True) + RMS_EPS) * g


def _silu(x):
    return x * jax.nn.sigmoid(x)


def _softplus(x):
    return jnp.maximum(x, 0.0) + jnp.log1p(jnp.exp(-jnp.abs(x)))


def _resident(shape):
    zeros = (0,) * len(shape)
    return pl.BlockSpec(shape, lambda *_: zeros, pipeline_mode=pl.Buffered(1))


def _inproj_kernel(x_ref, g_ref, wqkv_ref, wgate_ref, wq_ref, wk_ref, wv_ref, ws_ref,
                   qkv_ref, gate_ref, qb_ref, kb_ref, vb_ref, sm_ref, stage_scr):
    h = _rms(x_ref[...], g_ref[...]).astype(BF16)
    for w_ref, o_ref in ((wqkv_ref, qkv_ref), (wgate_ref, gate_ref), (ws_ref, sm_ref)):
        o_ref[...] = jnp.dot(h, w_ref[...], preferred_element_type=F32)
    tm = x_ref.shape[0]
    for w_ref, o_ref in ((wq_ref, qb_ref), (wk_ref, kb_ref), (wv_ref, vb_ref)):
        res = jnp.dot(h, w_ref[...], preferred_element_type=F32)
        for slab in range(SWA_WIDTH // LANES):
            stage_scr[slab] = res[:, slab * LANES:(slab + 1) * LANES]
        for slab in range(SWA_WIDTH // LANES):
            for r in range(SWA_RESIDUES):
                o_ref[slab, r] = stage_scr[slab, pl.ds(r, tm // SWA_RESIDUES, stride=SWA_RESIDUES), :]


def _inproj(x2d, g, wqkv, wgate, wq, wk, wv, ws, batch, seq, tm):
    T, D = x2d.shape
    nt = seq // tm
    row = lambda n: pl.BlockSpec((tm, n), lambda b, j: (b * nt + j, 0))
    n_slabs = SWA_WIDTH // LANES
    slabs = pl.BlockSpec((n_slabs, None, SWA_RESIDUES, tm // SWA_RESIDUES, LANES),
                         lambda b, j: (0, b, 0, j, 0))
    slab_shape = jax.ShapeDtypeStruct(
        (n_slabs, batch, SWA_RESIDUES, seq // SWA_RESIDUES, LANES), F32)
    return pl.pallas_call(
        _inproj_kernel,
        grid=(batch, nt),
        in_specs=[row(D), _resident((1, D))] + [_resident(w.shape) for w in (wqkv, wgate, wq, wk, wv, ws)],
        out_specs=[row(3 * GDN_WIDTH), row(GDN_WIDTH), slabs, slabs, slabs, row(LANES)],
        out_shape=[jax.ShapeDtypeStruct((T, 3 * GDN_WIDTH), F32),
                   jax.ShapeDtypeStruct((T, GDN_WIDTH), F32), slab_shape, slab_shape, slab_shape,
                   jax.ShapeDtypeStruct((T, LANES), F32)],
        scratch_shapes=[pltpu.VMEM((n_slabs, tm, LANES), F32)],
        compiler_params=pltpu.CompilerParams(dimension_semantics=("arbitrary", "arbitrary"),
                                             vmem_limit_bytes=VMEM_LIMIT_BYTES),
        name="inproj",
    )(x2d, g, wqkv, wgate, wq, wk, wv, ws)


def _unit_lower_inverses(mats, eye, xor_ij):
    n = mats[0].shape[0]
    ds = [eye - jnp.where(xor_ij < 2, a, 0.0) for a in mats]
    s = 2
    while s < n:
        band = jnp.logical_and(xor_ij >= s, xor_ij < 2 * s)
        dbs = [d.astype(BF16) for d in ds]
        eds = [jnp.dot(jnp.where(band, a, 0.0).astype(BF16), db,
                       preferred_element_type=F32).astype(BF16) for a, db in zip(mats, dbs)]
        ds = [d - jnp.dot(db, ed, preferred_element_type=F32) for d, db, ed in zip(ds, dbs, eds)]
        s *= 2
    return ds


def _gdn_kernel(qkv_ref, gate_ref, sm_ref, convw_ref, alog_ref, dtb_ref, ong_ref, ltri_ref,
                o_ref, ext_scr, q_scr, k_scr, v_scr, g_scr, beta_scr, state_scr, *, tb, chunk):
    width = 3 * GDN_WIDTH
    hd = GDN_HEAD_DIM
    halo = SUBLANES

    @pl.when(pl.program_id(1) == 0)
    def _():
        ext_scr[0:halo, :] = jnp.zeros((halo, width), F32)
        state_scr[...] = jnp.zeros_like(state_scr)

    @pl.when(pl.program_id(1) > 0)
    def _():
        ext_scr[0:halo, :] = ext_scr[tb:tb + halo, :]

    ext_scr[halo:halo + tb, :] = qkv_ref[...]

    for j in range(width // hd):
        cols = slice(j * hd, (j + 1) * hd)
        acc = None
        for i in range(CONV_WIDTH):
            start = halo - (CONV_WIDTH - 1) + i
            term = convw_ref[i:i + 1, cols] * ext_scr[start:start + tb, cols]
            acc = term if acc is None else acc + term
        y = _silu(acc)
        kind, head = divmod(j, GDN_HEADS)
        hcols = slice(head * hd, (head + 1) * hd)
        if kind < 2:
            y = y * lax.rsqrt(jnp.sum(y * y, axis=-1, keepdims=True) + 1e-6)
        if kind == 0:
            q_scr[:, hcols] = y * (hd ** -0.5)
        elif kind == 1:
            k_scr[:, hcols] = y
        else:
            v_scr[:, hcols] = y

    sm = sm_ref[...]
    beta_scr[...] = jax.nn.sigmoid(sm)
    g_scr[...] = -jnp.exp(alog_ref[...]) * _softplus(sm + dtb_ref[...])

    ii = lax.broadcasted_iota(jnp.int32, (chunk, chunk), 0)
    jj = lax.broadcasted_iota(jnp.int32, (chunk, chunk), 1)
    xor_ij = jnp.bitwise_xor(ii, jj)
    lower = ii >= jj
    strict = ii > jj
    eye = jnp.where(ii == jj, 1.0, 0.0).astype(F32)
    ltri = ltri_ref[...]
    ong = ong_ref[...]

    nc = tb // chunk
    heads = range(GDN_HEADS)
    rows = [slice(c * chunk, (c + 1) * chunk) for c in range(nc)]
    hcols = [slice(h * hd, (h + 1) * hd) for h in heads]
    probs = [(c, h) for c in range(nc) for h in heads]

    gcs = []
    for c in range(nc):
        g = g_scr[rows[c], :]
        g_hi = g.astype(BF16)
        rem = g - g_hi.astype(F32)
        g_mid = rem.astype(BF16)
        g_lo = (rem - g_mid.astype(F32)).astype(BF16)
        gcs.append(jnp.dot(ltri, g_hi, preferred_element_type=F32)
                   + jnp.dot(ltri, g_mid, preferred_element_type=F32)
                   + jnp.dot(ltri, g_lo, preferred_element_type=F32))
    gcts = [gc.T for gc in gcs]
    lane_of = lambda h: slice(GDN_HEADS + h, GDN_HEADS + h + 1)
    gcol = {(c, h): gcs[c][:, lane_of(h)] for c, h in probs}
    glast = {(c, h): gcs[c][chunk - 1:chunk, lane_of(h)] for c, h in probs}
    bcol = {(c, h): beta_scr[rows[c], h:h + 1] for c, h in probs}

    qks = {}
    for c, h in probs:
        kcb = k_scr[rows[c], hcols[h]].astype(BF16)
        qkb = jnp.concatenate([q_scr[rows[c], hcols[h]].astype(BF16), kcb], axis=0)
        qks[c, h] = lax.dot_general(qkb, kcb, NT_DIMS, preferred_element_type=F32)
    attn, a_mats = {}, []
    for c, h in probs:
        grow = gcts[c][lane_of(h), :]
        decay = jnp.exp(jnp.where(lower, gcol[c, h] - grow, -jnp.inf))
        attn[c, h] = (qks[c, h][:chunk] * decay).astype(BF16)
        a_mats.append(jnp.where(strict, qks[c, h][chunk:] * decay * bcol[c, h], 0.0))
    t_invs = _unit_lower_inverses(a_mats, eye, xor_ij)

    u, wq, k_dec = {}, {}, {}
    for (c, h), t_inv in zip(probs, t_invs):
        kc = k_scr[rows[c], hcols[h]]
        eg = jnp.exp(gcol[c, h])
        rhs = jnp.concatenate([v_scr[rows[c], hcols[h]] * bcol[c, h], kc * (bcol[c, h] * eg)],
                              axis=1).astype(BF16)
        uw = jnp.dot(t_inv.astype(BF16), rhs, preferred_element_type=F32)
        u[c, h] = uw[:, :hd]
        wq[c, h] = jnp.concatenate([uw[:, hd:], q_scr[rows[c], hcols[h]] * eg], axis=0).astype(BF16)
        k_dec[c, h] = (kc * jnp.exp(glast[c, h] - gcol[c, h])).astype(BF16)

    state = [state_scr[h] for h in heads]
    for c in range(nc):
        ws = [jnp.dot(wq[c, h], state[h].astype(BF16), preferred_element_type=F32) for h in heads]
        v_new = [(u[c, h] - ws[h][:chunk]).astype(BF16) for h in heads]
        o = [ws[h][chunk:] + jnp.dot(attn[c, h], v_new[h], preferred_element_type=F32) for h in heads]
        state = [state[h] * jnp.exp(glast[c, h])
                 + lax.dot_general(k_dec[c, h], v_new[h], TN_DIMS, preferred_element_type=F32)
                 for h in heads]
        for h in heads:
            gated = _rms(o[h], ong) * _silu(gate_ref[rows[c], hcols[h]])
            o_ref[rows[c], hcols[h]] = gated.astype(o_ref.dtype)
    for h in heads:
        state_scr[h] = state[h]


def _gdn(qkv, gate, sm, conv_w, alog_row, dtb_row, onorm_g, batch, seq, tb):
    T = batch * seq
    nb = seq // tb
    width = 3 * GDN_WIDTH
    idx = jnp.arange(CHUNK)
    ltri = (idx[:, None] >= idx[None, :]).astype(BF16)
    row = lambda n: pl.BlockSpec((tb, n), lambda b, t: (b * nb + t, 0))
    kern = functools.partial(_gdn_kernel, tb=tb, chunk=CHUNK)
    return pl.pallas_call(
        kern,
        grid=(batch, nb),
        in_specs=[row(width), row(GDN_WIDTH), row(LANES), _resident(conv_w.shape),
                  _resident((1, LANES)), _resident((1, LANES)), _resident((1, GDN_HEAD_DIM)),
                  _resident((CHUNK, CHUNK))],
        out_specs=row(GDN_WIDTH),
        out_shape=jax.ShapeDtypeStruct((T, GDN_WIDTH), BF16),
        scratch_shapes=[pltpu.VMEM((tb + 2 * SUBLANES, width), F32),
                        pltpu.VMEM((tb, GDN_WIDTH), F32),
                        pltpu.VMEM((tb, GDN_WIDTH), F32),
                        pltpu.VMEM((tb, GDN_WIDTH), F32),
                        pltpu.VMEM((tb, LANES), F32),
                        pltpu.VMEM((tb, LANES), F32),
                        pltpu.VMEM((GDN_HEADS, GDN_HEAD_DIM, GDN_HEAD_DIM), F32)],
        compiler_params=pltpu.CompilerParams(dimension_semantics=("arbitrary", "arbitrary"),
                                             vmem_limit_bytes=VMEM_LIMIT_BYTES),
        name="gdn",
    )(qkv, gate, sm, conv_w, alog_row, dtb_row, onorm_g, ltri)


def _swa_kernel(q_ref, k_ref, v_ref, bias_ref, sel_ref, o_ref, acc_scr, m_scr, s_scr, out_scr,
                *, w):
    n_res = SWA_RESIDUES
    tile = pl.program_id(1)
    n_slabs = SWA_WIDTH // LANES
    heads_per_slab = LANES // SWA_HEAD_DIM
    assert heads_per_slab == 2
    scale = SWA_HEAD_DIM ** -0.5 * math.log2(math.e)
    lane = lax.broadcasted_iota(jnp.int32, (w, LANES), 1)
    low_half = lane < SWA_HEAD_DIM

    def expand(packed):
        hi = packed.astype(BF16)
        lo = (packed - hi.astype(F32)).astype(BF16)
        return jnp.dot(jnp.concatenate([hi, lo], axis=1), sel_ref[...], preferred_element_type=F32)

    def task(pat, dil, t):
        first = pat == 0
        n_chunks = n_res // dil
        c = w // n_chunks
        res_d = jnp.bitwise_and(t, dil - 1)
        blk = jnp.right_shift(t, dil.bit_length() - 1)
        n_glob = tile * n_chunks + blk
        q0 = pl.multiple_of(blk * c, c)
        cur0 = pl.multiple_of(n_glob * c, c)
        prev0 = pl.multiple_of(jnp.maximum(n_glob - 1, 0) * c, c)
        is_first_blk = (n_glob == 0).astype(jnp.int32)
        res_of = [a * dil + res_d for a in range(n_chunks)]

        def gather(ref, lead, start):
            return jnp.concatenate([ref[lead + (res, pl.ds(start, c), slice(None))]
                                    for res in res_of], axis=0)

        logits, vps = [], []
        for slab in range(n_slabs):
            qp = (gather(q_ref, (slab,), q0) * scale).astype(BF16)
            kp = jnp.concatenate([gather(k_ref, (slab,), prev0), gather(k_ref, (slab,), cur0)],
                                 axis=0).astype(BF16)
            vps.append(jnp.concatenate([gather(v_ref, (slab,), prev0), gather(v_ref, (slab,), cur0)],
                                       axis=0).astype(BF16))
            for e in range(heads_per_slab):
                mine = low_half if e == 0 else jnp.logical_not(low_half)
                qh = jnp.where(mine, qp, jnp.zeros_like(qp))
                lg = lax.dot_general(qh, kp, NT_DIMS, preferred_element_type=F32)
                logits.append(lg + bias_ref[pat, is_first_blk, slab * heads_per_slab + e])
        m_blk = jnp.zeros((w, LANES), F32)
        s_blk = jnp.zeros((w, LANES), F32)
        ps = []
        for h, lg in enumerate(logits):
            m_h = jnp.max(lg, axis=-1, keepdims=True)
            p = jnp.exp2(lg - m_h)
            ps.append(p.astype(BF16))
            m_blk = jnp.where(lane == h, m_h, m_blk)
            s_blk = jnp.where(lane == h, jnp.sum(p, axis=-1, keepdims=True), s_blk)
        nums = []
        for slab in range(n_slabs):
            h0 = slab * heads_per_slab
            outs = [jnp.dot(ps[h0 + e], vps[slab], preferred_element_type=F32)
                    for e in range(heads_per_slab)]
            nums.append(jnp.where(low_half, outs[0], outs[1]))
        if first:
            m_new, s_new = m_blk, s_blk
        else:
            m_old = gather(m_scr, (), q0)
            m_new = jnp.maximum(m_old, m_blk)
            a_old = jnp.exp2(m_old - m_new)
            a_blk = jnp.exp2(m_blk - m_new)
            s_new = a_old * gather(s_scr, (), q0) + a_blk * s_blk
            w_old, w_blk = expand(a_old), expand(a_blk)
            for slab in range(n_slabs):
                cols = slice(slab * LANES, (slab + 1) * LANES)
                nums[slab] = (w_old[:, cols] * gather(acc_scr, (slab,), q0)
                              + w_blk[:, cols] * nums[slab])
        for idx, res in enumerate(res_of):
            rows, part = pl.ds(q0, c), slice(idx * c, (idx + 1) * c)
            for slab in range(n_slabs):
                acc_scr[slab, res, rows, :] = nums[slab][part]
            m_scr[res, rows, :] = m_new[part]
            s_scr[res, rows, :] = s_new[part]

    n_tasks = n_res
    for pat, (window, dil) in enumerate(DILATED_PATTERNS):
        assert window // dil == w and n_res % dil == 0 and (w * dil) % n_res == 0

        def body(t, carry, pat=pat, dil=dil):
            task(pat, dil, t)
            return carry
        lax.fori_loop(0, n_tasks, body, 0)

    def finish(r, carry):
        den = expand(s_scr[r])
        for slab in range(n_slabs):
            cols = slice(slab * LANES, (slab + 1) * LANES)
            out_scr[slab, pl.ds(r, w, stride=n_res), :] = acc_scr[slab, r] / den[:, cols]
        return carry
    lax.fori_loop(0, n_res, finish, 0)
    for slab in range(n_slabs):
        o_ref[:, slab * LANES:(slab + 1) * LANES] = out_scr[slab].astype(o_ref.dtype)


def _swa(qb, kb, vb, bias, batch, seq):
    n_slabs = SWA_WIDTH // LANES
    n_res = SWA_RESIDUES
    w = DILATED_PATTERNS[0][0] // DILATED_PATTERNS[0][1]
    tq = n_res * w
    nq = seq // tq
    q_blk = pl.BlockSpec((n_slabs, None, n_res, w, LANES), lambda b, i: (0, b, 0, i, 0))
    seq_blk = pl.BlockSpec((n_slabs, None, n_res, seq // n_res, LANES), lambda b, i: (0, b, 0, 0, 0),
                           pipeline_mode=pl.Buffered(1))
    head_of_col = jnp.arange(SWA_WIDTH) // SWA_HEAD_DIM
    sel = (jnp.arange(LANES)[:, None] == head_of_col[None, :]).astype(BF16)
    sel = jnp.concatenate([sel, sel], axis=0)
    return pl.pallas_call(
        functools.partial(_swa_kernel, w=w),
        grid=(batch, nq),
        in_specs=[q_blk, seq_blk, seq_blk, _resident(bias.shape), _resident(sel.shape)],
        out_specs=pl.BlockSpec((tq, SWA_WIDTH), lambda b, i: (b * nq + i, 0)),
        out_shape=jax.ShapeDtypeStruct((batch * seq, SWA_WIDTH), BF16),
        scratch_shapes=[pltpu.VMEM((n_slabs, n_res, w, LANES), F32),
                        pltpu.VMEM((n_res, w, LANES), F32), pltpu.VMEM((n_res, w, LANES), F32),
                        pltpu.VMEM((n_slabs, tq, LANES), F32)],
        compiler_params=pltpu.CompilerParams(dimension_semantics=("arbitrary", "arbitrary"),
                                             vmem_limit_bytes=VMEM_LIMIT_BYTES),
        name="swa",
    )(qb, kb, vb, bias, sel)


def _t5_causal_bucket(dist):
    max_exact = NUM_BUCKETS // 2
    d = jnp.maximum(dist, 1).astype(F32)
    log_b = max_exact + (jnp.log(d / max_exact) / math.log(MAX_DISTANCE / max_exact)
                         * (NUM_BUCKETS - max_exact)).astype(jnp.int32)
    return jnp.where(dist < max_exact, dist, jnp.minimum(log_b, NUM_BUCKETS - 1))


def _band_bias(rel_bias, window, dilation):
    w = window // dilation
    steps = rel_bias[_t5_causal_bucket(jnp.arange(w + 1) * dilation)].astype(F32).T
    n_heads = steps.shape[0]
    pad = jnp.full((n_heads, w - 1), -jnp.inf, F32)
    by_rel = jnp.concatenate([pad, steps, pad], axis=1)
    rev = jnp.concatenate([by_rel[:, ::-1], jnp.full((n_heads, 1), -jnp.inf, F32)], axis=1)
    skew = jnp.tile(rev, (1, w))[:, :w * (3 * w - 1)].reshape(n_heads, w, 3 * w - 1)
    return skew[:, :, w - 1:]


def _swa_bias_tables(rel_bias):
    tables = []
    for window, dil in DILATED_PATTERNS:
        w = window // dil
        n_chunks = SWA_RESIDUES // dil
        c = w // n_chunks
        bias = _band_bias(rel_bias, window, dil) * math.log2(math.e)
        n_heads = bias.shape[0]
        bias = bias.reshape(n_heads, c, n_chunks, 2, c, n_chunks).transpose(0, 2, 1, 3, 5, 4)
        bias = bias.reshape(n_heads, w, 2 * w)
        first = jnp.where(jnp.arange(2 * w) < w, -jnp.inf, bias)
        tables.append(jnp.stack([bias, first]))
    return jnp.stack(tables)


def _mixffn_kernel(x_ref, oa_ref, ob_ref, woa_ref, wob_ref, gpost_ref, gpre_ref, gfpost_ref,
                   wg_ref, wu_ref, wd_ref, out_ref, act_scr, *, ff_chunk):
    mix = (jnp.dot(oa_ref[...], woa_ref[...], preferred_element_type=F32)
           + jnp.dot(ob_ref[...], wob_ref[...], preferred_element_type=F32))
    x1 = x_ref[...] + _rms(mix, gpost_ref[...])
    h = _rms(x1, gpre_ref[...]).astype(BF16)
    d_ff = wg_ref.shape[1]
    for c in range(d_ff // ff_chunk):
        cols = slice(c * ff_chunk, (c + 1) * ff_chunk)
        gate = jnp.dot(h, wg_ref[:, cols], preferred_element_type=F32)
        up = jnp.dot(h, wu_ref[:, cols], preferred_element_type=F32)
        act_scr[:, cols] = (_silu(gate) * up).astype(BF16)
    f = jnp.dot(act_scr[...], wd_ref[...], preferred_element_type=F32)
    out_ref[...] = x1 + _rms(f, gfpost_ref[...])


def _mixffn(x2d, oa, ob, woa, wob, gpost, gpre, gfpost, wg, wu, wd, tm, ff_chunk):
    T, D = x2d.shape
    d_ff = wg.shape[1]
    row = lambda n: pl.BlockSpec((tm, n), lambda i: (i, 0))
    return pl.pallas_call(
        functools.partial(_mixffn_kernel, ff_chunk=ff_chunk),
        grid=(T // tm,),
        in_specs=([row(D), row(GDN_WIDTH), row(SWA_WIDTH)]
                  + [_resident(woa.shape), _resident(wob.shape)] + [_resident((1, D))] * 3
                  + [_resident(wg.shape), _resident(wu.shape), _resident(wd.shape)]),
        out_specs=row(D),
        out_shape=jax.ShapeDtypeStruct((T, D), F32),
        scratch_shapes=[pltpu.VMEM((tm, d_ff), BF16)],
        compiler_params=pltpu.CompilerParams(dimension_semantics=("arbitrary",),
                                             vmem_limit_bytes=VMEM_LIMIT_BYTES),
        name="mixffn",
    )(x2d, oa, ob, woa, wob, gpost, gpre, gfpost, wg, wu, wd)


def _layer(x2d, batch, seq, w_in, conv_w, a_log, dt_bias, onorm_g, rel_bias, w_out,
           g_mix_pre, g_mix_post, w_gate, w_up, w_down, g_ffn_pre, g_ffn_post):
    D = x2d.shape[1]
    gw, sw, nh = GDN_WIDTH, SWA_WIDTH, GDN_HEADS
    wb = w_in.astype(BF16)
    c_gate, c_small, c_q = 3 * gw, 4 * gw, 4 * gw + 2 * nh
    w_small = jnp.pad(wb[:, c_small:c_q], ((0, 0), (0, LANES - 2 * nh)))
    qkv, gate, qb, kb, vb, sm = _inproj(
        x2d, g_mix_pre.reshape(1, D), wb[:, :c_gate], wb[:, c_gate:c_small],
        wb[:, c_q:c_q + sw], wb[:, c_q + sw:c_q + 2 * sw], wb[:, c_q + 2 * sw:], w_small,
        batch, seq, tm=512)

    lane_pad = lambda v: jnp.pad(v.astype(F32).reshape(1, nh), ((0, 0), (nh, LANES - 2 * nh)))
    oa = _gdn(qkv, gate, sm, conv_w.astype(F32), lane_pad(a_log), lane_pad(dt_bias),
              onorm_g.astype(F32).reshape(1, GDN_HEAD_DIM), batch, seq, tb=512)

    ob = _swa(qb, kb, vb, _swa_bias_tables(rel_bias), batch, seq)

    wo = w_out.astype(BF16)
    return _mixffn(x2d, oa, ob, wo[:gw], wo[gw:], g_mix_post.reshape(1, D),
                   g_ffn_pre.reshape(1, D), g_ffn_post.reshape(1, D), w_gate.astype(BF16),
                   w_up.astype(BF16), w_down.astype(BF16), tm=512, ff_chunk=256)


def kernel(x, w_in, conv_w, a_log, dt_bias, onorm_g, rel_bias, w_out, g_mix_pre, g_mix_post,
           w_gate, w_up, w_down, g_ffn_pre, g_ffn_post):
    batch, seq, d_model = x.shape
    x2d = x.reshape(batch * seq, d_model)
    for l in range(w_in.shape[0]):
        x2d = _layer(x2d, batch, seq, w_in[l], conv_w[l], a_log[l], dt_bias[l], onorm_g[l],
                     rel_bias, w_out[l], g_mix_pre[l], g_mix_post[l], w_gate[l], w_up[l],
                     w_down[l], g_ffn_pre[l], g_ffn_post[l])
    return x2d.reshape(batch, seq, d_model)
```

```python
import functools
import math

import jax
import jax.numpy as jnp
from jax import lax
from jax.experimental import pallas as pl
from jax.experimental.pallas import tpu as pltpu

F32 = jnp.float32
BF16 = jnp.bfloat16

GDN_HEADS = 4
GDN_HEAD_DIM = 128
GDN_WIDTH = GDN_HEADS * GDN_HEAD_DIM
CONV_WIDTH = 4
CONV_ROW_PITCH = 2
CHUNK = 64
SWA_HEADS = 8
SWA_HEAD_DIM = 64
SWA_WIDTH = SWA_HEADS * SWA_HEAD_DIM
DILATED_PATTERNS = ((128, 1), (512, 4), (2048, 16))
SWA_RESIDUES = max(dil for _, dil in DILATED_PATTERNS)
NUM_BUCKETS = 32
MAX_DISTANCE = 2048
RMS_EPS = 1e-6

LANES = 128
SUBLANES = 8
VMEM_LIMIT_BYTES = 56 * 1024 * 1024

NT_DIMS = (((1,), (1,)), ((), ()))
TN_DIMS = (((0,), (0,)), ((), ()))


def _rms(x, g):
    return x * lax.rsqrt(jnp.mean(x * x, axis=-1, keepdims=True) + RMS_EPS) * g


def _silu(x):
    return x * jax.nn.sigmoid(x)


def _softplus(x):
    return jnp.maximum(x, 0.0) + jnp.log1p(jnp.exp(-jnp.abs(x)))


def _resident(shape):
    zeros = (0,) * len(shape)
    return pl.BlockSpec(shape, lambda *_: zeros, pipeline_mode=pl.Buffered(1))


def _inproj_kernel(x_ref, g_ref, wqkv_ref, wgate_ref, wq_ref, wk_ref, wv_ref, ws_ref,
                   qkv_ref, gate_ref, qb_ref, kb_ref, vb_ref, sm_ref, stage_scr, stage2_scr):
    h = _rms(x_ref[...], g_ref[...]).astype(BF16)
    for w_ref, o_ref in ((wqkv_ref, qkv_ref), (wgate_ref, gate_ref), (ws_ref, sm_ref)):
        o_ref[...] = jnp.dot(h, w_ref[...], preferred_element_type=F32)
    tm = x_ref.shape[0]
    f = math.isqrt(SWA_RESIDUES)
    assert f * f == SWA_RESIDUES
    for w_ref, o_ref in ((wq_ref, qb_ref), (wk_ref, kb_ref), (wv_ref, vb_ref)):
        res = jnp.dot(h, w_ref[...], preferred_element_type=F32)
        for slab in range(SWA_WIDTH // LANES):
            stage_scr[slab] = res[:, slab * LANES:(slab + 1) * LANES]
        for slab in range(SWA_WIDTH // LANES):
            for a in range(f):
                stage2_scr[slab, a] = stage_scr[slab, pl.ds(a, tm // f, stride=f), :]
        for slab in range(SWA_WIDTH // LANES):
            for a in range(f):
                for b in range(f):
                    o_ref[slab, f * b + a] = stage2_scr[slab, a, pl.ds(b, tm // SWA_RESIDUES, stride=f), :]


def _inproj(x2d, g, wqkv, wgate, wq, wk, wv, ws, batch, seq, tm):
    T, D = x2d.shape
    nt = seq // tm
    row = lambda n: pl.BlockSpec((tm, n), lambda b, j: (b * nt + j, 0))
    n_slabs = SWA_WIDTH // LANES
    slabs = pl.BlockSpec((n_slabs, None, SWA_RESIDUES, tm // SWA_RESIDUES, LANES),
                         lambda b, j: (0, b, 0, j, 0))
    slab_shape = jax.ShapeDtypeStruct(
        (n_slabs, batch, SWA_RESIDUES, seq // SWA_RESIDUES, LANES), F32)
    return pl.pallas_call(
        _inproj_kernel,
        grid=(batch, nt),
        in_specs=[row(D), _resident((1, D))] + [_resident(w.shape) for w in (wqkv, wgate, wq, wk, wv, ws)],
        out_specs=[row(3 * GDN_WIDTH), row(GDN_WIDTH), slabs, slabs, slabs, row(LANES)],
        out_shape=[jax.ShapeDtypeStruct((T, 3 * GDN_WIDTH), F32),
                   jax.ShapeDtypeStruct((T, GDN_WIDTH), F32), slab_shape, slab_shape, slab_shape,
                   jax.ShapeDtypeStruct((T, LANES), F32)],
        scratch_shapes=[pltpu.VMEM((n_slabs, tm, LANES), F32),
                        pltpu.VMEM((n_slabs, math.isqrt(SWA_RESIDUES), tm // math.isqrt(SWA_RESIDUES),
                                    LANES), F32)],
        compiler_params=pltpu.CompilerParams(dimension_semantics=("arbitrary", "arbitrary"),
                                             vmem_limit_bytes=VMEM_LIMIT_BYTES),
        name="inproj",
    )(x2d, g, wqkv, wgate, wq, wk, wv, ws)


def _unit_lower_inverses(mats, eye, xor_ij):
    n = mats[0].shape[0]
    ds = [eye - jnp.where(xor_ij < 2, a, 0.0) for a in mats]
    s = 2
    while s < n:
        band = jnp.logical_and(xor_ij >= s, xor_ij < 2 * s)
        dbs = [d.astype(BF16) for d in ds]
        eds = [jnp.dot(jnp.where(band, a, 0.0).astype(BF16), db,
                       preferred_element_type=F32).astype(BF16) for a, db in zip(mats, dbs)]
        ds = [d - jnp.dot(db, ed, preferred_element_type=F32) for d, db, ed in zip(ds, dbs, eds)]
        s *= 2
    return ds


def _gdn_kernel(qkv_ref, gate_ref, sm_ref, convw_ref, alog_ref, dtb_ref, ong_ref, ltri_ref,
                o_ref, ext_scr, q_scr, k_scr, v_scr, g_scr, beta_scr, state_scr, *, tb, chunk):
    width = 3 * GDN_WIDTH
    hd = GDN_HEAD_DIM
    halo = SUBLANES
    n_groups = width // hd

    pitch = CONV_ROW_PITCH
    at = lambda first, n: pl.ds(pitch * first, n, stride=pitch)

    @pl.when(pl.program_id(1) == 0)
    def _():
        for j in range(n_groups):
            ext_scr[j, at(0, halo), :] = jnp.zeros((halo, hd), F32)
        state_scr[...] = jnp.zeros_like(state_scr)

    @pl.when(pl.program_id(1) > 0)
    def _():
        for j in range(n_groups):
            ext_scr[j, at(0, halo), :] = ext_scr[j, at(tb, halo), :]

    for j in range(n_groups):
        ext_scr[j, at(halo, tb), :] = qkv_ref[:, j * hd:(j + 1) * hd]

    for j in range(n_groups):
        cols = slice(j * hd, (j + 1) * hd)
        acc = None
        for i in range(CONV_WIDTH):
            term = convw_ref[i:i + 1, cols] * ext_scr[j, at(halo - (CONV_WIDTH - 1) + i, tb), :]
            acc = term if acc is None else acc + term
        y = _silu(acc)
        kind, head = divmod(j, GDN_HEADS)
        hcols = slice(head * hd, (head + 1) * hd)
        if kind < 2:
            y = y * lax.rsqrt(jnp.sum(y * y, axis=-1, keepdims=True) + 1e-6)
        if kind == 0:
            q_scr[:, hcols] = y * (hd ** -0.5)
        elif kind == 1:
            k_scr[:, hcols] = y
        else:
            v_scr[:, hcols] = y

    sm = sm_ref[...]
    beta_scr[...] = jax.nn.sigmoid(sm)
    g_scr[...] = -jnp.exp(alog_ref[...]) * _softplus(sm + dtb_ref[...])

    ii = lax.broadcasted_iota(jnp.int32, (chunk, chunk), 0)
    jj = lax.broadcasted_iota(jnp.int32, (chunk, chunk), 1)
    xor_ij = jnp.bitwise_xor(ii, jj)
    lower = ii >= jj
    strict = ii > jj
    eye = jnp.where(ii == jj, 1.0, 0.0).astype(F32)
    ltri = ltri_ref[...]
    ong = ong_ref[...]

    nc = tb // chunk
    heads = range(GDN_HEADS)
    rows = [slice(c * chunk, (c + 1) * chunk) for c in range(nc)]
    hcols = [slice(h * hd, (h + 1) * hd) for h in heads]
    probs = [(c, h) for c in range(nc) for h in heads]

    gcs = []
    for c in range(nc):
        g = g_scr[rows[c], :]
        g_hi = g.astype(BF16)
        rem = g - g_hi.astype(F32)
        g_mid = rem.astype(BF16)
        g_lo = (rem - g_mid.astype(F32)).astype(BF16)
        gcs.append(jnp.dot(ltri, g_hi, preferred_element_type=F32)
                   + jnp.dot(ltri, g_mid, preferred_element_type=F32)
                   + jnp.dot(ltri, g_lo, preferred_element_type=F32))
    gcts = [gc.T for gc in gcs]
    lane_of = lambda h: slice(GDN_HEADS + h, GDN_HEADS + h + 1)
    gcol = {(c, h): gcs[c][:, lane_of(h)] for c, h in probs}
    glast = {(c, h): gcs[c][chunk - 1:chunk, lane_of(h)] for c, h in probs}
    bcol = {(c, h): beta_scr[rows[c], h:h + 1] for c, h in probs}

    qks = {}
    for c, h in probs:
        kcb = k_scr[rows[c], hcols[h]].astype(BF16)
        qkb = jnp.concatenate([q_scr[rows[c], hcols[h]].astype(BF16), kcb], axis=0)
        qks[c, h] = lax.dot_general(qkb, kcb, NT_DIMS, preferred_element_type=F32)
    attn, a_mats = {}, []
    for c, h in probs:
        grow = gcts[c][lane_of(h), :]
        decay = jnp.exp(jnp.where(lower, gcol[c, h] - grow, -jnp.inf))
        attn[c, h] = (qks[c, h][:chunk] * decay).astype(BF16)
        a_mats.append(jnp.where(strict, qks[c, h][chunk:] * decay * bcol[c, h], 0.0))
    t_invs = _unit_lower_inverses(a_mats, eye, xor_ij)

    uws = {}
    for (c, h), t_inv in zip(probs, t_invs):
        kc = k_scr[rows[c], hcols[h]]
        eg = jnp.exp(gcol[c, h])
        rhs = jnp.concatenate([v_scr[rows[c], hcols[h]] * bcol[c, h], kc * (bcol[c, h] * eg)],
                              axis=1).astype(BF16)
        uws[c, h] = jnp.dot(t_inv.astype(BF16), rhs, preferred_element_type=F32).astype(BF16)
    kd_uw, gq, attn_u = {}, {}, {}
    for c, h in probs:
        k_dec = (k_scr[rows[c], hcols[h]] * jnp.exp(glast[c, h] - gcol[c, h])).astype(BF16)
        kd_uw[c, h] = lax.dot_general(k_dec, uws[c, h], TN_DIMS, preferred_element_type=F32)
    for c, h in probs:
        a_uw = jnp.dot(attn[c, h], uws[c, h], preferred_element_type=F32)
        q_eff = q_scr[rows[c], hcols[h]] * jnp.exp(gcol[c, h]) - a_uw[:, hd:]
        gq[c, h] = jnp.concatenate([kd_uw[c, h][:, hd:], q_eff], axis=0).astype(BF16)
        attn_u[c, h] = a_uw[:, :hd]

    state = [state_scr[h] for h in heads]
    for c in range(nc):
        gs = [jnp.dot(gq[c, h], state[h].astype(BF16), preferred_element_type=F32) for h in heads]
        state = [state[h] * jnp.exp(glast[c, h]) - gs[h][:hd] + kd_uw[c, h][:, :hd] for h in heads]
        for h in heads:
            o = gs[h][hd:] + attn_u[c, h]
            gated = _rms(o, ong) * _silu(gate_ref[rows[c], hcols[h]])
            o_ref[rows[c], hcols[h]] = gated.astype(o_ref.dtype)
    for h in heads:
        state_scr[h] = state[h]


def _gdn(qkv, gate, sm, conv_w, alog_row, dtb_row, onorm_g, batch, seq, tb):
    T = batch * seq
    nb = seq // tb
    width = 3 * GDN_WIDTH
    idx = jnp.arange(CHUNK)
    ltri = (idx[:, None] >= idx[None, :]).astype(BF16)
    row = lambda n: pl.BlockSpec((tb, n), lambda b, t: (b * nb + t, 0))
    kern = functools.partial(_gdn_kernel, tb=tb, chunk=CHUNK)
    return pl.pallas_call(
        kern,
        grid=(batch, nb),
        in_specs=[row(width), row(GDN_WIDTH), row(LANES), _resident(conv_w.shape),
                  _resident((1, LANES)), _resident((1, LANES)), _resident((1, GDN_HEAD_DIM)),
                  _resident((CHUNK, CHUNK))],
        out_specs=row(GDN_WIDTH),
        out_shape=jax.ShapeDtypeStruct((T, GDN_WIDTH), BF16),
        scratch_shapes=[pltpu.VMEM((width // GDN_HEAD_DIM, CONV_ROW_PITCH * (tb + SUBLANES),
                                    GDN_HEAD_DIM), F32),
                        pltpu.VMEM((tb, GDN_WIDTH), F32),
                        pltpu.VMEM((tb, GDN_WIDTH), F32),
                        pltpu.VMEM((tb, GDN_WIDTH), F32),
                        pltpu.VMEM((tb, LANES), F32),
                        pltpu.VMEM((tb, LANES), F32),
                        pltpu.VMEM((GDN_HEADS, GDN_HEAD_DIM, GDN_HEAD_DIM), F32)],
        compiler_params=pltpu.CompilerParams(dimension_semantics=("arbitrary", "arbitrary"),
                                             vmem_limit_bytes=VMEM_LIMIT_BYTES),
        name="gdn",
    )(qkv, gate, sm, conv_w, alog_row, dtb_row, onorm_g, ltri)


def _swa_kernel(q_ref, k_ref, v_ref, bias_ref, sel_ref, o_ref, acc_scr, m_scr, s_scr, out_scr,
                *, w):
    n_res = SWA_RESIDUES
    tile = pl.program_id(1)
    n_slabs = SWA_WIDTH // LANES
    heads_per_slab = LANES // SWA_HEAD_DIM
    assert heads_per_slab == 2
    scale = SWA_HEAD_DIM ** -0.5 * math.log2(math.e)
    lane = lax.broadcasted_iota(jnp.int32, (w, LANES), 1)
    low_half = lane < SWA_HEAD_DIM

    def expand(packed):
        hi = packed.astype(BF16)
        lo = (packed - hi.astype(F32)).astype(BF16)
        return jnp.dot(jnp.concatenate([hi, lo], axis=1), sel_ref[...], preferred_element_type=F32)

    def task(pat, dil, t):
        first = pat == 0
        n_chunks = n_res // dil
        c = w // n_chunks
        res_d = jnp.bitwise_and(t, dil - 1)
        blk = jnp.right_shift(t, dil.bit_length() - 1)
        n_glob = tile * n_chunks + blk
        q0 = pl.multiple_of(blk * c, c)
        cur0 = pl.multiple_of(n_glob * c, c)
        prev0 = pl.multiple_of(jnp.maximum(n_glob - 1, 0) * c, c)
        is_first_blk = (n_glob == 0).astype(jnp.int32)
        res_of = [a * dil + res_d for a in range(n_chunks)]

        def gather(ref, lead, start):
            return jnp.concatenate([ref[lead + (res, pl.ds(start, c), slice(None))]
                                    for res in res_of], axis=0)

        logits, vps = [], []
        for slab in range(n_slabs):
            qp = (gather(q_ref, (slab,), q0) * scale).astype(BF16)
            kp = jnp.concatenate([gather(k_ref, (slab,), prev0), gather(k_ref, (slab,), cur0)],
                                 axis=0).astype(BF16)
            vps.append(jnp.concatenate([gather(v_ref, (slab,), prev0), gather(v_ref, (slab,), cur0)],
                                       axis=0).astype(BF16))
            for e in range(heads_per_slab):
                mine = low_half if e == 0 else jnp.logical_not(low_half)
                qh = jnp.where(mine, qp, jnp.zeros_like(qp))
                lg = lax.dot_general(qh, kp, NT_DIMS, preferred_element_type=F32)
                logits.append(lg + bias_ref[pat, is_first_blk, slab * heads_per_slab + e])
        m_blk = jnp.zeros((w, LANES), F32)
        s_blk = jnp.zeros((w, LANES), F32)
        ps = []
        for h, lg in enumerate(logits):
            m_h = jnp.max(lg, axis=-1, keepdims=True)
            p = jnp.exp2(lg - m_h)
            ps.append(p.astype(BF16))
            m_blk = jnp.where(lane == h, m_h, m_blk)
            s_blk = jnp.where(lane == h, jnp.sum(p, axis=-1, keepdims=True), s_blk)
        nums = []
        for slab in range(n_slabs):
            h0 = slab * heads_per_slab
            outs = [jnp.dot(ps[h0 + e], vps[slab], preferred_element_type=F32)
                    for e in range(heads_per_slab)]
            nums.append(jnp.where(low_half, outs[0], outs[1]))
        if first:
            m_new, s_new = m_blk, s_blk
        else:
            m_old = gather(m_scr, (), q0)
            m_new = jnp.maximum(m_old, m_blk)
            a_old = jnp.exp2(m_old - m_new)
            a_blk = jnp.exp2(m_blk - m_new)
            s_new = a_old * gather(s_scr, (), q0) + a_blk * s_blk
            w_old, w_blk = expand(a_old), expand(a_blk)
            for slab in range(n_slabs):
                cols = slice(slab * LANES, (slab + 1) * LANES)
                nums[slab] = (w_old[:, cols] * gather(acc_scr, (slab,), q0)
                              + w_blk[:, cols] * nums[slab])
        for idx, res in enumerate(res_of):
            rows, part = pl.ds(q0, c), slice(idx * c, (idx + 1) * c)
            for slab in range(n_slabs):
                acc_scr[slab, res, rows, :] = nums[slab][part]
            m_scr[res, rows, :] = m_new[part]
            s_scr[res, rows, :] = s_new[part]

    n_tasks = n_res
    for pat, (window, dil) in enumerate(DILATED_PATTERNS):
        assert window // dil == w and n_res % dil == 0 and (w * dil) % n_res == 0

        def body(t, carry, pat=pat, dil=dil):
            task(pat, dil, t)
            return carry
        lax.fori_loop(0, n_tasks, body, 0, unroll=4)

    def finish(r, carry):
        den = expand(s_scr[r])
        for slab in range(n_slabs):
            cols = slice(slab * LANES, (slab + 1) * LANES)
            out_scr[slab, pl.ds(r, w, stride=n_res), :] = acc_scr[slab, r] / den[:, cols]
        return carry
    lax.fori_loop(0, n_res, finish, 0, unroll=4)
    for slab in range(n_slabs):
        o_ref[:, slab * LANES:(slab + 1) * LANES] = out_scr[slab].astype(o_ref.dtype)


def _swa(qb, kb, vb, bias, batch, seq):
    n_slabs = SWA_WIDTH // LANES
    n_res = SWA_RESIDUES
    w = DILATED_PATTERNS[0][0] // DILATED_PATTERNS[0][1]
    tq = n_res * w
    nq = seq // tq
    q_blk = pl.BlockSpec((n_slabs, None, n_res, w, LANES), lambda b, i: (0, b, 0, i, 0))
    seq_blk = pl.BlockSpec((n_slabs, None, n_res, seq // n_res, LANES), lambda b, i: (0, b, 0, 0, 0),
                           pipeline_mode=pl.Buffered(1))
    head_of_col = jnp.arange(SWA_WIDTH) // SWA_HEAD_DIM
    sel = (jnp.arange(LANES)[:, None] == head_of_col[None, :]).astype(BF16)
    sel = jnp.concatenate([sel, sel], axis=0)
    return pl.pallas_call(
        functools.partial(_swa_kernel, w=w),
        grid=(batch, nq),
        in_specs=[q_blk, seq_blk, seq_blk, _resident(bias.shape), _resident(sel.shape)],
        out_specs=pl.BlockSpec((tq, SWA_WIDTH), lambda b, i: (b * nq + i, 0)),
        out_shape=jax.ShapeDtypeStruct((batch * seq, SWA_WIDTH), BF16),
        scratch_shapes=[pltpu.VMEM((n_slabs, n_res, w, LANES), F32),
                        pltpu.VMEM((n_res, w, LANES), F32), pltpu.VMEM((n_res, w, LANES), F32),
                        pltpu.VMEM((n_slabs, tq, LANES), F32)],
        compiler_params=pltpu.CompilerParams(dimension_semantics=("arbitrary", "arbitrary"),
                                             vmem_limit_bytes=VMEM_LIMIT_BYTES),
        name="swa",
    )(qb, kb, vb, bias, sel)


def _t5_causal_bucket(dist):
    max_exact = NUM_BUCKETS // 2
    d = jnp.maximum(dist, 1).astype(F32)
    log_b = max_exact + (jnp.log(d / max_exact) / math.log(MAX_DISTANCE / max_exact)
                         * (NUM_BUCKETS - max_exact)).astype(jnp.int32)
    return jnp.where(dist < max_exact, dist, jnp.minimum(log_b, NUM_BUCKETS - 1))


def _band_bias(rel_bias, window, dilation):
    w = window // dilation
    steps = rel_bias[_t5_causal_bucket(jnp.arange(w + 1) * dilation)].astype(F32).T
    n_heads = steps.shape[0]
    pad = jnp.full((n_heads, w - 1), -jnp.inf, F32)
    by_rel = jnp.concatenate([pad, steps, pad], axis=1)
    rev = jnp.concatenate([by_rel[:, ::-1], jnp.full((n_heads, 1), -jnp.inf, F32)], axis=1)
    skew = jnp.tile(rev, (1, w))[:, :w * (3 * w - 1)].reshape(n_heads, w, 3 * w - 1)
    return skew[:, :, w - 1:]


def _swa_bias_tables(rel_bias):
    tables = []
    for window, dil in DILATED_PATTERNS:
        w = window // dil
        n_chunks = SWA_RESIDUES // dil
        c = w // n_chunks
        bias = _band_bias(rel_bias, window, dil) * math.log2(math.e)
        n_heads = bias.shape[0]
        bias = bias.reshape(n_heads, c, n_chunks, 2, c, n_chunks).transpose(0, 2, 1, 3, 5, 4)
        bias = bias.reshape(n_heads, w, 2 * w)
        first = jnp.where(jnp.arange(2 * w) < w, -jnp.inf, bias)
        tables.append(jnp.stack([bias, first]))
    return jnp.stack(tables)


def _mixffn_kernel(x_ref, oa_ref, ob_ref, woa_ref, wob_ref, gpost_ref, gpre_ref, gfpost_ref,
                   wg_ref, wu_ref, wd_ref, out_ref, act_scr, *, ff_chunk):
    mix = (jnp.dot(oa_ref[...], woa_ref[...], preferred_element_type=F32)
           + jnp.dot(ob_ref[...], wob_ref[...], preferred_element_type=F32))
    x1 = x_ref[...] + _rms(mix, gpost_ref[...])
    h = _rms(x1, gpre_ref[...]).astype(BF16)
    d_ff = wg_ref.shape[1]
    for c in range(d_ff // ff_chunk):
        cols = slice(c * ff_chunk, (c + 1) * ff_chunk)
        gate = jnp.dot(h, wg_ref[:, cols], preferred_element_type=F32)
        up = jnp.dot(h, wu_ref[:, cols], preferred_element_type=F32)
        act_scr[:, cols] = (_silu(gate) * up).astype(BF16)
    f = jnp.dot(act_scr[...], wd_ref[...], preferred_element_type=F32)
    out_ref[...] = x1 + _rms(f, gfpost_ref[...])


def _mixffn(x2d, oa, ob, woa, wob, gpost, gpre, gfpost, wg, wu, wd, tm, ff_chunk):
    T, D = x2d.shape
    d_ff = wg.shape[1]
    row = lambda n: pl.BlockSpec((tm, n), lambda i: (i, 0))
    return pl.pallas_call(
        functools.partial(_mixffn_kernel, ff_chunk=ff_chunk),
        grid=(T // tm,),
        in_specs=([row(D), row(GDN_WIDTH), row(SWA_WIDTH)]
                  + [_resident(woa.shape), _resident(wob.shape)] + [_resident((1, D))] * 3
                  + [_resident(wg.shape), _resident(wu.shape), _resident(wd.shape)]),
        out_specs=row(D),
        out_shape=jax.ShapeDtypeStruct((T, D), F32),
        scratch_shapes=[pltpu.VMEM((tm, d_ff), BF16)],
        compiler_params=pltpu.CompilerParams(dimension_semantics=("arbitrary",),
                                             vmem_limit_bytes=VMEM_LIMIT_BYTES),
        name="mixffn",
    )(x2d, oa, ob, woa, wob, gpost, gpre, gfpost, wg, wu, wd)


def _layer(x2d, batch, seq, w_in, conv_w, a_log, dt_bias, onorm_g, rel_bias, w_out,
           g_mix_pre, g_mix_post, w_gate, w_up, w_down, g_ffn_pre, g_ffn_post):
    D = x2d.shape[1]
    gw, sw, nh = GDN_WIDTH, SWA_WIDTH, GDN_HEADS
    wb = w_in.astype(BF16)
    c_gate, c_small, c_q = 3 * gw, 4 * gw, 4 * gw + 2 * nh
    w_small = jnp.pad(wb[:, c_small:c_q], ((0, 0), (0, LANES - 2 * nh)))
    qkv, gate, qb, kb, vb, sm = _inproj(
        x2d, g_mix_pre.reshape(1, D), wb[:, :c_gate], wb[:, c_gate:c_small],
        wb[:, c_q:c_q + sw], wb[:, c_q + sw:c_q + 2 * sw], wb[:, c_q + 2 * sw:], w_small,
        batch, seq, tm=512)

    lane_pad = lambda v: jnp.pad(v.astype(F32).reshape(1, nh), ((0, 0), (nh, LANES - 2 * nh)))
    oa = _gdn(qkv, gate, sm, conv_w.astype(F32), lane_pad(a_log), lane_pad(dt_bias),
              onorm_g.astype(F32).reshape(1, GDN_HEAD_DIM), batch, seq, tb=512)

    ob = _swa(qb, kb, vb, _swa_bias_tables(rel_bias), batch, seq)

    wo = w_out.astype(BF16)
    return _mixffn(x2d, oa, ob, wo[:gw], wo[gw:], g_mix_post.reshape(1, D),
                   g_ffn_pre.reshape(1, D), g_ffn_post.reshape(1, D), w_gate.astype(BF16),
                   w_up.astype(BF16), w_down.astype(BF16), tm=512, ff_chunk=256)


def kernel(x, w_in, conv_w, a_log, dt_bias, onorm_g, rel_bias, w_out, g_mix_pre, g_mix_post,
           w_gate, w_up, w_down, g_ffn_pre, g_ffn_post):
    batch, seq, d_model = x.shape
    x2d = x.reshape(batch * seq, d_model)
    for l in range(w_in.shape[0]):
        x2d = _layer(x2d, batch, seq, w_in[l], conv_w[l], a_log[l], dt_bias[l], onorm_g[l],
                     rel_bias, w_out[l], g_mix_pre[l], g_mix_post[l], w_gate[l], w_up[l],
                     w_down[l], g_ffn_pre[l], g_ffn_post[l])
    return x2d.reshape(batch, seq, d_model)
```

```python
import functools
import math

import jax
import jax.numpy as jnp
from jax import lax
from jax.experimental import pallas as pl
from jax.experimental.pallas import tpu as pltpu

F32 = jnp.float32
BF16 = jnp.bfloat16

GDN_HEADS = 4
GDN_HEAD_DIM = 128
GDN_WIDTH = GDN_HEADS * GDN_HEAD_DIM
CONV_WIDTH = 4
CONV_ROW_PITCH = 2
CHUNK = 64
SWA_HEADS = 8
SWA_HEAD_DIM = 64
SWA_WIDTH = SWA_HEADS * SWA_HEAD_DIM
DILATED_PATTERNS = ((128, 1), (512, 4), (2048, 16))
SWA_RESIDUES = max(dil for _, dil in DILATED_PATTERNS)
SWA_ROW_PAD = 8
NUM_BUCKETS = 32
MAX_DISTANCE = 2048
RMS_EPS = 1e-6

LANES = 128
SUBLANES = 8
VMEM_LIMIT_BYTES = 56 * 1024 * 1024

NT_DIMS = (((1,), (1,)), ((), ()))
TN_DIMS = (((0,), (0,)), ((), ()))


def _rms(x, g):
    return x * lax.rsqrt(jnp.mean(x * x, axis=-1, keepdims=True) + RMS_EPS) * g


def _silu(x):
    return x * jax.nn.sigmoid(x)


def _softplus(x):
    return jnp.maximum(x, 0.0) + jnp.log1p(jnp.exp(-jnp.abs(x)))


def _resident(shape):
    zeros = (0,) * len(shape)
    return pl.BlockSpec(shape, lambda *_: zeros, pipeline_mode=pl.Buffered(1))


def _inproj_kernel(x_ref, g_ref, wqkv_ref, wgate_ref, wq_ref, wk_ref, wv_ref, ws_ref,
                   qkv_ref, gate_ref, qb_ref, kb_ref, vb_ref, sm_ref, stage_scr, stage2_scr):
    h = _rms(x_ref[...], g_ref[...]).astype(BF16)
    for w_ref, o_ref in ((wqkv_ref, qkv_ref), (wgate_ref, gate_ref), (ws_ref, sm_ref)):
        o_ref[...] = jnp.dot(h, w_ref[...], preferred_element_type=F32)
    tm = x_ref.shape[0]
    f = math.isqrt(SWA_RESIDUES)
    assert f * f == SWA_RESIDUES
    for w_ref, o_ref in ((wq_ref, qb_ref), (wk_ref, kb_ref), (wv_ref, vb_ref)):
        res = jnp.dot(h, w_ref[...], preferred_element_type=F32)
        for slab in range(SWA_WIDTH // LANES):
            stage_scr[slab] = res[:, slab * LANES:(slab + 1) * LANES]
        for slab in range(SWA_WIDTH // LANES):
            for a in range(f):
                stage2_scr[slab, a] = stage_scr[slab, pl.ds(a, tm // f, stride=f), :]
        for slab in range(SWA_WIDTH // LANES):
            for a in range(f):
                for b in range(f):
                    o_ref[slab, f * b + a] = stage2_scr[slab, a, pl.ds(b, tm // SWA_RESIDUES, stride=f), :]


def _inproj(x2d, g, wqkv, wgate, wq, wk, wv, ws, batch, seq, tm):
    T, D = x2d.shape
    nt = seq // tm
    row = lambda n: pl.BlockSpec((tm, n), lambda b, j: (b * nt + j, 0))
    n_slabs = SWA_WIDTH // LANES
    slabs = pl.BlockSpec((n_slabs, None, SWA_RESIDUES, tm // SWA_RESIDUES, LANES),
                         lambda b, j: (0, b, 0, j, 0))
    slab_shape = jax.ShapeDtypeStruct(
        (n_slabs, batch, SWA_RESIDUES, seq // SWA_RESIDUES + SWA_ROW_PAD, LANES), F32)
    return pl.pallas_call(
        _inproj_kernel,
        grid=(batch, nt),
        in_specs=[row(D), _resident((1, D))] + [_resident(w.shape) for w in (wqkv, wgate, wq, wk, wv, ws)],
        out_specs=[row(3 * GDN_WIDTH), row(GDN_WIDTH), slabs, slabs, slabs, row(LANES)],
        out_shape=[jax.ShapeDtypeStruct((T, 3 * GDN_WIDTH), F32),
                   jax.ShapeDtypeStruct((T, GDN_WIDTH), F32), slab_shape, slab_shape, slab_shape,
                   jax.ShapeDtypeStruct((T, LANES), F32)],
        scratch_shapes=[pltpu.VMEM((n_slabs, tm, LANES), F32),
                        pltpu.VMEM((n_slabs, math.isqrt(SWA_RESIDUES), tm // math.isqrt(SWA_RESIDUES),
                                    LANES), F32)],
        compiler_params=pltpu.CompilerParams(dimension_semantics=("arbitrary", "arbitrary"),
                                             vmem_limit_bytes=VMEM_LIMIT_BYTES),
        name="inproj",
    )(x2d, g, wqkv, wgate, wq, wk, wv, ws)


def _unit_lower_inverses(mats, eye, xor_ij):
    n = mats[0].shape[0]
    ds = [eye - jnp.where(xor_ij < 2, a, 0.0) for a in mats]
    s = 2
    while s < n:
        band = jnp.logical_and(xor_ij >= s, xor_ij < 2 * s)
        dbs = [d.astype(BF16) for d in ds]
        eds = [jnp.dot(jnp.where(band, a, 0.0).astype(BF16), db,
                       preferred_element_type=F32).astype(BF16) for a, db in zip(mats, dbs)]
        ds = [d - jnp.dot(db, ed, preferred_element_type=F32) for d, db, ed in zip(ds, dbs, eds)]
        s *= 2
    return ds


def _gdn_kernel(qkv_ref, gate_ref, sm_ref, convw_ref, alog_ref, dtb_ref, ong_ref, ltri_ref,
                o_ref, ext_scr, q_scr, k_scr, v_scr, g_scr, beta_scr, state_scr, *, tb, chunk):
    width = 3 * GDN_WIDTH
    hd = GDN_HEAD_DIM
    halo = SUBLANES
    n_groups = width // hd

    pitch = CONV_ROW_PITCH
    at = lambda first, n: pl.ds(pitch * first, n, stride=pitch)

    @pl.when(pl.program_id(1) == 0)
    def _():
        for j in range(n_groups):
            ext_scr[j, at(0, halo), :] = jnp.zeros((halo, hd), F32)
        state_scr[...] = jnp.zeros_like(state_scr)

    @pl.when(pl.program_id(1) > 0)
    def _():
        for j in range(n_groups):
            ext_scr[j, at(0, halo), :] = ext_scr[j, at(tb, halo), :]

    for j in range(n_groups):
        ext_scr[j, at(halo, tb), :] = qkv_ref[:, j * hd:(j + 1) * hd]

    for j in range(n_groups):
        cols = slice(j * hd, (j + 1) * hd)
        acc = None
        for i in range(CONV_WIDTH):
            term = convw_ref[i:i + 1, cols] * ext_scr[j, at(halo - (CONV_WIDTH - 1) + i, tb), :]
            acc = term if acc is None else acc + term
        y = _silu(acc)
        kind, head = divmod(j, GDN_HEADS)
        hcols = slice(head * hd, (head + 1) * hd)
        if kind < 2:
            y = y * lax.rsqrt(jnp.sum(y * y, axis=-1, keepdims=True) + 1e-6)
        if kind == 0:
            q_scr[:, hcols] = y * (hd ** -0.5)
        elif kind == 1:
            k_scr[:, hcols] = y
        else:
            v_scr[:, hcols] = y

    sm = sm_ref[...]
    beta_scr[...] = jax.nn.sigmoid(sm)
    g_scr[...] = -jnp.exp(alog_ref[...]) * _softplus(sm + dtb_ref[...])

    ii = lax.broadcasted_iota(jnp.int32, (chunk, chunk), 0)
    jj = lax.broadcasted_iota(jnp.int32, (chunk, chunk), 1)
    xor_ij = jnp.bitwise_xor(ii, jj)
    lower = ii >= jj
    strict = ii > jj
    eye = jnp.where(ii == jj, 1.0, 0.0).astype(F32)
    ltri = ltri_ref[...]
    ong = ong_ref[...]

    nc = tb // chunk
    heads = range(GDN_HEADS)
    rows = [slice(c * chunk, (c + 1) * chunk) for c in range(nc)]
    hcols = [slice(h * hd, (h + 1) * hd) for h in heads]
    probs = [(c, h) for c in range(nc) for h in heads]

    gcs = []
    for c in range(nc):
        g = g_scr[rows[c], :]
        g_hi = g.astype(BF16)
        rem = g - g_hi.astype(F32)
        g_mid = rem.astype(BF16)
        g_lo = (rem - g_mid.astype(F32)).astype(BF16)
        gcs.append(jnp.dot(ltri, g_hi, preferred_element_type=F32)
                   + jnp.dot(ltri, g_mid, preferred_element_type=F32)
                   + jnp.dot(ltri, g_lo, preferred_element_type=F32))
    gcts = [gc.T for gc in gcs]
    lane_of = lambda h: slice(GDN_HEADS + h, GDN_HEADS + h + 1)
    gcol = {(c, h): gcs[c][:, lane_of(h)] for c, h in probs}
    glast = {(c, h): gcs[c][chunk - 1:chunk, lane_of(h)] for c, h in probs}
    bcol = {(c, h): beta_scr[rows[c], h:h + 1] for c, h in probs}

    qks = {}
    for c, h in probs:
        kcb = k_scr[rows[c], hcols[h]].astype(BF16)
        qkb = jnp.concatenate([q_scr[rows[c], hcols[h]].astype(BF16), kcb], axis=0)
        qks[c, h] = lax.dot_general(qkb, kcb, NT_DIMS, preferred_element_type=F32)
    attn, a_mats = {}, []
    for c, h in probs:
        grow = gcts[c][lane_of(h), :]
        decay = jnp.exp(jnp.where(lower, gcol[c, h] - grow, -jnp.inf))
        attn[c, h] = (qks[c, h][:chunk] * decay).astype(BF16)
        a_mats.append(jnp.where(strict, qks[c, h][chunk:] * decay * bcol[c, h], 0.0))
    t_invs = _unit_lower_inverses(a_mats, eye, xor_ij)

    uws = {}
    for (c, h), t_inv in zip(probs, t_invs):
        kc = k_scr[rows[c], hcols[h]]
        eg = jnp.exp(gcol[c, h])
        rhs = jnp.concatenate([v_scr[rows[c], hcols[h]] * bcol[c, h], kc * (bcol[c, h] * eg)],
                              axis=1).astype(BF16)
        uws[c, h] = jnp.dot(t_inv.astype(BF16), rhs, preferred_element_type=F32).astype(BF16)
    kd_uw, gq, attn_u = {}, {}, {}
    for c, h in probs:
        k_dec = (k_scr[rows[c], hcols[h]] * jnp.exp(glast[c, h] - gcol[c, h])).astype(BF16)
        kd_uw[c, h] = lax.dot_general(k_dec, uws[c, h], TN_DIMS, preferred_element_type=F32)
    for c, h in probs:
        a_uw = jnp.dot(attn[c, h], uws[c, h], preferred_element_type=F32)
        q_eff = q_scr[rows[c], hcols[h]] * jnp.exp(gcol[c, h]) - a_uw[:, hd:]
        gq[c, h] = jnp.concatenate([kd_uw[c, h][:, hd:], q_eff], axis=0).astype(BF16)
        attn_u[c, h] = a_uw[:, :hd]

    state = [state_scr[h] for h in heads]
    for c in range(nc):
        gs = [jnp.dot(gq[c, h], state[h].astype(BF16), preferred_element_type=F32) for h in heads]
        state = [state[h] * jnp.exp(glast[c, h]) - gs[h][:hd] + kd_uw[c, h][:, :hd] for h in heads]
        for h in heads:
            o = gs[h][hd:] + attn_u[c, h]
            gated = _rms(o, ong) * _silu(gate_ref[rows[c], hcols[h]])
            o_ref[rows[c], hcols[h]] = gated.astype(o_ref.dtype)
    for h in heads:
        state_scr[h] = state[h]


def _gdn(qkv, gate, sm, conv_w, alog_row, dtb_row, onorm_g, batch, seq, tb):
    T = batch * seq
    nb = seq // tb
    width = 3 * GDN_WIDTH
    idx = jnp.arange(CHUNK)
    ltri = (idx[:, None] >= idx[None, :]).astype(BF16)
    row = lambda n: pl.BlockSpec((tb, n), lambda b, t: (b * nb + t, 0))
    kern = functools.partial(_gdn_kernel, tb=tb, chunk=CHUNK)
    return pl.pallas_call(
        kern,
        grid=(batch, nb),
        in_specs=[row(width), row(GDN_WIDTH), row(LANES), _resident(conv_w.shape),
                  _resident((1, LANES)), _resident((1, LANES)), _resident((1, GDN_HEAD_DIM)),
                  _resident((CHUNK, CHUNK))],
        out_specs=row(GDN_WIDTH),
        out_shape=jax.ShapeDtypeStruct((T, GDN_WIDTH), BF16),
        scratch_shapes=[pltpu.VMEM((width // GDN_HEAD_DIM, CONV_ROW_PITCH * (tb + SUBLANES),
                                    GDN_HEAD_DIM), F32),
                        pltpu.VMEM((tb, GDN_WIDTH), F32),
                        pltpu.VMEM((tb, GDN_WIDTH), F32),
                        pltpu.VMEM((tb, GDN_WIDTH), F32),
                        pltpu.VMEM((tb, LANES), F32),
                        pltpu.VMEM((tb, LANES), F32),
                        pltpu.VMEM((GDN_HEADS, GDN_HEAD_DIM, GDN_HEAD_DIM), F32)],
        compiler_params=pltpu.CompilerParams(dimension_semantics=("arbitrary", "arbitrary"),
                                             vmem_limit_bytes=VMEM_LIMIT_BYTES),
        name="gdn",
    )(qkv, gate, sm, conv_w, alog_row, dtb_row, onorm_g, ltri)


def _swa_kernel(q_ref, k_ref, v_ref, bias_ref, sel_ref, o_ref, acc_scr, m_scr, s_scr, out_scr,
                *, w):
    n_res = SWA_RESIDUES
    tile = pl.program_id(1)
    n_slabs = SWA_WIDTH // LANES
    heads_per_slab = LANES // SWA_HEAD_DIM
    assert heads_per_slab == 2
    scale = SWA_HEAD_DIM ** -0.5 * math.log2(math.e)
    lane = lax.broadcasted_iota(jnp.int32, (w, LANES), 1)
    low_half = lane < SWA_HEAD_DIM

    def expand(packed):
        hi = packed.astype(BF16)
        lo = (packed - hi.astype(F32)).astype(BF16)
        return jnp.dot(jnp.concatenate([hi, lo], axis=1), sel_ref[...], preferred_element_type=F32)

    def task(pat, dil, t):
        first = pat == 0
        n_chunks = n_res // dil
        c = w // n_chunks
        res_d = jnp.bitwise_and(t, dil - 1)
        blk = jnp.right_shift(t, dil.bit_length() - 1)
        n_glob = tile * n_chunks + blk
        q0 = pl.multiple_of(blk * c, c)
        cur0 = pl.multiple_of(n_glob * c, c)
        prev0 = pl.multiple_of(jnp.maximum(n_glob - 1, 0) * c, c)
        is_first_blk = (n_glob == 0).astype(jnp.int32)
        res_of = [a * dil + res_d for a in range(n_chunks)]

        def gather(ref, lead, start):
            return jnp.concatenate([ref[lead + (res, pl.ds(start, c), slice(None))]
                                    for res in res_of], axis=0)

        logits, vps = [], []
        for slab in range(n_slabs):
            qp = (gather(q_ref, (slab,), q0) * scale).astype(BF16)
            kp = jnp.concatenate([gather(k_ref, (slab,), prev0), gather(k_ref, (slab,), cur0)],
                                 axis=0).astype(BF16)
            vps.append(jnp.concatenate([gather(v_ref, (slab,), prev0), gather(v_ref, (slab,), cur0)],
                                       axis=0).astype(BF16))
            for e in range(heads_per_slab):
                mine = low_half if e == 0 else jnp.logical_not(low_half)
                qh = jnp.where(mine, qp, jnp.zeros_like(qp))
                lg = lax.dot_general(qh, kp, NT_DIMS, preferred_element_type=F32)
                logits.append(lg + bias_ref[pat, is_first_blk, slab * heads_per_slab + e])
        m_blk = jnp.zeros((w, LANES), F32)
        s_blk = jnp.zeros((w, LANES), F32)
        ps = []
        for h, lg in enumerate(logits):
            m_h = jnp.max(lg, axis=-1, keepdims=True)
            p = jnp.exp2(lg - m_h)
            ps.append(p.astype(BF16))
            m_blk = jnp.where(lane == h, m_h, m_blk)
            s_blk = jnp.where(lane == h, jnp.sum(p, axis=-1, keepdims=True), s_blk)
        nums = []
        for slab in range(n_slabs):
            h0 = slab * heads_per_slab
            outs = [jnp.dot(ps[h0 + e], vps[slab], preferred_element_type=F32)
                    for e in range(heads_per_slab)]
            nums.append(jnp.where(low_half, outs[0], outs[1]))
        if first:
            m_new, s_new = m_blk, s_blk
        else:
            m_old = gather(m_scr, (), q0)
            m_new = jnp.maximum(m_old, m_blk)
            a_old = jnp.exp2(m_old - m_new)
            a_blk = jnp.exp2(m_blk - m_new)
            s_new = a_old * gather(s_scr, (), q0) + a_blk * s_blk
            w_old, w_blk = expand(a_old), expand(a_blk)
            for slab in range(n_slabs):
                cols = slice(slab * LANES, (slab + 1) * LANES)
                nums[slab] = (w_old[:, cols] * gather(acc_scr, (slab,), q0)
                              + w_blk[:, cols] * nums[slab])
        for idx, res in enumerate(res_of):
            rows, part = pl.ds(q0, c), slice(idx * c, (idx + 1) * c)
            for slab in range(n_slabs):
                acc_scr[slab, res, rows, :] = nums[slab][part]
            m_scr[res, rows, :] = m_new[part]
            s_scr[res, rows, :] = s_new[part]

    n_tasks = n_res
    for pat, (window, dil) in enumerate(DILATED_PATTERNS):
        assert window // dil == w and n_res % dil == 0 and (w * dil) % n_res == 0

        def body(t, carry, pat=pat, dil=dil):
            task(pat, dil, t)
            return carry
        lax.fori_loop(0, n_tasks, body, 0, unroll=4)

    def finish(r, carry):
        den = expand(s_scr[r, 0:w, :])
        for slab in range(n_slabs):
            cols = slice(slab * LANES, (slab + 1) * LANES)
            out_scr[slab, pl.ds(r, w, stride=n_res), :] = acc_scr[slab, r, 0:w, :] / den[:, cols]
        return carry
    lax.fori_loop(0, n_res, finish, 0, unroll=4)
    for slab in range(n_slabs):
        o_ref[:, slab * LANES:(slab + 1) * LANES] = out_scr[slab].astype(o_ref.dtype)


def _swa(qb, kb, vb, bias, batch, seq):
    n_slabs = SWA_WIDTH // LANES
    n_res = SWA_RESIDUES
    w = DILATED_PATTERNS[0][0] // DILATED_PATTERNS[0][1]
    tq = n_res * w
    nq = seq // tq
    q_blk = pl.BlockSpec((n_slabs, None, n_res, w, LANES), lambda b, i: (0, b, 0, i, 0))
    seq_blk = pl.BlockSpec((n_slabs, None, n_res, kb.shape[3], LANES), lambda b, i: (0, b, 0, 0, 0),
                           pipeline_mode=pl.Buffered(1))
    head_of_col = jnp.arange(SWA_WIDTH) // SWA_HEAD_DIM
    sel = (jnp.arange(LANES)[:, None] == head_of_col[None, :]).astype(BF16)
    sel = jnp.concatenate([sel, sel], axis=0)
    return pl.pallas_call(
        functools.partial(_swa_kernel, w=w),
        grid=(batch, nq),
        in_specs=[q_blk, seq_blk, seq_blk, _resident(bias.shape), _resident(sel.shape)],
        out_specs=pl.BlockSpec((tq, SWA_WIDTH), lambda b, i: (b * nq + i, 0)),
        out_shape=jax.ShapeDtypeStruct((batch * seq, SWA_WIDTH), BF16),
        scratch_shapes=[pltpu.VMEM((n_slabs, n_res, w + SWA_ROW_PAD, LANES), F32),
                        pltpu.VMEM((n_res, w + SWA_ROW_PAD, LANES), F32),
                        pltpu.VMEM((n_res, w + SWA_ROW_PAD, LANES), F32),
                        pltpu.VMEM((n_slabs, tq, LANES), F32)],
        compiler_params=pltpu.CompilerParams(dimension_semantics=("arbitrary", "arbitrary"),
                                             vmem_limit_bytes=VMEM_LIMIT_BYTES),
        name="swa",
    )(qb, kb, vb, bias, sel)


def _t5_causal_bucket(dist):
    max_exact = NUM_BUCKETS // 2
    d = jnp.maximum(dist, 1).astype(F32)
    log_b = max_exact + (jnp.log(d / max_exact) / math.log(MAX_DISTANCE / max_exact)
                         * (NUM_BUCKETS - max_exact)).astype(jnp.int32)
    return jnp.where(dist < max_exact, dist, jnp.minimum(log_b, NUM_BUCKETS - 1))


def _band_bias(rel_bias, window, dilation):
    w = window // dilation
    steps = rel_bias[_t5_causal_bucket(jnp.arange(w + 1) * dilation)].astype(F32).T
    n_heads = steps.shape[0]
    pad = jnp.full((n_heads, w - 1), -jnp.inf, F32)
    by_rel = jnp.concatenate([pad, steps, pad], axis=1)
    rev = jnp.concatenate([by_rel[:, ::-1], jnp.full((n_heads, 1), -jnp.inf, F32)], axis=1)
    skew = jnp.tile(rev, (1, w))[:, :w * (3 * w - 1)].reshape(n_heads, w, 3 * w - 1)
    return skew[:, :, w - 1:]


def _swa_bias_tables(rel_bias):
    tables = []
    for window, dil in DILATED_PATTERNS:
        w = window // dil
        n_chunks = SWA_RESIDUES // dil
        c = w // n_chunks
        bias = _band_bias(rel_bias, window, dil) * math.log2(math.e)
        n_heads = bias.shape[0]
        bias = bias.reshape(n_heads, c, n_chunks, 2, c, n_chunks).transpose(0, 2, 1, 3, 5, 4)
        bias = bias.reshape(n_heads, w, 2 * w)
        first = jnp.where(jnp.arange(2 * w) < w, -jnp.inf, bias)
        tables.append(jnp.stack([bias, first]))
    return jnp.stack(tables)


def _mixffn_kernel(x_ref, oa_ref, ob_ref, woa_ref, wob_ref, gpost_ref, gpre_ref, gfpost_ref,
                   wg_ref, wu_ref, wd_ref, out_ref, act_scr, *, ff_chunk):
    mix = (jnp.dot(oa_ref[...], woa_ref[...], preferred_element_type=F32)
           + jnp.dot(ob_ref[...], wob_ref[...], preferred_element_type=F32))
    x1 = x_ref[...] + _rms(mix, gpost_ref[...])
    h = _rms(x1, gpre_ref[...]).astype(BF16)
    d_ff = wg_ref.shape[1]
    for c in range(d_ff // ff_chunk):
        cols = slice(c * ff_chunk, (c + 1) * ff_chunk)
        gate = jnp.dot(h, wg_ref[:, cols], preferred_element_type=F32)
        up = jnp.dot(h, wu_ref[:, cols], preferred_element_type=F32)
        act_scr[:, cols] = (_silu(gate) * up).astype(BF16)
    f = jnp.dot(act_scr[...], wd_ref[...], preferred_element_type=F32)
    out_ref[...] = x1 + _rms(f, gfpost_ref[...])


def _mixffn(x2d, oa, ob, woa, wob, gpost, gpre, gfpost, wg, wu, wd, tm, ff_chunk):
    T, D = x2d.shape
    d_ff = wg.shape[1]
    row = lambda n: pl.BlockSpec((tm, n), lambda i: (i, 0))
    return pl.pallas_call(
        functools.partial(_mixffn_kernel, ff_chunk=ff_chunk),
        grid=(T // tm,),
        in_specs=([row(D), row(GDN_WIDTH), row(SWA_WIDTH)]
                  + [_resident(woa.shape), _resident(wob.shape)] + [_resident((1, D))] * 3
                  + [_resident(wg.shape), _resident(wu.shape), _resident(wd.shape)]),
        out_specs=row(D),
        out_shape=jax.ShapeDtypeStruct((T, D), F32),
        scratch_shapes=[pltpu.VMEM((tm, d_ff), BF16)],
        compiler_params=pltpu.CompilerParams(dimension_semantics=("arbitrary",),
                                             vmem_limit_bytes=VMEM_LIMIT_BYTES),
        name="mixffn",
    )(x2d, oa, ob, woa, wob, gpost, gpre, gfpost, wg, wu, wd)


def _layer(x2d, batch, seq, w_in, conv_w, a_log, dt_bias, onorm_g, rel_bias, w_out,
           g_mix_pre, g_mix_post, w_gate, w_up, w_down, g_ffn_pre, g_ffn_post):
    D = x2d.shape[1]
    gw, sw, nh = GDN_WIDTH, SWA_WIDTH, GDN_HEADS
    wb = w_in.astype(BF16)
    c_gate, c_small, c_q = 3 * gw, 4 * gw, 4 * gw + 2 * nh
    w_small = jnp.pad(wb[:, c_small:c_q], ((0, 0), (0, LANES - 2 * nh)))
    qkv, gate, qb, kb, vb, sm = _inproj(
        x2d, g_mix_pre.reshape(1, D), wb[:, :c_gate], wb[:, c_gate:c_small],
        wb[:, c_q:c_q + sw], wb[:, c_q + sw:c_q + 2 * sw], wb[:, c_q + 2 * sw:], w_small,
        batch, seq, tm=512)

    lane_pad = lambda v: jnp.pad(v.astype(F32).reshape(1, nh), ((0, 0), (nh, LANES - 2 * nh)))
    oa = _gdn(qkv, gate, sm, conv_w.astype(F32), lane_pad(a_log), lane_pad(dt_bias),
              onorm_g.astype(F32).reshape(1, GDN_HEAD_DIM), batch, seq, tb=512)

    ob = _swa(qb, kb, vb, _swa_bias_tables(rel_bias), batch, seq)

    wo = w_out.astype(BF16)
    return _mixffn(x2d, oa, ob, wo[:gw], wo[gw:], g_mix_post.reshape(1, D),
                   g_ffn_pre.reshape(1, D), g_ffn_post.reshape(1, D), w_gate.astype(BF16),
                   w_up.astype(BF16), w_down.astype(BF16), tm=512, ff_chunk=256)


def kernel(x, w_in, conv_w, a_log, dt_bias, onorm_g, rel_bias, w_out, g_mix_pre, g_mix_post,
           w_gate, w_up, w_down, g_ffn_pre, g_ffn_post):
    batch, seq, d_model = x.shape
    x2d = x.reshape(batch * seq, d_model)
    for l in range(w_in.shape[0]):
        x2d = _layer(x2d, batch, seq, w_in[l], conv_w[l], a_log[l], dt_bias[l], onorm_g[l],
                     rel_bias, w_out[l], g_mix_pre[l], g_mix_post[l], w_gate[l], w_up[l],
                     w_down[l], g_ffn_pre[l], g_ffn_post[l])
    return x2d.reshape(batch, seq, d_model)
```

```python
import functools
import math

import jax
import jax.numpy as jnp
import numpy as np
from jax import lax
from jax.experimental import pallas as pl
from jax.experimental.pallas import tpu as pltpu

F32 = jnp.float32
BF16 = jnp.bfloat16

GDN_HEADS = 4
GDN_HEAD_DIM = 128
GDN_WIDTH = GDN_HEADS * GDN_HEAD_DIM
CONV_WIDTH = 4
CONV_ROW_PITCH = 2
CHUNK = 64
SWA_HEADS = 8
SWA_HEAD_DIM = 64
SWA_WIDTH = SWA_HEADS * SWA_HEAD_DIM
DILATED_PATTERNS = ((128, 1), (512, 4), (2048, 16))
SWA_RESIDUES = max(dil for _, dil in DILATED_PATTERNS)
NUM_BUCKETS = 32
MAX_DISTANCE = 2048
RMS_EPS = 1e-6

LANES = 128
SUBLANES = 8
VMEM_LIMIT_BYTES = 56 * 1024 * 1024

NT_DIMS = (((1,), (1,)), ((), ()))
TN_DIMS = (((0,), (0,)), ((), ()))


def _rms(x, g):
    return x * lax.rsqrt(jnp.mean(x * x, axis=-1, keepdims=True) + RMS_EPS) * g


def _silu(x):
    return x * jax.nn.sigmoid(x)


def _softplus(x):
    return jnp.maximum(x, 0.0) + jnp.log1p(jnp.exp(-jnp.abs(x)))


def _resident(shape):
    zeros = (0,) * len(shape)
    return pl.BlockSpec(shape, lambda *_: zeros, pipeline_mode=pl.Buffered(1))


def _inproj_kernel(x_ref, g_ref, wqkv_ref, wgate_ref, wq_ref, wk_ref, wv_ref, ws_ref,
                   qkv_ref, gate_ref, qb_ref, kb_ref, vb_ref, sm_ref, stage_scr, stage2_scr):
    h = _rms(x_ref[...], g_ref[...]).astype(BF16)
    for w_ref, o_ref in ((wqkv_ref, qkv_ref), (wgate_ref, gate_ref), (ws_ref, sm_ref)):
        o_ref[...] = jnp.dot(h, w_ref[...], preferred_element_type=F32)
    tm = x_ref.shape[0]
    f = math.isqrt(SWA_RESIDUES)
    assert f * f == SWA_RESIDUES
    for w_ref, o_ref in ((wq_ref, qb_ref), (wk_ref, kb_ref), (wv_ref, vb_ref)):
        res = jnp.dot(h, w_ref[...], preferred_element_type=F32)
        for slab in range(SWA_WIDTH // LANES):
            stage_scr[slab] = res[:, slab * LANES:(slab + 1) * LANES]
        for slab in range(SWA_WIDTH // LANES):
            for a in range(f):
                stage2_scr[slab, a] = stage_scr[slab, pl.ds(a, tm // f, stride=f), :]
        for slab in range(SWA_WIDTH // LANES):
            for a in range(f):
                for b in range(f):
                    o_ref[slab, f * b + a] = stage2_scr[slab, a, pl.ds(b, tm // SWA_RESIDUES, stride=f), :]


def _inproj(x2d, g, wqkv, wgate, wq, wk, wv, ws, batch, seq, tm):
    T, D = x2d.shape
    nt = seq // tm
    row = lambda n: pl.BlockSpec((tm, n), lambda b, j: (b * nt + j, 0))
    n_slabs = SWA_WIDTH // LANES
    slabs = pl.BlockSpec((n_slabs, None, SWA_RESIDUES, tm // SWA_RESIDUES, LANES),
                         lambda b, j: (0, b, 0, j, 0))
    slab_shape = jax.ShapeDtypeStruct(
        (n_slabs, batch, SWA_RESIDUES, seq // SWA_RESIDUES, LANES), F32)
    return pl.pallas_call(
        _inproj_kernel,
        grid=(batch, nt),
        in_specs=[row(D), _resident((1, D))] + [_resident(w.shape) for w in (wqkv, wgate, wq, wk, wv, ws)],
        out_specs=[row(3 * GDN_WIDTH), row(GDN_WIDTH), slabs, slabs, slabs, row(LANES)],
        out_shape=[jax.ShapeDtypeStruct((T, 3 * GDN_WIDTH), F32),
                   jax.ShapeDtypeStruct((T, GDN_WIDTH), F32), slab_shape, slab_shape, slab_shape,
                   jax.ShapeDtypeStruct((T, LANES), F32)],
        scratch_shapes=[pltpu.VMEM((n_slabs, tm, LANES), F32),
                        pltpu.VMEM((n_slabs, math.isqrt(SWA_RESIDUES), tm // math.isqrt(SWA_RESIDUES),
                                    LANES), F32)],
        compiler_params=pltpu.CompilerParams(dimension_semantics=("arbitrary", "arbitrary"),
                                             vmem_limit_bytes=VMEM_LIMIT_BYTES),
        name="inproj",
    )(x2d, g, wqkv, wgate, wq, wk, wv, ws)


def _unit_lower_inverses(mats, eye, xor_ij):
    n = mats[0].shape[0]
    ds = [eye - jnp.where(xor_ij < 2, a, 0.0) for a in mats]
    s = 2
    while s < n:
        band = jnp.logical_and(xor_ij >= s, xor_ij < 2 * s)
        dbs = [d.astype(BF16) for d in ds]
        eds = [jnp.dot(jnp.where(band, a, 0.0).astype(BF16), db,
                       preferred_element_type=F32).astype(BF16) for a, db in zip(mats, dbs)]
        ds = [d - jnp.dot(db, ed, preferred_element_type=F32) for d, db, ed in zip(ds, dbs, eds)]
        s *= 2
    return ds


def _gdn_kernel(qkv_ref, gate_ref, sm_ref, convw_ref, alog_ref, dtb_ref, ong_ref, ltri_ref,
                o_ref, ext_scr, q_scr, k_scr, v_scr, g_scr, beta_scr, state_scr, *, tb, chunk):
    width = 3 * GDN_WIDTH
    hd = GDN_HEAD_DIM
    halo = SUBLANES
    n_groups = width // hd

    pitch = CONV_ROW_PITCH
    at = lambda first, n: pl.ds(pitch * first, n, stride=pitch)

    @pl.when(pl.program_id(1) == 0)
    def _():
        for j in range(n_groups):
            ext_scr[j, at(0, halo), :] = jnp.zeros((halo, hd), F32)
        state_scr[...] = jnp.zeros_like(state_scr)

    @pl.when(pl.program_id(1) > 0)
    def _():
        for j in range(n_groups):
            ext_scr[j, at(0, halo), :] = ext_scr[j, at(tb, halo), :]

    for j in range(n_groups):
        ext_scr[j, at(halo, tb), :] = qkv_ref[:, j * hd:(j + 1) * hd]

    for j in range(n_groups):
        cols = slice(j * hd, (j + 1) * hd)
        acc = None
        for i in range(CONV_WIDTH):
            term = convw_ref[i:i + 1, cols] * ext_scr[j, at(halo - (CONV_WIDTH - 1) + i, tb), :]
            acc = term if acc is None else acc + term
        y = _silu(acc)
        kind, head = divmod(j, GDN_HEADS)
        hcols = slice(head * hd, (head + 1) * hd)
        if kind < 2:
            y = y * lax.rsqrt(jnp.sum(y * y, axis=-1, keepdims=True) + 1e-6)
        if kind == 0:
            q_scr[:, hcols] = y * (hd ** -0.5)
        elif kind == 1:
            k_scr[:, hcols] = y
        else:
            v_scr[:, hcols] = y

    sm = sm_ref[...]
    beta_scr[...] = jax.nn.sigmoid(sm)
    g_scr[...] = -jnp.exp(alog_ref[...]) * _softplus(sm + dtb_ref[...])

    ii = lax.broadcasted_iota(jnp.int32, (chunk, chunk), 0)
    jj = lax.broadcasted_iota(jnp.int32, (chunk, chunk), 1)
    xor_ij = jnp.bitwise_xor(ii, jj)
    lower = ii >= jj
    strict = ii > jj
    eye = jnp.where(ii == jj, 1.0, 0.0).astype(F32)
    ltri = ltri_ref[...]
    ong = ong_ref[...]

    nc = tb // chunk
    heads = range(GDN_HEADS)
    rows = [slice(c * chunk, (c + 1) * chunk) for c in range(nc)]
    hcols = [slice(h * hd, (h + 1) * hd) for h in heads]
    probs = [(c, h) for c in range(nc) for h in heads]

    gcs = []
    for c in range(nc):
        g = g_scr[rows[c], :]
        g_hi = g.astype(BF16)
        rem = g - g_hi.astype(F32)
        g_mid = rem.astype(BF16)
        g_lo = (rem - g_mid.astype(F32)).astype(BF16)
        gcs.append(jnp.dot(ltri, g_hi, preferred_element_type=F32)
                   + jnp.dot(ltri, g_mid, preferred_element_type=F32)
                   + jnp.dot(ltri, g_lo, preferred_element_type=F32))
    gcts = [gc.T for gc in gcs]
    lane_of = lambda h: slice(GDN_HEADS + h, GDN_HEADS + h + 1)
    gcol = {(c, h): gcs[c][:, lane_of(h)] for c, h in probs}
    glast = {(c, h): gcs[c][chunk - 1:chunk, lane_of(h)] for c, h in probs}
    bcol = {(c, h): beta_scr[rows[c], h:h + 1] for c, h in probs}

    qks = {}
    for c, h in probs:
        kcb = k_scr[rows[c], hcols[h]].astype(BF16)
        qkb = jnp.concatenate([q_scr[rows[c], hcols[h]].astype(BF16), kcb], axis=0)
        qks[c, h] = lax.dot_general(qkb, kcb, NT_DIMS, preferred_element_type=F32)
    attn, a_mats = {}, []
    for c, h in probs:
        grow = gcts[c][lane_of(h), :]
        decay = jnp.exp(jnp.where(lower, gcol[c, h] - grow, -jnp.inf))
        attn[c, h] = (qks[c, h][:chunk] * decay).astype(BF16)
        a_mats.append(jnp.where(strict, qks[c, h][chunk:] * decay * bcol[c, h], 0.0))
    t_invs = _unit_lower_inverses(a_mats, eye, xor_ij)

    uws = {}
    for (c, h), t_inv in zip(probs, t_invs):
        kc = k_scr[rows[c], hcols[h]]
        eg = jnp.exp(gcol[c, h])
        rhs = jnp.concatenate([v_scr[rows[c], hcols[h]] * bcol[c, h], kc * (bcol[c, h] * eg)],
                              axis=1).astype(BF16)
        uws[c, h] = jnp.dot(t_inv.astype(BF16), rhs, preferred_element_type=F32).astype(BF16)
    kd_uw, gq, attn_u = {}, {}, {}
    for c, h in probs:
        k_dec = (k_scr[rows[c], hcols[h]] * jnp.exp(glast[c, h] - gcol[c, h])).astype(BF16)
        kd_uw[c, h] = lax.dot_general(k_dec, uws[c, h], TN_DIMS, preferred_element_type=F32)
    for c, h in probs:
        a_uw = jnp.dot(attn[c, h], uws[c, h], preferred_element_type=F32)
        q_eff = q_scr[rows[c], hcols[h]] * jnp.exp(gcol[c, h]) - a_uw[:, hd:]
        gq[c, h] = jnp.concatenate([kd_uw[c, h][:, hd:], q_eff], axis=0).astype(BF16)
        attn_u[c, h] = a_uw[:, :hd]

    state = [state_scr[h] for h in heads]
    for c in range(nc):
        gs = [jnp.dot(gq[c, h], state[h].astype(BF16), preferred_element_type=F32) for h in heads]
        state = [state[h] * jnp.exp(glast[c, h]) - gs[h][:hd] + kd_uw[c, h][:, :hd] for h in heads]
        for h in heads:
            o = gs[h][hd:] + attn_u[c, h]
            gated = _rms(o, ong) * _silu(gate_ref[rows[c], hcols[h]])
            o_ref[rows[c], hcols[h]] = gated.astype(o_ref.dtype)
    for h in heads:
        state_scr[h] = state[h]


def _gdn(qkv, gate, sm, conv_w, alog_row, dtb_row, onorm_g, batch, seq, tb):
    T = batch * seq
    nb = seq // tb
    width = 3 * GDN_WIDTH
    idx = jnp.arange(CHUNK)
    ltri = (idx[:, None] >= idx[None, :]).astype(BF16)
    row = lambda n: pl.BlockSpec((tb, n), lambda b, t: (b * nb + t, 0))
    kern = functools.partial(_gdn_kernel, tb=tb, chunk=CHUNK)
    return pl.pallas_call(
        kern,
        grid=(batch, nb),
        in_specs=[row(width), row(GDN_WIDTH), row(LANES), _resident(conv_w.shape),
                  _resident((1, LANES)), _resident((1, LANES)), _resident((1, GDN_HEAD_DIM)),
                  _resident((CHUNK, CHUNK))],
        out_specs=row(GDN_WIDTH),
        out_shape=jax.ShapeDtypeStruct((T, GDN_WIDTH), BF16),
        scratch_shapes=[pltpu.VMEM((width // GDN_HEAD_DIM, CONV_ROW_PITCH * (tb + SUBLANES),
                                    GDN_HEAD_DIM), F32),
                        pltpu.VMEM((tb, GDN_WIDTH), F32),
                        pltpu.VMEM((tb, GDN_WIDTH), F32),
                        pltpu.VMEM((tb, GDN_WIDTH), F32),
                        pltpu.VMEM((tb, LANES), F32),
                        pltpu.VMEM((tb, LANES), F32),
                        pltpu.VMEM((GDN_HEADS, GDN_HEAD_DIM, GDN_HEAD_DIM), F32)],
        compiler_params=pltpu.CompilerParams(dimension_semantics=("arbitrary", "arbitrary"),
                                             vmem_limit_bytes=VMEM_LIMIT_BYTES),
        name="gdn",
    )(qkv, gate, sm, conv_w, alog_row, dtb_row, onorm_g, ltri)


def _swa_kernel(q_ref, k_ref, v_ref, bias_ref, sel_ref, o_ref, acc_scr, m_scr, s_scr, out_scr,
                *, w):
    n_res = SWA_RESIDUES
    tile = pl.program_id(1)
    n_slabs = SWA_WIDTH // LANES
    heads_per_slab = LANES // SWA_HEAD_DIM
    assert heads_per_slab == 2
    scale = SWA_HEAD_DIM ** -0.5 * math.log2(math.e)
    lane = lax.broadcasted_iota(jnp.int32, (w, LANES), 1)
    low_half = lane < SWA_HEAD_DIM

    def expand(packed):
        hi = packed.astype(BF16)
        lo = (packed - hi.astype(F32)).astype(BF16)
        return jnp.dot(jnp.concatenate([hi, lo], axis=1), sel_ref[...], preferred_element_type=F32)

    def task(pat, dil, t):
        first = pat == 0
        n_chunks = n_res // dil
        c = w // n_chunks
        res_d = jnp.bitwise_and(t, dil - 1)
        blk = jnp.right_shift(t, dil.bit_length() - 1)
        n_glob = tile * n_chunks + blk
        q0 = pl.multiple_of(blk * c, c)
        cur0 = pl.multiple_of(n_glob * c, c)
        prev0 = pl.multiple_of(jnp.maximum(n_glob - 1, 0) * c, c)
        is_first_blk = (n_glob == 0).astype(jnp.int32)
        res_of = [a * dil + res_d for a in range(n_chunks)]

        def gather(ref, lead, start):
            return jnp.concatenate([ref[lead + (res, pl.ds(start, c), slice(None))]
                                    for res in res_of], axis=0)

        logits, vps = [], []
        for slab in range(n_slabs):
            qp = (gather(q_ref, (slab,), q0) * scale).astype(BF16)
            kp = jnp.concatenate([gather(k_ref, (slab,), prev0), gather(k_ref, (slab,), cur0)],
                                 axis=0).astype(BF16)
            vps.append(jnp.concatenate([gather(v_ref, (slab,), prev0), gather(v_ref, (slab,), cur0)],
                                       axis=0).astype(BF16))
            for e in range(heads_per_slab):
                mine = low_half if e == 0 else jnp.logical_not(low_half)
                qh = jnp.where(mine, qp, jnp.zeros_like(qp))
                lg = lax.dot_general(qh, kp, NT_DIMS, preferred_element_type=F32)
                logits.append(lg + bias_ref[pat, is_first_blk, slab * heads_per_slab + e])
        m_blk = jnp.zeros((w, LANES), F32)
        s_blk = jnp.zeros((w, LANES), F32)
        ps = []
        for h, lg in enumerate(logits):
            m_h = jnp.max(lg, axis=-1, keepdims=True)
            p = jnp.exp2(lg - m_h)
            ps.append(p.astype(BF16))
            m_blk = jnp.where(lane == h, m_h, m_blk)
            s_blk = jnp.where(lane == h, jnp.sum(p, axis=-1, keepdims=True), s_blk)
        nums = []
        for slab in range(n_slabs):
            h0 = slab * heads_per_slab
            outs = [jnp.dot(ps[h0 + e], vps[slab], preferred_element_type=F32)
                    for e in range(heads_per_slab)]
            nums.append(jnp.where(low_half, outs[0], outs[1]))
        if first:
            m_new, s_new = m_blk, s_blk
        else:
            m_old = gather(m_scr, (), q0)
            m_new = jnp.maximum(m_old, m_blk)
            a_old = jnp.exp2(m_old - m_new)
            a_blk = jnp.exp2(m_blk - m_new)
            s_new = a_old * gather(s_scr, (), q0) + a_blk * s_blk
            w_old, w_blk = expand(a_old), expand(a_blk)
            for slab in range(n_slabs):
                cols = slice(slab * LANES, (slab + 1) * LANES)
                nums[slab] = (w_old[:, cols] * gather(acc_scr, (slab,), q0)
                              + w_blk[:, cols] * nums[slab])
        for idx, res in enumerate(res_of):
            rows, part = pl.ds(q0, c), slice(idx * c, (idx + 1) * c)
            for slab in range(n_slabs):
                acc_scr[slab, res, rows, :] = nums[slab][part]
            m_scr[res, rows, :] = m_new[part]
            s_scr[res, rows, :] = s_new[part]

    n_tasks = n_res
    for pat, (window, dil) in enumerate(DILATED_PATTERNS):
        assert window // dil == w and n_res % dil == 0 and (w * dil) % n_res == 0

        def body(t, carry, pat=pat, dil=dil):
            task(pat, dil, t)
            return carry
        lax.fori_loop(0, n_tasks, body, 0, unroll=4)

    def finish(r, carry):
        den = expand(s_scr[r])
        for slab in range(n_slabs):
            cols = slice(slab * LANES, (slab + 1) * LANES)
            out_scr[slab, pl.ds(r, w, stride=n_res), :] = acc_scr[slab, r] / den[:, cols]
        return carry
    lax.fori_loop(0, n_res, finish, 0, unroll=4)
    for slab in range(n_slabs):
        o_ref[:, slab * LANES:(slab + 1) * LANES] = out_scr[slab].astype(o_ref.dtype)


def _swa(qb, kb, vb, bias, batch, seq):
    n_slabs = SWA_WIDTH // LANES
    n_res = SWA_RESIDUES
    w = DILATED_PATTERNS[0][0] // DILATED_PATTERNS[0][1]
    tq = n_res * w
    nq = seq // tq
    q_blk = pl.BlockSpec((n_slabs, None, n_res, w, LANES), lambda b, i: (0, b, 0, i, 0))
    seq_blk = pl.BlockSpec((n_slabs, None, n_res, seq // n_res, LANES), lambda b, i: (0, b, 0, 0, 0),
                           pipeline_mode=pl.Buffered(1))
    head_of_col = jnp.arange(SWA_WIDTH) // SWA_HEAD_DIM
    sel = (jnp.arange(LANES)[:, None] == head_of_col[None, :]).astype(BF16)
    sel = jnp.concatenate([sel, sel], axis=0)
    return pl.pallas_call(
        functools.partial(_swa_kernel, w=w),
        grid=(batch, nq),
        in_specs=[q_blk, seq_blk, seq_blk, _resident(bias.shape), _resident(sel.shape)],
        out_specs=pl.BlockSpec((tq, SWA_WIDTH), lambda b, i: (b * nq + i, 0)),
        out_shape=jax.ShapeDtypeStruct((batch * seq, SWA_WIDTH), BF16),
        scratch_shapes=[pltpu.VMEM((n_slabs, n_res, w, LANES), F32),
                        pltpu.VMEM((n_res, w, LANES), F32), pltpu.VMEM((n_res, w, LANES), F32),
                        pltpu.VMEM((n_slabs, tq, LANES), F32)],
        compiler_params=pltpu.CompilerParams(dimension_semantics=("arbitrary", "arbitrary"),
                                             vmem_limit_bytes=VMEM_LIMIT_BYTES),
        name="swa",
    )(qb, kb, vb, bias, sel)


def _t5_causal_bucket(dist):
    max_exact = NUM_BUCKETS // 2
    d = jnp.maximum(dist, 1).astype(F32)
    log_b = max_exact + (jnp.log(d / max_exact) / math.log(MAX_DISTANCE / max_exact)
                         * (NUM_BUCKETS - max_exact)).astype(jnp.int32)
    return jnp.where(dist < max_exact, dist, jnp.minimum(log_b, NUM_BUCKETS - 1))


def _swa_bias_tables(rel_bias):
    w = DILATED_PATTERNS[0][0] // DILATED_PATTERNS[0][1]
    rels, steps = [], []
    for window, dil in DILATED_PATTERNS:
        n_chunks = SWA_RESIDUES // dil
        c = w // n_chunks
        pos = (np.arange(w) % c) * n_chunks + np.arange(w) // c
        rels.append(pos[:, None] + w - np.concatenate([pos, w + pos])[None, :])
        steps.append(rel_bias[_t5_causal_bucket(jnp.arange(w + 1) * dil)].astype(F32))
    rel = np.stack(rels)
    one_hot = (jnp.asarray(np.clip(rel, 0, w))[..., None] == jnp.arange(w + 1)).astype(F32)
    table = jnp.einsum("prh,pijr->phij", jnp.stack(steps), one_hot,
                       precision=lax.Precision.HIGHEST) * math.log2(math.e)
    in_band = np.logical_and(rel >= 0, rel <= w)[:, None]
    no_prev = np.arange(2 * w) >= w
    keep = np.stack([in_band, np.logical_and(in_band, no_prev)], axis=1)
    return jnp.where(jnp.asarray(keep), table[:, None], -jnp.inf)


def _mixffn_kernel(x_ref, oa_ref, ob_ref, woa_ref, wob_ref, gpost_ref, gpre_ref, gfpost_ref,
                   wg_ref, wu_ref, wd_ref, out_ref, act_scr, x1_scr, *, ff_chunk, n_sub):
    sub = x_ref.shape[0] // n_sub
    rows = [slice(i * sub, (i + 1) * sub) for i in range(n_sub)]
    d_ff = wg_ref.shape[1]
    mixes = [jnp.dot(oa_ref[r, :], woa_ref[...], preferred_element_type=F32)
             + jnp.dot(ob_ref[r, :], wob_ref[...], preferred_element_type=F32) for r in rows]
    for r, mix in zip(rows, mixes):
        x1 = x_ref[r, :] + _rms(mix, gpost_ref[...])
        x1_scr[r, :] = x1
        h = _rms(x1, gpre_ref[...]).astype(BF16)
        for c in range(d_ff // ff_chunk):
            cols = slice(c * ff_chunk, (c + 1) * ff_chunk)
            gate = jnp.dot(h, wg_ref[:, cols], preferred_element_type=F32)
            up = jnp.dot(h, wu_ref[:, cols], preferred_element_type=F32)
            act_scr[r, cols] = (_silu(gate) * up).astype(BF16)
    fs = [jnp.dot(act_scr[r, :], wd_ref[...], preferred_element_type=F32) for r in rows]
    for r, f in zip(rows, fs):
        out_ref[r, :] = x1_scr[r, :] + _rms(f, gfpost_ref[...])


def _mixffn(x2d, oa, ob, woa, wob, gpost, gpre, gfpost, wg, wu, wd, tm, sub, ff_chunk):
    T, D = x2d.shape
    d_ff = wg.shape[1]
    row = lambda n: pl.BlockSpec((tm, n), lambda i: (i, 0))
    return pl.pallas_call(
        functools.partial(_mixffn_kernel, ff_chunk=ff_chunk, n_sub=tm // sub),
        grid=(T // tm,),
        in_specs=([row(D), row(GDN_WIDTH), row(SWA_WIDTH)]
                  + [_resident(woa.shape), _resident(wob.shape)] + [_resident((1, D))] * 3
                  + [_resident(wg.shape), _resident(wu.shape), _resident(wd.shape)]),
        out_specs=row(D),
        out_shape=jax.ShapeDtypeStruct((T, D), F32),
        scratch_shapes=[pltpu.VMEM((tm, d_ff), BF16), pltpu.VMEM((tm, D), F32)],
        compiler_params=pltpu.CompilerParams(dimension_semantics=("arbitrary",),
                                             vmem_limit_bytes=VMEM_LIMIT_BYTES),
        name="mixffn",
    )(x2d, oa, ob, woa, wob, gpost, gpre, gfpost, wg, wu, wd)


def _layer(x2d, batch, seq, w_in, conv_w, a_log, dt_bias, onorm_g, rel_bias, w_out,
           g_mix_pre, g_mix_post, w_gate, w_up, w_down, g_ffn_pre, g_ffn_post):
    D = x2d.shape[1]
    gw, sw, nh = GDN_WIDTH, SWA_WIDTH, GDN_HEADS
    wb = w_in.astype(BF16)
    c_gate, c_small, c_q = 3 * gw, 4 * gw, 4 * gw + 2 * nh
    w_small = jnp.pad(wb[:, c_small:c_q], ((0, 0), (0, LANES - 2 * nh)))
    qkv, gate, qb, kb, vb, sm = _inproj(
        x2d, g_mix_pre.reshape(1, D), wb[:, :c_gate], wb[:, c_gate:c_small],
        wb[:, c_q:c_q + sw], wb[:, c_q + sw:c_q + 2 * sw], wb[:, c_q + 2 * sw:], w_small,
        batch, seq, tm=512)

    lane_pad = lambda v: jnp.pad(v.astype(F32).reshape(1, nh), ((0, 0), (nh, LANES - 2 * nh)))
    oa = _gdn(qkv, gate, sm, conv_w.astype(F32), lane_pad(a_log), lane_pad(dt_bias),
              onorm_g.astype(F32).reshape(1, GDN_HEAD_DIM), batch, seq, tb=512)

    ob = _swa(qb, kb, vb, _swa_bias_tables(rel_bias), batch, seq)

    wo = w_out.astype(BF16)
    return _mixffn(x2d, oa, ob, wo[:gw], wo[gw:], g_mix_post.reshape(1, D),
                   g_ffn_pre.reshape(1, D), g_ffn_post.reshape(1, D), w_gate.astype(BF16),
                   w_up.astype(BF16), w_down.astype(BF16), tm=1024, sub=256, ff_chunk=256)


def kernel(x, w_in, conv_w, a_log, dt_bias, onorm_g, rel_bias, w_out, g_mix_pre, g_mix_post,
           w_gate, w_up, w_down, g_ffn_pre, g_ffn_post):
    batch, seq, d_model = x.shape
    x2d = x.reshape(batch * seq, d_model)
    for l in range(w_in.shape[0]):
        x2d = _layer(x2d, batch, seq, w_in[l], conv_w[l], a_log[l], dt_bias[l], onorm_g[l],
                     rel_bias, w_out[l], g_mix_pre[l], g_mix_post[l], w_gate[l], w_up[l],
                     w_down[l], g_ffn_pre[l], g_ffn_post[l])
    return x2d.reshape(batch, seq, d_model)
```

```python
import functools
import math

import jax
import jax.numpy as jnp
import numpy as np
from jax import lax
from jax.experimental import pallas as pl
from jax.experimental.pallas import tpu as pltpu

F32 = jnp.float32
BF16 = jnp.bfloat16

GDN_HEADS = 4
GDN_HEAD_DIM = 128
GDN_WIDTH = GDN_HEADS * GDN_HEAD_DIM
CONV_WIDTH = 4
CONV_ROW_PITCH = 2
CHUNK = 64
SWA_HEADS = 8
SWA_HEAD_DIM = 64
SWA_WIDTH = SWA_HEADS * SWA_HEAD_DIM
DILATED_PATTERNS = ((128, 1), (512, 4), (2048, 16))
SWA_RESIDUES = max(dil for _, dil in DILATED_PATTERNS)
NUM_BUCKETS = 32
MAX_DISTANCE = 2048
RMS_EPS = 1e-6

LANES = 128
SUBLANES = 8
BF16_ROWS = 16
SWA_Q_SCALE = SWA_HEAD_DIM ** -0.5 * math.log2(math.e)
SWA_TASK_UNROLL = 4
VMEM_LIMIT_BYTES = 56 * 1024 * 1024

NT_DIMS = (((1,), (1,)), ((), ()))
TN_DIMS = (((0,), (0,)), ((), ()))


def _rms(x, g):
    return x * lax.rsqrt(jnp.mean(x * x, axis=-1, keepdims=True) + RMS_EPS) * g


def _silu(x):
    return x * jax.nn.sigmoid(x)


def _softplus(x):
    return jnp.maximum(x, 0.0) + jnp.log1p(jnp.exp(-jnp.abs(x)))


def _resident(shape):
    zeros = (0,) * len(shape)
    return pl.BlockSpec(shape, lambda *_: zeros, pipeline_mode=pl.Buffered(1))


def _inproj_kernel(x_ref, g_ref, wqkv_ref, wgate_ref, wq_ref, wk_ref, wv_ref, ws_ref,
                   qkv_ref, gate_ref, qb_ref, kb_ref, vb_ref, sm_ref, stage_scr, stage2_scr):
    h = _rms(x_ref[...], g_ref[...]).astype(BF16)
    for w_ref, o_ref in ((wqkv_ref, qkv_ref), (wgate_ref, gate_ref), (ws_ref, sm_ref)):
        o_ref[...] = jnp.dot(h, w_ref[...], preferred_element_type=F32)
    tm = x_ref.shape[0]
    f = math.isqrt(SWA_RESIDUES)
    assert f * f == SWA_RESIDUES
    for w_ref, o_ref, scale in ((wq_ref, qb_ref, SWA_Q_SCALE), (wk_ref, kb_ref, None),
                                (wv_ref, vb_ref, None)):
        res = jnp.dot(h, w_ref[...], preferred_element_type=F32)
        if scale is not None:
            res = res * scale
        for slab in range(SWA_WIDTH // LANES):
            stage_scr[slab] = res[:, slab * LANES:(slab + 1) * LANES]
        for slab in range(SWA_WIDTH // LANES):
            for a in range(f):
                stage2_scr[slab, a] = stage_scr[slab, pl.ds(a, tm // f, stride=f), :]
        for slab in range(SWA_WIDTH // LANES):
            for a in range(f):
                for b in range(f):
                    o_ref[slab, f * b + a] = stage2_scr[
                        slab, a, pl.ds(b, tm // SWA_RESIDUES, stride=f), :].astype(o_ref.dtype)


def _inproj(x2d, g, wqkv, wgate, wq, wk, wv, ws, batch, seq, tm):
    T, D = x2d.shape
    nt = seq // tm
    row = lambda n: pl.BlockSpec((tm, n), lambda b, j: (b * nt + j, 0))
    n_slabs = SWA_WIDTH // LANES
    slabs = pl.BlockSpec((n_slabs, None, SWA_RESIDUES, tm // SWA_RESIDUES, LANES),
                         lambda b, j: (0, b, 0, j, 0))
    slab_shape = jax.ShapeDtypeStruct(
        (n_slabs, batch, SWA_RESIDUES, seq // SWA_RESIDUES, LANES), BF16)
    return pl.pallas_call(
        _inproj_kernel,
        grid=(batch, nt),
        in_specs=[row(D), _resident((1, D))] + [_resident(w.shape) for w in (wqkv, wgate, wq, wk, wv, ws)],
        out_specs=[row(3 * GDN_WIDTH), row(GDN_WIDTH), slabs, slabs, slabs, row(LANES)],
        out_shape=[jax.ShapeDtypeStruct((T, 3 * GDN_WIDTH), F32),
                   jax.ShapeDtypeStruct((T, GDN_WIDTH), F32), slab_shape, slab_shape, slab_shape,
                   jax.ShapeDtypeStruct((T, LANES), F32)],
        scratch_shapes=[pltpu.VMEM((n_slabs, tm, LANES), F32),
                        pltpu.VMEM((n_slabs, math.isqrt(SWA_RESIDUES), tm // math.isqrt(SWA_RESIDUES),
                                    LANES), F32)],
        compiler_params=pltpu.CompilerParams(dimension_semantics=("arbitrary", "arbitrary"),
                                             vmem_limit_bytes=VMEM_LIMIT_BYTES),
        name="inproj",
    )(x2d, g, wqkv, wgate, wq, wk, wv, ws)


def _unit_lower_inverses(mats, eye, xor_ij):
    n = mats[0].shape[0]
    ds = [eye - jnp.where(xor_ij < 2, a, 0.0) for a in mats]
    s = 2
    while s < n:
        band = jnp.logical_and(xor_ij >= s, xor_ij < 2 * s)
        dbs = [d.astype(BF16) for d in ds]
        eds = [jnp.dot(jnp.where(band, a, 0.0).astype(BF16), db,
                       preferred_element_type=F32).astype(BF16) for a, db in zip(mats, dbs)]
        ds = [d - jnp.dot(db, ed, preferred_element_type=F32) for d, db, ed in zip(ds, dbs, eds)]
        s *= 2
    return ds


def _gdn_kernel(qkv_ref, gate_ref, sm_ref, convw_ref, alog_ref, dtb_ref, ong_ref, ltri_ref,
                o_ref, ext_scr, q_scr, k_scr, v_scr, g_scr, beta_scr, state_scr, *, tb, chunk):
    width = 3 * GDN_WIDTH
    hd = GDN_HEAD_DIM
    halo = SUBLANES
    n_groups = width // hd

    pitch = CONV_ROW_PITCH
    at = lambda first, n: pl.ds(pitch * first, n, stride=pitch)

    @pl.when(pl.program_id(1) == 0)
    def _():
        for j in range(n_groups):
            ext_scr[j, at(0, halo), :] = jnp.zeros((halo, hd), F32)
        state_scr[...] = jnp.zeros_like(state_scr)

    @pl.when(pl.program_id(1) > 0)
    def _():
        for j in range(n_groups):
            ext_scr[j, at(0, halo), :] = ext_scr[j, at(tb, halo), :]

    for j in range(n_groups):
        ext_scr[j, at(halo, tb), :] = qkv_ref[:, j * hd:(j + 1) * hd]

    for j in range(n_groups):
        cols = slice(j * hd, (j + 1) * hd)
        acc = None
        for i in range(CONV_WIDTH):
            term = convw_ref[i:i + 1, cols] * ext_scr[j, at(halo - (CONV_WIDTH - 1) + i, tb), :]
            acc = term if acc is None else acc + term
        y = _silu(acc)
        kind, head = divmod(j, GDN_HEADS)
        hcols = slice(head * hd, (head + 1) * hd)
        if kind < 2:
            y = y * lax.rsqrt(jnp.sum(y * y, axis=-1, keepdims=True) + 1e-6)
        if kind == 0:
            q_scr[:, hcols] = y * (hd ** -0.5)
        elif kind == 1:
            k_scr[:, hcols] = y
        else:
            v_scr[:, hcols] = y

    sm = sm_ref[...]
    beta_scr[...] = jax.nn.sigmoid(sm)
    g_scr[...] = -jnp.exp(alog_ref[...]) * _softplus(sm + dtb_ref[...])

    ii = lax.broadcasted_iota(jnp.int32, (chunk, chunk), 0)
    jj = lax.broadcasted_iota(jnp.int32, (chunk, chunk), 1)
    xor_ij = jnp.bitwise_xor(ii, jj)
    lower = ii >= jj
    strict = ii > jj
    eye = jnp.where(ii == jj, 1.0, 0.0).astype(F32)
    ltri = ltri_ref[...]
    ong = ong_ref[...]

    nc = tb // chunk
    heads = range(GDN_HEADS)
    rows = [slice(c * chunk, (c + 1) * chunk) for c in range(nc)]
    hcols = [slice(h * hd, (h + 1) * hd) for h in heads]
    probs = [(c, h) for c in range(nc) for h in heads]

    gcs = []
    for c in range(nc):
        g = g_scr[rows[c], :]
        g_hi = g.astype(BF16)
        rem = g - g_hi.astype(F32)
        g_mid = rem.astype(BF16)
        g_lo = (rem - g_mid.astype(F32)).astype(BF16)
        gcs.append(jnp.dot(ltri, g_hi, preferred_element_type=F32)
                   + jnp.dot(ltri, g_mid, preferred_element_type=F32)
                   + jnp.dot(ltri, g_lo, preferred_element_type=F32))
    gcts = [gc.T for gc in gcs]
    lane_of = lambda h: slice(GDN_HEADS + h, GDN_HEADS + h + 1)
    gcol = {(c, h): gcs[c][:, lane_of(h)] for c, h in probs}
    glast = {(c, h): gcs[c][chunk - 1:chunk, lane_of(h)] for c, h in probs}
    bcol = {(c, h): beta_scr[rows[c], h:h + 1] for c, h in probs}

    qks = {}
    for c, h in probs:
        kcb = k_scr[rows[c], hcols[h]].astype(BF16)
        qkb = jnp.concatenate([q_scr[rows[c], hcols[h]].astype(BF16), kcb], axis=0)
        qks[c, h] = lax.dot_general(qkb, kcb, NT_DIMS, preferred_element_type=F32)
    attn, a_mats = {}, []
    for c, h in probs:
        grow = gcts[c][lane_of(h), :]
        decay = jnp.exp(jnp.where(lower, gcol[c, h] - grow, -jnp.inf))
        attn[c, h] = (qks[c, h][:chunk] * decay).astype(BF16)
        a_mats.append(jnp.where(strict, qks[c, h][chunk:] * decay * bcol[c, h], 0.0))
    t_invs = _unit_lower_inverses(a_mats, eye, xor_ij)

    uws = {}
    for (c, h), t_inv in zip(probs, t_invs):
        kc = k_scr[rows[c], hcols[h]]
        eg = jnp.exp(gcol[c, h])
        rhs = jnp.concatenate([v_scr[rows[c], hcols[h]] * bcol[c, h], kc * (bcol[c, h] * eg)],
                              axis=1).astype(BF16)
        uws[c, h] = jnp.dot(t_inv.astype(BF16), rhs, preferred_element_type=F32).astype(BF16)
    kd_uw, gq, attn_u = {}, {}, {}
    for c, h in probs:
        k_dec = (k_scr[rows[c], hcols[h]] * jnp.exp(glast[c, h] - gcol[c, h])).astype(BF16)
        kd_uw[c, h] = lax.dot_general(k_dec, uws[c, h], TN_DIMS, preferred_element_type=F32)
    for c, h in probs:
        a_uw = jnp.dot(attn[c, h], uws[c, h], preferred_element_type=F32)
        q_eff = q_scr[rows[c], hcols[h]] * jnp.exp(gcol[c, h]) - a_uw[:, hd:]
        gq[c, h] = jnp.concatenate([kd_uw[c, h][:, hd:], q_eff], axis=0).astype(BF16)
        attn_u[c, h] = a_uw[:, :hd]

    state = [state_scr[h] for h in heads]
    for c in range(nc):
        gs = [jnp.dot(gq[c, h], state[h].astype(BF16), preferred_element_type=F32) for h in heads]
        state = [state[h] * jnp.exp(glast[c, h]) - gs[h][:hd] + kd_uw[c, h][:, :hd] for h in heads]
        for h in heads:
            o = gs[h][hd:] + attn_u[c, h]
            gated = _rms(o, ong) * _silu(gate_ref[rows[c], hcols[h]])
            o_ref[rows[c], hcols[h]] = gated.astype(o_ref.dtype)
    for h in heads:
        state_scr[h] = state[h]


def _gdn(qkv, gate, sm, conv_w, alog_row, dtb_row, onorm_g, batch, seq, tb):
    T = batch * seq
    nb = seq // tb
    width = 3 * GDN_WIDTH
    idx = jnp.arange(CHUNK)
    ltri = (idx[:, None] >= idx[None, :]).astype(BF16)
    row = lambda n: pl.BlockSpec((tb, n), lambda b, t: (b * nb + t, 0))
    kern = functools.partial(_gdn_kernel, tb=tb, chunk=CHUNK)
    return pl.pallas_call(
        kern,
        grid=(batch, nb),
        in_specs=[row(width), row(GDN_WIDTH), row(LANES), _resident(conv_w.shape),
                  _resident((1, LANES)), _resident((1, LANES)), _resident((1, GDN_HEAD_DIM)),
                  _resident((CHUNK, CHUNK))],
        out_specs=row(GDN_WIDTH),
        out_shape=jax.ShapeDtypeStruct((T, GDN_WIDTH), BF16),
        scratch_shapes=[pltpu.VMEM((width // GDN_HEAD_DIM, CONV_ROW_PITCH * (tb + SUBLANES),
                                    GDN_HEAD_DIM), F32),
                        pltpu.VMEM((tb, GDN_WIDTH), F32),
                        pltpu.VMEM((tb, GDN_WIDTH), F32),
                        pltpu.VMEM((tb, GDN_WIDTH), F32),
                        pltpu.VMEM((tb, LANES), F32),
                        pltpu.VMEM((tb, LANES), F32),
                        pltpu.VMEM((GDN_HEADS, GDN_HEAD_DIM, GDN_HEAD_DIM), F32)],
        compiler_params=pltpu.CompilerParams(dimension_semantics=("arbitrary", "arbitrary"),
                                             vmem_limit_bytes=VMEM_LIMIT_BYTES),
        name="gdn",
    )(qkv, gate, sm, conv_w, alog_row, dtb_row, onorm_g, ltri)


def _swa_kernel(q_ref, k_ref, v_ref, bias_ref, sel_ref, o_ref, acc_scr, m_scr, s_scr, out_scr,
                *, w):
    n_res = SWA_RESIDUES
    tile = pl.program_id(1)
    n_slabs = SWA_WIDTH // LANES
    heads_per_slab = LANES // SWA_HEAD_DIM
    assert heads_per_slab == 2
    tasks_per_call = lambda dil: 2 if (w * dil // n_res) % BF16_ROWS else 1
    lane = lax.broadcasted_iota(jnp.int32, (w, LANES), 1)
    low_half = lane < SWA_HEAD_DIM

    def expand(packed):
        hi = packed.astype(BF16)
        lo = (packed - hi.astype(F32)).astype(BF16)
        return jnp.dot(jnp.concatenate([hi, lo], axis=1), sel_ref[...], preferred_element_type=F32)

    def task(pat, dil, t, sub):
        first = pat == 0
        n_chunks = n_res // dil
        c = w // n_chunks
        per_call = tasks_per_call(dil)
        t = t * per_call + sub
        res_d = jnp.bitwise_and(t, dil - 1)
        blk = jnp.right_shift(t, dil.bit_length() - 1)
        n_glob = tile * n_chunks + blk
        q0 = pl.multiple_of(blk * c, c)
        cur0 = pl.multiple_of(n_glob * c, c)
        prev0 = pl.multiple_of(jnp.maximum(n_glob - 1, 0) * c, c)
        is_first_blk = (n_glob == 0).astype(jnp.int32)
        res_of = [a * dil + res_d for a in range(n_chunks)]

        def gather(ref, lead, start):
            return jnp.concatenate([ref[lead + (res, pl.ds(start, c), slice(None))]
                                    for res in res_of], axis=0)

        def operand(ref, slab, start, shift):
            if c % BF16_ROWS == 0:
                return gather(ref, (slab,), start)
            assert BF16_ROWS == 2 * c and dil == 1 and per_call % 2 == 0
            half = (sub + shift) % 2
            tile0 = pl.multiple_of(jnp.maximum(start - half * c, 0), BF16_ROWS)
            return jnp.concatenate(
                [ref[slab, res, pl.ds(tile0, BF16_ROWS), :].astype(F32)[half * c:(half + 1) * c]
                 for res in res_of], axis=0).astype(BF16)

        logits, vps = [], []
        for slab in range(n_slabs):
            qp = operand(q_ref, slab, q0, 0)
            kp = jnp.concatenate([operand(k_ref, slab, prev0, -1), operand(k_ref, slab, cur0, 0)],
                                 axis=0)
            vps.append(jnp.concatenate([operand(v_ref, slab, prev0, -1),
                                        operand(v_ref, slab, cur0, 0)], axis=0))
            for e in range(heads_per_slab):
                mine = low_half if e == 0 else jnp.logical_not(low_half)
                qh = jnp.where(mine, qp, jnp.zeros_like(qp))
                lg = lax.dot_general(qh, kp, NT_DIMS, preferred_element_type=F32)
                logits.append(lg + bias_ref[pat, is_first_blk, slab * heads_per_slab + e])
        m_blk = jnp.zeros((w, LANES), F32)
        s_blk = jnp.zeros((w, LANES), F32)
        ps = []
        for h, lg in enumerate(logits):
            m_h = jnp.max(lg, axis=-1, keepdims=True)
            p = jnp.exp2(lg - m_h)
            ps.append(p.astype(BF16))
            m_blk = jnp.where(lane == h, m_h, m_blk)
            s_blk = jnp.where(lane == h, jnp.sum(p, axis=-1, keepdims=True), s_blk)
        nums = []
        for slab in range(n_slabs):
            h0 = slab * heads_per_slab
            outs = [jnp.dot(ps[h0 + e], vps[slab], preferred_element_type=F32)
                    for e in range(heads_per_slab)]
            nums.append(jnp.where(low_half, outs[0], outs[1]))
        if first:
            m_new, s_new = m_blk, s_blk
        else:
            m_old = gather(m_scr, (), q0)
            m_new = jnp.maximum(m_old, m_blk)
            a_old = jnp.exp2(m_old - m_new)
            a_blk = jnp.exp2(m_blk - m_new)
            s_new = a_old * gather(s_scr, (), q0) + a_blk * s_blk
            w_old, w_blk = expand(a_old), expand(a_blk)
            for slab in range(n_slabs):
                cols = slice(slab * LANES, (slab + 1) * LANES)
                nums[slab] = (w_old[:, cols] * gather(acc_scr, (slab,), q0)
                              + w_blk[:, cols] * nums[slab])
        for idx, res in enumerate(res_of):
            rows, part = pl.ds(q0, c), slice(idx * c, (idx + 1) * c)
            for slab in range(n_slabs):
                acc_scr[slab, res, rows, :] = nums[slab][part]
            m_scr[res, rows, :] = m_new[part]
            s_scr[res, rows, :] = s_new[part]

    n_tasks = n_res
    for pat, (window, dil) in enumerate(DILATED_PATTERNS):
        assert window // dil == w and n_res % dil == 0 and (w * dil) % n_res == 0

        per_call = tasks_per_call(dil)

        def body(t, carry, pat=pat, dil=dil, per_call=per_call):
            for sub in range(per_call):
                task(pat, dil, t, sub)
            return carry
        lax.fori_loop(0, n_tasks // per_call, body, 0, unroll=SWA_TASK_UNROLL // per_call)

    def finish(r, carry):
        den = expand(s_scr[r])
        for slab in range(n_slabs):
            cols = slice(slab * LANES, (slab + 1) * LANES)
            out_scr[slab, pl.ds(r, w, stride=n_res), :] = acc_scr[slab, r] / den[:, cols]
        return carry
    lax.fori_loop(0, n_res, finish, 0, unroll=4)
    for slab in range(n_slabs):
        o_ref[:, slab * LANES:(slab + 1) * LANES] = out_scr[slab].astype(o_ref.dtype)


def _swa(qb, kb, vb, bias, batch, seq):
    n_slabs = SWA_WIDTH // LANES
    n_res = SWA_RESIDUES
    w = DILATED_PATTERNS[0][0] // DILATED_PATTERNS[0][1]
    tq = n_res * w
    nq = seq // tq
    q_blk = pl.BlockSpec((n_slabs, None, n_res, w, LANES), lambda b, i: (0, b, 0, i, 0))
    seq_blk = pl.BlockSpec((n_slabs, None, n_res, seq // n_res, LANES), lambda b, i: (0, b, 0, 0, 0))
    head_of_col = jnp.arange(SWA_WIDTH) // SWA_HEAD_DIM
    sel = (jnp.arange(LANES)[:, None] == head_of_col[None, :]).astype(BF16)
    sel = jnp.concatenate([sel, sel], axis=0)
    return pl.pallas_call(
        functools.partial(_swa_kernel, w=w),
        grid=(batch, nq),
        in_specs=[q_blk, seq_blk, seq_blk, _resident(bias.shape), _resident(sel.shape)],
        out_specs=pl.BlockSpec((tq, SWA_WIDTH), lambda b, i: (b * nq + i, 0)),
        out_shape=jax.ShapeDtypeStruct((batch * seq, SWA_WIDTH), BF16),
        scratch_shapes=[pltpu.VMEM((n_slabs, n_res, w, LANES), F32),
                        pltpu.VMEM((n_res, w, LANES), F32), pltpu.VMEM((n_res, w, LANES), F32),
                        pltpu.VMEM((n_slabs, tq, LANES), F32)],
        compiler_params=pltpu.CompilerParams(dimension_semantics=("arbitrary", "arbitrary"),
                                             vmem_limit_bytes=VMEM_LIMIT_BYTES),
        name="swa",
    )(qb, kb, vb, bias, sel)


def _t5_causal_bucket(dist):
    max_exact = NUM_BUCKETS // 2
    d = jnp.maximum(dist, 1).astype(F32)
    log_b = max_exact + (jnp.log(d / max_exact) / math.log(MAX_DISTANCE / max_exact)
                         * (NUM_BUCKETS - max_exact)).astype(jnp.int32)
    return jnp.where(dist < max_exact, dist, jnp.minimum(log_b, NUM_BUCKETS - 1))


def _swa_bias_tables(rel_bias):
    w = DILATED_PATTERNS[0][0] // DILATED_PATTERNS[0][1]
    rels, steps = [], []
    for window, dil in DILATED_PATTERNS:
        n_chunks = SWA_RESIDUES // dil
        c = w // n_chunks
        pos = (np.arange(w) % c) * n_chunks + np.arange(w) // c
        rels.append(pos[:, None] + w - np.concatenate([pos, w + pos])[None, :])
        steps.append(rel_bias[_t5_causal_bucket(jnp.arange(w + 1) * dil)].astype(F32))
    rel = np.stack(rels)
    one_hot = (jnp.asarray(np.clip(rel, 0, w))[..., None] == jnp.arange(w + 1)).astype(F32)
    table = jnp.einsum("prh,pijr->phij", jnp.stack(steps), one_hot,
                       precision=lax.Precision.HIGHEST) * math.log2(math.e)
    in_band = np.logical_and(rel >= 0, rel <= w)[:, None]
    no_prev = np.arange(2 * w) >= w
    keep = np.stack([in_band, np.logical_and(in_band, no_prev)], axis=1)
    return jnp.where(jnp.asarray(keep), table[:, None], -jnp.inf)


def _mixffn_kernel(x_ref, oa_ref, ob_ref, woa_ref, wob_ref, gpost_ref, gpre_ref, gfpost_ref,
                   wg_ref, wu_ref, wd_ref, out_ref, act_scr, x1_scr, *, ff_chunk, n_sub):
    sub = x_ref.shape[0] // n_sub
    rows = [slice(i * sub, (i + 1) * sub) for i in range(n_sub)]
    d_ff = wg_ref.shape[1]
    mixes = [jnp.dot(oa_ref[r, :], woa_ref[...], preferred_element_type=F32)
             + jnp.dot(ob_ref[r, :], wob_ref[...], preferred_element_type=F32) for r in rows]
    for r, mix in zip(rows, mixes):
        x1 = x_ref[r, :] + _rms(mix, gpost_ref[...])
        x1_scr[r, :] = x1
        h = _rms(x1, gpre_ref[...]).astype(BF16)
        for c in range(d_ff // ff_chunk):
            cols = slice(c * ff_chunk, (c + 1) * ff_chunk)
            gate = jnp.dot(h, wg_ref[:, cols], preferred_element_type=F32)
            up = jnp.dot(h, wu_ref[:, cols], preferred_element_type=F32)
            act_scr[r, cols] = (_silu(gate) * up).astype(BF16)
    fs = [jnp.dot(act_scr[r, :], wd_ref[...], preferred_element_type=F32) for r in rows]
    for r, f in zip(rows, fs):
        out_ref[r, :] = x1_scr[r, :] + _rms(f, gfpost_ref[...])


def _mixffn(x2d, oa, ob, woa, wob, gpost, gpre, gfpost, wg, wu, wd, tm, sub, ff_chunk):
    T, D = x2d.shape
    d_ff = wg.shape[1]
    row = lambda n: pl.BlockSpec((tm, n), lambda i: (i, 0))
    return pl.pallas_call(
        functools.partial(_mixffn_kernel, ff_chunk=ff_chunk, n_sub=tm // sub),
        grid=(T // tm,),
        in_specs=([row(D), row(GDN_WIDTH), row(SWA_WIDTH)]
                  + [_resident(woa.shape), _resident(wob.shape)] + [_resident((1, D))] * 3
                  + [_resident(wg.shape), _resident(wu.shape), _resident(wd.shape)]),
        out_specs=row(D),
        out_shape=jax.ShapeDtypeStruct((T, D), F32),
        scratch_shapes=[pltpu.VMEM((tm, d_ff), BF16), pltpu.VMEM((tm, D), F32)],
        compiler_params=pltpu.CompilerParams(dimension_semantics=("arbitrary",),
                                             vmem_limit_bytes=VMEM_LIMIT_BYTES),
        name="mixffn",
    )(x2d, oa, ob, woa, wob, gpost, gpre, gfpost, wg, wu, wd)


def _layer(x2d, batch, seq, w_in, conv_w, a_log, dt_bias, onorm_g, rel_bias, w_out,
           g_mix_pre, g_mix_post, w_gate, w_up, w_down, g_ffn_pre, g_ffn_post):
    D = x2d.shape[1]
    gw, sw, nh = GDN_WIDTH, SWA_WIDTH, GDN_HEADS
    wb = w_in.astype(BF16)
    c_gate, c_small, c_q = 3 * gw, 4 * gw, 4 * gw + 2 * nh
    w_small = jnp.pad(wb[:, c_small:c_q], ((0, 0), (0, LANES - 2 * nh)))
    qkv, gate, qb, kb, vb, sm = _inproj(
        x2d, g_mix_pre.reshape(1, D), wb[:, :c_gate], wb[:, c_gate:c_small],
        wb[:, c_q:c_q + sw], wb[:, c_q + sw:c_q + 2 * sw], wb[:, c_q + 2 * sw:], w_small,
        batch, seq, tm=512)

    lane_pad = lambda v: jnp.pad(v.astype(F32).reshape(1, nh), ((0, 0), (nh, LANES - 2 * nh)))
    oa = _gdn(qkv, gate, sm, conv_w.astype(F32), lane_pad(a_log), lane_pad(dt_bias),
              onorm_g.astype(F32).reshape(1, GDN_HEAD_DIM), batch, seq, tb=512)

    ob = _swa(qb, kb, vb, _swa_bias_tables(rel_bias), batch, seq)

    wo = w_out.astype(BF16)
    return _mixffn(x2d, oa, ob, wo[:gw], wo[gw:], g_mix_post.reshape(1, D),
                   g_ffn_pre.reshape(1, D), g_ffn_post.reshape(1, D), w_gate.astype(BF16),
                   w_up.astype(BF16), w_down.astype(BF16), tm=1024, sub=256, ff_chunk=256)


def kernel(x, w_in, conv_w, a_log, dt_bias, onorm_g, rel_bias, w_out, g_mix_pre, g_mix_post,
           w_gate, w_up, w_down, g_ffn_pre, g_ffn_post):
    batch, seq, d_model = x.shape
    x2d = x.reshape(batch * seq, d_model)
    for l in range(w_in.shape[0]):
        x2d = _layer(x2d, batch, seq, w_in[l], conv_w[l], a_log[l], dt_bias[l], onorm_g[l],
                     rel_bias, w_out[l], g_mix_pre[l], g_mix_post[l], w_gate[l], w_up[l],
                     w_down[l], g_ffn_pre[l], g_ffn_post[l])
    return x2d.reshape(batch, seq, d_model)
```

```python
import functools
import math

import jax
import jax.numpy as jnp
import numpy as np
from jax import lax
from jax.experimental import pallas as pl
from jax.experimental.pallas import tpu as pltpu

F32 = jnp.float32
BF16 = jnp.bfloat16

GDN_HEADS = 4
GDN_HEAD_DIM = 128
GDN_WIDTH = GDN_HEADS * GDN_HEAD_DIM
CONV_WIDTH = 4
CONV_ROW_PITCH = 2
CHUNK = 64
SWA_HEADS = 8
SWA_HEAD_DIM = 64
SWA_WIDTH = SWA_HEADS * SWA_HEAD_DIM
DILATED_PATTERNS = ((128, 1), (512, 4), (2048, 16))
SWA_RESIDUES = max(dil for _, dil in DILATED_PATTERNS)
NUM_BUCKETS = 32
MAX_DISTANCE = 2048
RMS_EPS = 1e-6

LANES = 128
SUBLANES = 8
BF16_ROWS = 16
SWA_Q_SCALE = SWA_HEAD_DIM ** -0.5 * math.log2(math.e)
SWA_TASK_UNROLL = 8
VMEM_LIMIT_BYTES = 56 * 1024 * 1024

NT_DIMS = (((1,), (1,)), ((), ()))
TN_DIMS = (((0,), (0,)), ((), ()))


def _rms(x, g):
    return x * lax.rsqrt(jnp.mean(x * x, axis=-1, keepdims=True) + RMS_EPS) * g


def _silu(x):
    return x * jax.nn.sigmoid(x)


def _softplus(x):
    return jnp.maximum(x, 0.0) + jnp.log1p(jnp.exp(-jnp.abs(x)))


def _resident(shape):
    zeros = (0,) * len(shape)
    return pl.BlockSpec(shape, lambda *_: zeros, pipeline_mode=pl.Buffered(1))


def _inproj_kernel(x_ref, g_ref, wqkv_ref, wgate_ref, wq_ref, wk_ref, wv_ref, ws_ref,
                   qkv_ref, gate_ref, qb_ref, kb_ref, vb_ref, sm_ref, stage_scr, stage2_scr):
    h = _rms(x_ref[...], g_ref[...]).astype(BF16)
    for w_ref, o_ref in ((wqkv_ref, qkv_ref), (wgate_ref, gate_ref), (ws_ref, sm_ref)):
        o_ref[...] = jnp.dot(h, w_ref[...], preferred_element_type=F32)
    tm = x_ref.shape[0]
    f = math.isqrt(SWA_RESIDUES)
    assert f * f == SWA_RESIDUES
    for w_ref, o_ref, scale in ((wq_ref, qb_ref, SWA_Q_SCALE), (wk_ref, kb_ref, None),
                                (wv_ref, vb_ref, None)):
        res = jnp.dot(h, w_ref[...], preferred_element_type=F32)
        if scale is not None:
            res = res * scale
        for slab in range(SWA_WIDTH // LANES):
            stage_scr[slab] = res[:, slab * LANES:(slab + 1) * LANES]
        for slab in range(SWA_WIDTH // LANES):
            for a in range(f):
                stage2_scr[slab, a] = stage_scr[slab, pl.ds(a, tm // f, stride=f), :]
        for slab in range(SWA_WIDTH // LANES):
            for a in range(f):
                for b in range(f):
                    o_ref[slab, f * b + a] = stage2_scr[
                        slab, a, pl.ds(b, tm // SWA_RESIDUES, stride=f), :].astype(o_ref.dtype)


def _inproj(x2d, g, wqkv, wgate, wq, wk, wv, ws, batch, seq, tm):
    T, D = x2d.shape
    nt = seq // tm
    row = lambda n: pl.BlockSpec((tm, n), lambda b, j: (b * nt + j, 0))
    n_slabs = SWA_WIDTH // LANES
    slabs = pl.BlockSpec((n_slabs, None, SWA_RESIDUES, tm // SWA_RESIDUES, LANES),
                         lambda b, j: (0, b, 0, j, 0))
    slab_shape = jax.ShapeDtypeStruct(
        (n_slabs, batch, SWA_RESIDUES, seq // SWA_RESIDUES, LANES), BF16)
    return pl.pallas_call(
        _inproj_kernel,
        grid=(batch, nt),
        in_specs=[row(D), _resident((1, D))] + [_resident(w.shape) for w in (wqkv, wgate, wq, wk, wv, ws)],
        out_specs=[row(3 * GDN_WIDTH), row(GDN_WIDTH), slabs, slabs, slabs, row(LANES)],
        out_shape=[jax.ShapeDtypeStruct((T, 3 * GDN_WIDTH), F32),
                   jax.ShapeDtypeStruct((T, GDN_WIDTH), F32), slab_shape, slab_shape, slab_shape,
                   jax.ShapeDtypeStruct((T, LANES), F32)],
        scratch_shapes=[pltpu.VMEM((n_slabs, tm, LANES), F32),
                        pltpu.VMEM((n_slabs, math.isqrt(SWA_RESIDUES), tm // math.isqrt(SWA_RESIDUES),
                                    LANES), F32)],
        compiler_params=pltpu.CompilerParams(dimension_semantics=("arbitrary", "arbitrary"),
                                             vmem_limit_bytes=VMEM_LIMIT_BYTES),
        name="inproj",
    )(x2d, g, wqkv, wgate, wq, wk, wv, ws)


def _unit_lower_inverses(mats, eye, xor_ij):
    n = mats[0].shape[0]
    ds = [eye - jnp.where(xor_ij < 2, a, 0.0) for a in mats]
    s = 2
    while s < n:
        band = jnp.logical_and(xor_ij >= s, xor_ij < 2 * s)
        dbs = [d.astype(BF16) for d in ds]
        eds = [jnp.dot(jnp.where(band, a, 0.0).astype(BF16), db,
                       preferred_element_type=F32).astype(BF16) for a, db in zip(mats, dbs)]
        ds = [d - jnp.dot(db, ed, preferred_element_type=F32) for d, db, ed in zip(ds, dbs, eds)]
        s *= 2
    return ds


def _gdn_kernel(qkv_ref, gate_ref, sm_ref, convw_ref, alog_ref, dtb_ref, ong_ref, ltri_ref,
                o_ref, ext_scr, q_scr, k_scr, v_scr, g_scr, beta_scr, state_scr, *, tb, chunk):
    width = 3 * GDN_WIDTH
    hd = GDN_HEAD_DIM
    halo = SUBLANES
    n_groups = width // hd

    pitch = CONV_ROW_PITCH
    at = lambda first, n: pl.ds(pitch * first, n, stride=pitch)

    @pl.when(pl.program_id(1) == 0)
    def _():
        for j in range(n_groups):
            ext_scr[j, at(0, halo), :] = jnp.zeros((halo, hd), F32)
        state_scr[...] = jnp.zeros_like(state_scr)

    @pl.when(pl.program_id(1) > 0)
    def _():
        for j in range(n_groups):
            ext_scr[j, at(0, halo), :] = ext_scr[j, at(tb, halo), :]

    for j in range(n_groups):
        ext_scr[j, at(halo, tb), :] = qkv_ref[:, j * hd:(j + 1) * hd]

    for j in range(n_groups):
        cols = slice(j * hd, (j + 1) * hd)
        acc = None
        for i in range(CONV_WIDTH):
            term = convw_ref[i:i + 1, cols] * ext_scr[j, at(halo - (CONV_WIDTH - 1) + i, tb), :]
            acc = term if acc is None else acc + term
        y = _silu(acc)
        kind, head = divmod(j, GDN_HEADS)
        hcols = slice(head * hd, (head + 1) * hd)
        if kind < 2:
            y = y * lax.rsqrt(jnp.sum(y * y, axis=-1, keepdims=True) + 1e-6)
        if kind == 0:
            q_scr[:, hcols] = y * (hd ** -0.5)
        elif kind == 1:
            k_scr[:, hcols] = y
        else:
            v_scr[:, hcols] = y

    sm = sm_ref[...]
    beta_scr[...] = jax.nn.sigmoid(sm)
    g_scr[...] = -jnp.exp(alog_ref[...]) * _softplus(sm + dtb_ref[...])

    ii = lax.broadcasted_iota(jnp.int32, (chunk, chunk), 0)
    jj = lax.broadcasted_iota(jnp.int32, (chunk, chunk), 1)
    xor_ij = jnp.bitwise_xor(ii, jj)
    lower = ii >= jj
    strict = ii > jj
    eye = jnp.where(ii == jj, 1.0, 0.0).astype(F32)
    ltri = ltri_ref[...]
    ong = ong_ref[...]

    nc = tb // chunk
    heads = range(GDN_HEADS)
    rows = [slice(c * chunk, (c + 1) * chunk) for c in range(nc)]
    hcols = [slice(h * hd, (h + 1) * hd) for h in heads]
    probs = [(c, h) for c in range(nc) for h in heads]

    gcs = []
    for c in range(nc):
        g = g_scr[rows[c], :]
        g_hi = g.astype(BF16)
        rem = g - g_hi.astype(F32)
        g_mid = rem.astype(BF16)
        g_lo = (rem - g_mid.astype(F32)).astype(BF16)
        gcs.append(jnp.dot(ltri, g_hi, preferred_element_type=F32)
                   + jnp.dot(ltri, g_mid, preferred_element_type=F32)
                   + jnp.dot(ltri, g_lo, preferred_element_type=F32))
    gcts = [gc.T for gc in gcs]
    lane_of = lambda h: slice(GDN_HEADS + h, GDN_HEADS + h + 1)
    gcol = {(c, h): gcs[c][:, lane_of(h)] for c, h in probs}
    glast = {(c, h): gcs[c][chunk - 1:chunk, lane_of(h)] for c, h in probs}
    bcol = {(c, h): beta_scr[rows[c], h:h + 1] for c, h in probs}

    qks = {}
    for c, h in probs:
        kcb = k_scr[rows[c], hcols[h]].astype(BF16)
        qkb = jnp.concatenate([q_scr[rows[c], hcols[h]].astype(BF16), kcb], axis=0)
        qks[c, h] = lax.dot_general(qkb, kcb, NT_DIMS, preferred_element_type=F32)
    attn, a_mats = {}, []
    for c, h in probs:
        grow = gcts[c][lane_of(h), :]
        decay = jnp.exp(jnp.where(lower, gcol[c, h] - grow, -jnp.inf))
        attn[c, h] = (qks[c, h][:chunk] * decay).astype(BF16)
        a_mats.append(jnp.where(strict, qks[c, h][chunk:] * decay * bcol[c, h], 0.0))
    t_invs = _unit_lower_inverses(a_mats, eye, xor_ij)

    uws = {}
    for (c, h), t_inv in zip(probs, t_invs):
        kc = k_scr[rows[c], hcols[h]]
        eg = jnp.exp(gcol[c, h])
        rhs = jnp.concatenate([v_scr[rows[c], hcols[h]] * bcol[c, h], kc * (bcol[c, h] * eg)],
                              axis=1).astype(BF16)
        uws[c, h] = jnp.dot(t_inv.astype(BF16), rhs, preferred_element_type=F32).astype(BF16)
    kd_uw, gq, attn_u = {}, {}, {}
    for c, h in probs:
        k_dec = (k_scr[rows[c], hcols[h]] * jnp.exp(glast[c, h] - gcol[c, h])).astype(BF16)
        kd_uw[c, h] = lax.dot_general(k_dec, uws[c, h], TN_DIMS, preferred_element_type=F32)
    for c, h in probs:
        a_uw = jnp.dot(attn[c, h], uws[c, h], preferred_element_type=F32)
        q_eff = q_scr[rows[c], hcols[h]] * jnp.exp(gcol[c, h]) - a_uw[:, hd:]
        gq[c, h] = jnp.concatenate([kd_uw[c, h][:, hd:], q_eff], axis=0).astype(BF16)
        attn_u[c, h] = a_uw[:, :hd]

    state = [state_scr[h] for h in heads]
    for c in range(nc):
        gs = [jnp.dot(gq[c, h], state[h].astype(BF16), preferred_element_type=F32) for h in heads]
        state = [state[h] * jnp.exp(glast[c, h]) - gs[h][:hd] + kd_uw[c, h][:, :hd] for h in heads]
        for h in heads:
            o = gs[h][hd:] + attn_u[c, h]
            gated = _rms(o, ong) * _silu(gate_ref[rows[c], hcols[h]])
            o_ref[rows[c], hcols[h]] = gated.astype(o_ref.dtype)
    for h in heads:
        state_scr[h] = state[h]


def _gdn(qkv, gate, sm, conv_w, alog_row, dtb_row, onorm_g, batch, seq, tb):
    T = batch * seq
    nb = seq // tb
    width = 3 * GDN_WIDTH
    idx = jnp.arange(CHUNK)
    ltri = (idx[:, None] >= idx[None, :]).astype(BF16)
    row = lambda n: pl.BlockSpec((tb, n), lambda b, t: (b * nb + t, 0))
    kern = functools.partial(_gdn_kernel, tb=tb, chunk=CHUNK)
    return pl.pallas_call(
        kern,
        grid=(batch, nb),
        in_specs=[row(width), row(GDN_WIDTH), row(LANES), _resident(conv_w.shape),
                  _resident((1, LANES)), _resident((1, LANES)), _resident((1, GDN_HEAD_DIM)),
                  _resident((CHUNK, CHUNK))],
        out_specs=row(GDN_WIDTH),
        out_shape=jax.ShapeDtypeStruct((T, GDN_WIDTH), BF16),
        scratch_shapes=[pltpu.VMEM((width // GDN_HEAD_DIM, CONV_ROW_PITCH * (tb + SUBLANES),
                                    GDN_HEAD_DIM), F32),
                        pltpu.VMEM((tb, GDN_WIDTH), F32),
                        pltpu.VMEM((tb, GDN_WIDTH), F32),
                        pltpu.VMEM((tb, GDN_WIDTH), F32),
                        pltpu.VMEM((tb, LANES), F32),
                        pltpu.VMEM((tb, LANES), F32),
                        pltpu.VMEM((GDN_HEADS, GDN_HEAD_DIM, GDN_HEAD_DIM), F32)],
        compiler_params=pltpu.CompilerParams(dimension_semantics=("arbitrary", "arbitrary"),
                                             vmem_limit_bytes=VMEM_LIMIT_BYTES),
        name="gdn",
    )(qkv, gate, sm, conv_w, alog_row, dtb_row, onorm_g, ltri)


def _swa_kernel(q_ref, k_ref, v_ref, bias_ref, sel_ref, o_ref, acc_scr, m_scr, s_scr, out_scr,
                *, w):
    n_res = SWA_RESIDUES
    tile = pl.program_id(1)
    n_slabs = SWA_WIDTH // LANES
    heads_per_slab = LANES // SWA_HEAD_DIM
    assert heads_per_slab == 2
    tasks_per_call = lambda dil: 2 if (w * dil // n_res) % BF16_ROWS else 1
    lane = lax.broadcasted_iota(jnp.int32, (w, LANES), 1)
    low_half = lane < SWA_HEAD_DIM

    def expand(packed):
        hi = packed.astype(BF16)
        lo = (packed - hi.astype(F32)).astype(BF16)
        return jnp.dot(jnp.concatenate([hi, lo], axis=1), sel_ref[...], preferred_element_type=F32)

    def task(pat, dil, t, sub):
        first = pat == 0
        n_chunks = n_res // dil
        c = w // n_chunks
        per_call = tasks_per_call(dil)
        t = t * per_call + sub
        res_d = jnp.bitwise_and(t, dil - 1)
        blk = jnp.right_shift(t, dil.bit_length() - 1)
        n_glob = tile * n_chunks + blk
        q0 = pl.multiple_of(blk * c, c)
        cur0 = pl.multiple_of(n_glob * c, c)
        prev0 = pl.multiple_of(jnp.maximum(n_glob - 1, 0) * c, c)
        is_first_blk = (n_glob == 0).astype(jnp.int32)
        res_of = [a * dil + res_d for a in range(n_chunks)]

        def gather(ref, lead, start):
            return jnp.concatenate([ref[lead + (res, pl.ds(start, c), slice(None))]
                                    for res in res_of], axis=0)

        def operand(ref, slab, start, shift):
            if c % BF16_ROWS == 0:
                return gather(ref, (slab,), start)
            assert BF16_ROWS == 2 * c and dil == 1 and per_call % 2 == 0
            half = (sub + shift) % 2
            tile0 = pl.multiple_of(jnp.maximum(start - half * c, 0), BF16_ROWS)
            return jnp.concatenate(
                [ref[slab, res, pl.ds(tile0, BF16_ROWS), :].astype(F32)[half * c:(half + 1) * c]
                 for res in res_of], axis=0).astype(BF16)

        logits, vps = [], []
        for slab in range(n_slabs):
            qp = operand(q_ref, slab, q0, 0)
            kp = jnp.concatenate([operand(k_ref, slab, prev0, -1), operand(k_ref, slab, cur0, 0)],
                                 axis=0)
            vps.append(jnp.concatenate([operand(v_ref, slab, prev0, -1),
                                        operand(v_ref, slab, cur0, 0)], axis=0))
            for e in range(heads_per_slab):
                mine = low_half if e == 0 else jnp.logical_not(low_half)
                qh = jnp.where(mine, qp, jnp.zeros_like(qp))
                lg = lax.dot_general(qh, kp, NT_DIMS, preferred_element_type=F32)
                logits.append(lg + bias_ref[pat, is_first_blk, slab * heads_per_slab + e])
        m_blk = jnp.zeros((w, LANES), F32)
        s_blk = jnp.zeros((w, LANES), F32)
        ps = []
        for h, lg in enumerate(logits):
            m_h = jnp.max(lg, axis=-1, keepdims=True)
            p = jnp.exp2(lg - m_h)
            ps.append(p.astype(BF16))
            m_blk = jnp.where(lane == h, m_h, m_blk)
            s_blk = jnp.where(lane == h, jnp.sum(p, axis=-1, keepdims=True), s_blk)
        nums = []
        for slab in range(n_slabs):
            h0 = slab * heads_per_slab
            outs = [jnp.dot(ps[h0 + e], vps[slab], preferred_element_type=F32)
                    for e in range(heads_per_slab)]
            nums.append(jnp.where(low_half, outs[0], outs[1]))
        if first:
            m_new, s_new = m_blk, s_blk
        else:
            m_old = gather(m_scr, (), q0)
            m_new = jnp.maximum(m_old, m_blk)
            a_old = jnp.exp2(m_old - m_new)
            a_blk = jnp.exp2(m_blk - m_new)
            s_new = a_old * gather(s_scr, (), q0) + a_blk * s_blk
            w_old, w_blk = expand(a_old), expand(a_blk)
            for slab in range(n_slabs):
                cols = slice(slab * LANES, (slab + 1) * LANES)
                nums[slab] = (w_old[:, cols] * gather(acc_scr, (slab,), q0)
                              + w_blk[:, cols] * nums[slab])
        for idx, res in enumerate(res_of):
            rows, part = pl.ds(q0, c), slice(idx * c, (idx + 1) * c)
            for slab in range(n_slabs):
                acc_scr[slab, res, rows, :] = nums[slab][part]
            m_scr[res, rows, :] = m_new[part]
            s_scr[res, rows, :] = s_new[part]

    n_tasks = n_res
    for pat, (window, dil) in enumerate(DILATED_PATTERNS):
        assert window // dil == w and n_res % dil == 0 and (w * dil) % n_res == 0

        per_call = tasks_per_call(dil)

        def body(t, carry, pat=pat, dil=dil, per_call=per_call):
            for sub in range(per_call):
                task(pat, dil, t, sub)
            return carry
        lax.fori_loop(0, n_tasks // per_call, body, 0, unroll=SWA_TASK_UNROLL // per_call)

    def finish(r, carry):
        den = expand(s_scr[r])
        for slab in range(n_slabs):
            cols = slice(slab * LANES, (slab + 1) * LANES)
            out_scr[slab, pl.ds(r, w, stride=n_res), :] = acc_scr[slab, r] / den[:, cols]
        return carry
    lax.fori_loop(0, n_res, finish, 0, unroll=4)
    for slab in range(n_slabs):
        o_ref[:, slab * LANES:(slab + 1) * LANES] = out_scr[slab].astype(o_ref.dtype)


def _swa(qb, kb, vb, bias, batch, seq):
    n_slabs = SWA_WIDTH // LANES
    n_res = SWA_RESIDUES
    w = DILATED_PATTERNS[0][0] // DILATED_PATTERNS[0][1]
    tq = n_res * w
    nq = seq // tq
    q_blk = pl.BlockSpec((n_slabs, None, n_res, w, LANES), lambda b, i: (0, b, 0, i, 0))
    seq_blk = pl.BlockSpec((n_slabs, None, n_res, seq // n_res, LANES), lambda b, i: (0, b, 0, 0, 0))
    head_of_col = jnp.arange(SWA_WIDTH) // SWA_HEAD_DIM
    sel = (jnp.arange(LANES)[:, None] == head_of_col[None, :]).astype(BF16)
    sel = jnp.concatenate([sel, sel], axis=0)
    return pl.pallas_call(
        functools.partial(_swa_kernel, w=w),
        grid=(batch, nq),
        in_specs=[q_blk, seq_blk, seq_blk, _resident(bias.shape), _resident(sel.shape)],
        out_specs=pl.BlockSpec((tq, SWA_WIDTH), lambda b, i: (b * nq + i, 0)),
        out_shape=jax.ShapeDtypeStruct((batch * seq, SWA_WIDTH), BF16),
        scratch_shapes=[pltpu.VMEM((n_slabs, n_res, w, LANES), F32),
                        pltpu.VMEM((n_res, w, LANES), F32), pltpu.VMEM((n_res, w, LANES), F32),
                        pltpu.VMEM((n_slabs, tq, LANES), F32)],
        compiler_params=pltpu.CompilerParams(dimension_semantics=("arbitrary", "arbitrary"),
                                             vmem_limit_bytes=VMEM_LIMIT_BYTES),
        name="swa",
    )(qb, kb, vb, bias, sel)


def _t5_causal_bucket(dist):
    max_exact = NUM_BUCKETS // 2
    d = jnp.maximum(dist, 1).astype(F32)
    log_b = max_exact + (jnp.log(d / max_exact) / math.log(MAX_DISTANCE / max_exact)
                         * (NUM_BUCKETS - max_exact)).astype(jnp.int32)
    return jnp.where(dist < max_exact, dist, jnp.minimum(log_b, NUM_BUCKETS - 1))


def _swa_bias_tables(rel_bias):
    w = DILATED_PATTERNS[0][0] // DILATED_PATTERNS[0][1]
    rels, steps = [], []
    for window, dil in DILATED_PATTERNS:
        n_chunks = SWA_RESIDUES // dil
        c = w // n_chunks
        pos = (np.arange(w) % c) * n_chunks + np.arange(w) // c
        rels.append(pos[:, None] + w - np.concatenate([pos, w + pos])[None, :])
        steps.append(rel_bias[_t5_causal_bucket(jnp.arange(w + 1) * dil)].astype(F32))
    rel = np.stack(rels)
    one_hot = (jnp.asarray(np.clip(rel, 0, w))[..., None] == jnp.arange(w + 1)).astype(F32)
    table = jnp.einsum("prh,pijr->phij", jnp.stack(steps), one_hot,
                       precision=lax.Precision.HIGHEST) * math.log2(math.e)
    in_band = np.logical_and(rel >= 0, rel <= w)[:, None]
    no_prev = np.arange(2 * w) >= w
    keep = np.stack([in_band, np.logical_and(in_band, no_prev)], axis=1)
    return jnp.where(jnp.asarray(keep), table[:, None], -jnp.inf)


def _mixffn_kernel(x_ref, oa_ref, ob_ref, woa_ref, wob_ref, gpost_ref, gpre_ref, gfpost_ref,
                   wg_ref, wu_ref, wd_ref, out_ref, act_scr, x1_scr, *, ff_chunk, n_sub):
    sub = x_ref.shape[0] // n_sub
    rows = [slice(i * sub, (i + 1) * sub) for i in range(n_sub)]
    d_ff = wg_ref.shape[1]
    mixes = [jnp.dot(oa_ref[r, :], woa_ref[...], preferred_element_type=F32)
             + jnp.dot(ob_ref[r, :], wob_ref[...], preferred_element_type=F32) for r in rows]
    for r, mix in zip(rows, mixes):
        x1 = x_ref[r, :] + _rms(mix, gpost_ref[...])
        x1_scr[r, :] = x1
        h = _rms(x1, gpre_ref[...]).astype(BF16)
        for c in range(d_ff // ff_chunk):
            cols = slice(c * ff_chunk, (c + 1) * ff_chunk)
            gate = jnp.dot(h, wg_ref[:, cols], preferred_element_type=F32)
            up = jnp.dot(h, wu_ref[:, cols], preferred_element_type=F32)
            act_scr[r, cols] = (_silu(gate) * up).astype(BF16)
    fs = [jnp.dot(act_scr[r, :], wd_ref[...], preferred_element_type=F32) for r in rows]
    for r, f in zip(rows, fs):
        out_ref[r, :] = x1_scr[r, :] + _rms(f, gfpost_ref[...])


def _mixffn(x2d, oa, ob, woa, wob, gpost, gpre, gfpost, wg, wu, wd, tm, sub, ff_chunk):
    T, D = x2d.shape
    d_ff = wg.shape[1]
    row = lambda n: pl.BlockSpec((tm, n), lambda i: (i, 0))
    return pl.pallas_call(
        functools.partial(_mixffn_kernel, ff_chunk=ff_chunk, n_sub=tm // sub),
        grid=(T // tm,),
        in_specs=([row(D), row(GDN_WIDTH), row(SWA_WIDTH)]
                  + [_resident(woa.shape), _resident(wob.shape)] + [_resident((1, D))] * 3
                  + [_resident(wg.shape), _resident(wu.shape), _resident(wd.shape)]),
        out_specs=row(D),
        out_shape=jax.ShapeDtypeStruct((T, D), F32),
        scratch_shapes=[pltpu.VMEM((tm, d_ff), BF16), pltpu.VMEM((tm, D), F32)],
        compiler_params=pltpu.CompilerParams(dimension_semantics=("arbitrary",),
                                             vmem_limit_bytes=VMEM_LIMIT_BYTES),
        name="mixffn",
    )(x2d, oa, ob, woa, wob, gpost, gpre, gfpost, wg, wu, wd)


def _layer(x2d, batch, seq, w_in, conv_w, a_log, dt_bias, onorm_g, rel_bias, w_out,
           g_mix_pre, g_mix_post, w_gate, w_up, w_down, g_ffn_pre, g_ffn_post):
    D = x2d.shape[1]
    gw, sw, nh = GDN_WIDTH, SWA_WIDTH, GDN_HEADS
    wb = w_in.astype(BF16)
    c_gate, c_small, c_q = 3 * gw, 4 * gw, 4 * gw + 2 * nh
    w_small = jnp.pad(wb[:, c_small:c_q], ((0, 0), (0, LANES - 2 * nh)))
    qkv, gate, qb, kb, vb, sm = _inproj(
        x2d, g_mix_pre.reshape(1, D), wb[:, :c_gate], wb[:, c_gate:c_small],
        wb[:, c_q:c_q + sw], wb[:, c_q + sw:c_q + 2 * sw], wb[:, c_q + 2 * sw:], w_small,
        batch, seq, tm=1024)

    lane_pad = lambda v: jnp.pad(v.astype(F32).reshape(1, nh), ((0, 0), (nh, LANES - 2 * nh)))
    oa = _gdn(qkv, gate, sm, conv_w.astype(F32), lane_pad(a_log), lane_pad(dt_bias),
              onorm_g.astype(F32).reshape(1, GDN_HEAD_DIM), batch, seq, tb=512)

    ob = _swa(qb, kb, vb, _swa_bias_tables(rel_bias), batch, seq)

    wo = w_out.astype(BF16)
    return _mixffn(x2d, oa, ob, wo[:gw], wo[gw:], g_mix_post.reshape(1, D),
                   g_ffn_pre.reshape(1, D), g_ffn_post.reshape(1, D), w_gate.astype(BF16),
                   w_up.astype(BF16), w_down.astype(BF16), tm=1024, sub=256, ff_chunk=256)


def kernel(x, w_in, conv_w, a_log, dt_bias, onorm_g, rel_bias, w_out, g_mix_pre, g_mix_post,
           w_gate, w_up, w_down, g_ffn_pre, g_ffn_post):
    batch, seq, d_model = x.shape
    x2d = x.reshape(batch * seq, d_model)
    for l in range(w_in.shape[0]):
        x2d = _layer(x2d, batch, seq, w_in[l], conv_w[l], a_log[l], dt_bias[l], onorm_g[l],
                     rel_bias, w_out[l], g_mix_pre[l], g_mix_post[l], w_gate[l], w_up[l],
                     w_down[l], g_ffn_pre[l], g_ffn_post[l])
    return x2d.reshape(batch, seq, d_model)
```

```python
import functools
import math

import jax
import jax.numpy as jnp
import numpy as np
from jax import lax
from jax.experimental import pallas as pl
from jax.experimental.pallas import tpu as pltpu

F32 = jnp.float32
BF16 = jnp.bfloat16

GDN_HEADS = 4
GDN_HEAD_DIM = 128
GDN_WIDTH = GDN_HEADS * GDN_HEAD_DIM
CONV_WIDTH = 4
CONV_ROW_PITCH = 2
CHUNK = 64
SWA_HEADS = 8
SWA_HEAD_DIM = 64
SWA_WIDTH = SWA_HEADS * SWA_HEAD_DIM
DILATED_PATTERNS = ((128, 1), (512, 4), (2048, 16))
SWA_RESIDUES = max(dil for _, dil in DILATED_PATTERNS)
NUM_BUCKETS = 32
MAX_DISTANCE = 2048
RMS_EPS = 1e-6

LANES = 128
SUBLANES = 8
BF16_ROWS = 16
SWA_Q_SCALE = SWA_HEAD_DIM ** -0.5 * math.log2(math.e)
SWA_TASK_UNROLL = 8
VMEM_LIMIT_BYTES = 56 * 1024 * 1024

NT_DIMS = (((1,), (1,)), ((), ()))
TN_DIMS = (((0,), (0,)), ((), ()))


def _rms(x, g):
    return x * lax.rsqrt(jnp.mean(x * x, axis=-1, keepdims=True) + RMS_EPS) * g


def _silu(x):
    return x * jax.nn.sigmoid(x)


def _softplus(x):
    return jnp.maximum(x, 0.0) + jnp.log1p(jnp.exp(-jnp.abs(x)))


def _resident(shape):
    zeros = (0,) * len(shape)
    return pl.BlockSpec(shape, lambda *_: zeros, pipeline_mode=pl.Buffered(1))


def _inproj_kernel(x_ref, g_ref, wqkv_ref, wgate_ref, wq_ref, wk_ref, wv_ref, ws_ref,
                   convw_ref, alog_ref, dtb_ref,
                   qkv_ref, gate_ref, qb_ref, kb_ref, vb_ref, bg_ref,
                   stage_scr, stage2_scr, ext_scr):
    tm = x_ref.shape[0]
    hd = GDN_HEAD_DIM
    n_groups = 3 * GDN_WIDTH // hd
    halo = SUBLANES
    h = _rms(x_ref[...], g_ref[...]).astype(BF16)

    at = lambda first, n: pl.ds(CONV_ROW_PITCH * first, n, stride=CONV_ROW_PITCH)

    @pl.when(pl.program_id(1) == 0)
    def _():
        for j in range(n_groups):
            ext_scr[j, at(0, halo), :] = jnp.zeros((halo, hd), F32)

    @pl.when(pl.program_id(1) > 0)
    def _():
        for j in range(n_groups):
            ext_scr[j, at(0, halo), :] = ext_scr[j, at(tm, halo), :]

    pre = jnp.dot(h, wqkv_ref[...], preferred_element_type=F32)
    for j in range(n_groups):
        ext_scr[j, at(halo, tm), :] = pre[:, j * hd:(j + 1) * hd]
    gate_ref[...] = _silu(jnp.dot(h, wgate_ref[...], preferred_element_type=F32))
    sm = jnp.dot(h, ws_ref[...], preferred_element_type=F32)
    is_beta = lax.broadcasted_iota(jnp.int32, sm.shape, 1) < GDN_HEADS
    bg_ref[...] = jnp.where(is_beta, jax.nn.sigmoid(sm),
                            -jnp.exp(alog_ref[...]) * _softplus(sm + dtb_ref[...]))
    for j in range(n_groups):
        cols = slice(j * hd, (j + 1) * hd)
        acc = None
        for i in range(CONV_WIDTH):
            term = convw_ref[i:i + 1, cols] * ext_scr[j, at(halo - (CONV_WIDTH - 1) + i, tm), :]
            acc = term if acc is None else acc + term
        y = _silu(acc)
        kind = j // GDN_HEADS
        if kind < 2:
            y = y * lax.rsqrt(jnp.sum(y * y, axis=-1, keepdims=True) + 1e-6)
        if kind == 0:
            y = y * (hd ** -0.5)
        qkv_ref[:, cols] = y

    f = math.isqrt(SWA_RESIDUES)
    assert f * f == SWA_RESIDUES
    for w_ref, o_ref, scale in ((wq_ref, qb_ref, SWA_Q_SCALE), (wk_ref, kb_ref, None),
                                (wv_ref, vb_ref, None)):
        res = jnp.dot(h, w_ref[...], preferred_element_type=F32)
        if scale is not None:
            res = res * scale
        for slab in range(SWA_WIDTH // LANES):
            stage_scr[slab] = res[:, slab * LANES:(slab + 1) * LANES]
        for slab in range(SWA_WIDTH // LANES):
            for a in range(f):
                stage2_scr[slab, a] = stage_scr[slab, pl.ds(a, tm // f, stride=f), :]
        for slab in range(SWA_WIDTH // LANES):
            for a in range(f):
                for b in range(f):
                    o_ref[slab, f * b + a] = stage2_scr[
                        slab, a, pl.ds(b, tm // SWA_RESIDUES, stride=f), :].astype(o_ref.dtype)


def _inproj(x2d, g, wqkv, wgate, wq, wk, wv, ws, conv_w, alog_row, dtb_row, batch, seq, tm):
    T, D = x2d.shape
    nt = seq // tm
    row = lambda n: pl.BlockSpec((tm, n), lambda b, j: (b * nt + j, 0))
    n_slabs = SWA_WIDTH // LANES
    slabs = pl.BlockSpec((n_slabs, None, SWA_RESIDUES, tm // SWA_RESIDUES, LANES),
                         lambda b, j: (0, b, 0, j, 0))
    slab_shape = jax.ShapeDtypeStruct(
        (n_slabs, batch, SWA_RESIDUES, seq // SWA_RESIDUES, LANES), BF16)
    return pl.pallas_call(
        _inproj_kernel,
        grid=(batch, nt),
        in_specs=([row(D), _resident((1, D))]
                  + [_resident(w.shape) for w in (wqkv, wgate, wq, wk, wv, ws, conv_w)]
                  + [_resident((1, LANES))] * 2),
        out_specs=[row(3 * GDN_WIDTH), row(GDN_WIDTH), slabs, slabs, slabs, row(LANES)],
        out_shape=[jax.ShapeDtypeStruct((T, 3 * GDN_WIDTH), F32),
                   jax.ShapeDtypeStruct((T, GDN_WIDTH), F32), slab_shape, slab_shape, slab_shape,
                   jax.ShapeDtypeStruct((T, LANES), F32)],
        scratch_shapes=[pltpu.VMEM((n_slabs, tm, LANES), F32),
                        pltpu.VMEM((n_slabs, math.isqrt(SWA_RESIDUES), tm // math.isqrt(SWA_RESIDUES),
                                    LANES), F32),
                        pltpu.VMEM((3 * GDN_WIDTH // GDN_HEAD_DIM, CONV_ROW_PITCH * (tm + SUBLANES),
                                    GDN_HEAD_DIM), F32)],
        compiler_params=pltpu.CompilerParams(dimension_semantics=("arbitrary", "arbitrary"),
                                             vmem_limit_bytes=VMEM_LIMIT_BYTES),
        name="inproj",
    )(x2d, g, wqkv, wgate, wq, wk, wv, ws, conv_w, alog_row, dtb_row)


def _unit_lower_inverses(mats, eye, xor_ij):
    n = mats[0].shape[0]
    ds = [eye - jnp.where(xor_ij < 2, a, 0.0) for a in mats]
    s = 2
    while s < n:
        band = jnp.logical_and(xor_ij >= s, xor_ij < 2 * s)
        dbs = [d.astype(BF16) for d in ds]
        eds = [jnp.dot(jnp.where(band, a, 0.0).astype(BF16), db,
                       preferred_element_type=F32).astype(BF16) for a, db in zip(mats, dbs)]
        ds = [d - jnp.dot(db, ed, preferred_element_type=F32) for d, db, ed in zip(ds, dbs, eds)]
        s *= 2
    return ds


def _gdn_kernel(qkv_ref, gate_ref, bg_ref, ong_ref, ltri_ref, o_ref, state_scr, *, tb, chunk):
    hd = GDN_HEAD_DIM

    @pl.when(pl.program_id(1) == 0)
    def _():
        state_scr[...] = jnp.zeros_like(state_scr)

    ii = lax.broadcasted_iota(jnp.int32, (chunk, chunk), 0)
    jj = lax.broadcasted_iota(jnp.int32, (chunk, chunk), 1)
    xor_ij = jnp.bitwise_xor(ii, jj)
    lower = ii >= jj
    strict = ii > jj
    eye = jnp.where(ii == jj, 1.0, 0.0).astype(F32)
    ltri = ltri_ref[...]
    ong = ong_ref[...]

    nc = tb // chunk
    heads = range(GDN_HEADS)
    rows = [slice(c * chunk, (c + 1) * chunk) for c in range(nc)]
    hcols = [slice(h * hd, (h + 1) * hd) for h in heads]
    qcols, kcols, vcols = ([slice(part * GDN_WIDTH + h * hd, part * GDN_WIDTH + (h + 1) * hd)
                            for h in heads] for part in range(3))
    probs = [(c, h) for c in range(nc) for h in heads]

    gcs = []
    for c in range(nc):
        g = bg_ref[rows[c], :]
        g_hi = g.astype(BF16)
        rem = g - g_hi.astype(F32)
        g_mid = rem.astype(BF16)
        g_lo = (rem - g_mid.astype(F32)).astype(BF16)
        gcs.append(jnp.dot(ltri, g_hi, preferred_element_type=F32)
                   + jnp.dot(ltri, g_mid, preferred_element_type=F32)
                   + jnp.dot(ltri, g_lo, preferred_element_type=F32))
    gcts = [gc.T for gc in gcs]
    lane_of = lambda h: slice(GDN_HEADS + h, GDN_HEADS + h + 1)
    gcol = {(c, h): gcs[c][:, lane_of(h)] for c, h in probs}
    glast = {(c, h): gcs[c][chunk - 1:chunk, lane_of(h)] for c, h in probs}
    bcol = {(c, h): bg_ref[rows[c], h:h + 1] for c, h in probs}

    qks = {}
    for c, h in probs:
        kcb = qkv_ref[rows[c], kcols[h]].astype(BF16)
        qkb = jnp.concatenate([qkv_ref[rows[c], qcols[h]].astype(BF16), kcb], axis=0)
        qks[c, h] = lax.dot_general(qkb, kcb, NT_DIMS, preferred_element_type=F32)
    attn, a_mats = {}, []
    for c, h in probs:
        grow = gcts[c][lane_of(h), :]
        decay = jnp.exp(jnp.where(lower, gcol[c, h] - grow, -jnp.inf))
        attn[c, h] = (qks[c, h][:chunk] * decay).astype(BF16)
        a_mats.append(jnp.where(strict, qks[c, h][chunk:] * decay * bcol[c, h], 0.0))
    t_invs = _unit_lower_inverses(a_mats, eye, xor_ij)

    uws = {}
    for (c, h), t_inv in zip(probs, t_invs):
        kc = qkv_ref[rows[c], kcols[h]]
        eg = jnp.exp(gcol[c, h])
        rhs = jnp.concatenate([qkv_ref[rows[c], vcols[h]] * bcol[c, h], kc * (bcol[c, h] * eg)],
                              axis=1).astype(BF16)
        uws[c, h] = jnp.dot(t_inv.astype(BF16), rhs, preferred_element_type=F32).astype(BF16)
    kd_uw, gq, attn_u = {}, {}, {}
    for c, h in probs:
        k_dec = (qkv_ref[rows[c], kcols[h]] * jnp.exp(glast[c, h] - gcol[c, h])).astype(BF16)
        kd_uw[c, h] = lax.dot_general(k_dec, uws[c, h], TN_DIMS, preferred_element_type=F32)
    for c, h in probs:
        a_uw = jnp.dot(attn[c, h], uws[c, h], preferred_element_type=F32)
        q_eff = qkv_ref[rows[c], qcols[h]] * jnp.exp(gcol[c, h]) - a_uw[:, hd:]
        gq[c, h] = jnp.concatenate([kd_uw[c, h][:, hd:], q_eff], axis=0).astype(BF16)
        attn_u[c, h] = a_uw[:, :hd]

    state = [state_scr[h] for h in heads]
    for c in range(nc):
        gs = [jnp.dot(gq[c, h], state[h].astype(BF16), preferred_element_type=F32) for h in heads]
        state = [state[h] * jnp.exp(glast[c, h]) - gs[h][:hd] + kd_uw[c, h][:, :hd] for h in heads]
        for h in heads:
            o = gs[h][hd:] + attn_u[c, h]
            gated = _rms(o, ong) * gate_ref[rows[c], hcols[h]]
            o_ref[rows[c], hcols[h]] = gated.astype(o_ref.dtype)
    for h in heads:
        state_scr[h] = state[h]


def _gdn(qkv, gate, bg, onorm_g, batch, seq, tb):
    T = batch * seq
    nb = seq // tb
    idx = jnp.arange(CHUNK)
    ltri = (idx[:, None] >= idx[None, :]).astype(BF16)
    row = lambda n: pl.BlockSpec((tb, n), lambda b, t: (b * nb + t, 0))
    kern = functools.partial(_gdn_kernel, tb=tb, chunk=CHUNK)
    return pl.pallas_call(
        kern,
        grid=(batch, nb),
        in_specs=[row(3 * GDN_WIDTH), row(GDN_WIDTH), row(LANES), _resident((1, GDN_HEAD_DIM)),
                  _resident((CHUNK, CHUNK))],
        out_specs=row(GDN_WIDTH),
        out_shape=jax.ShapeDtypeStruct((T, GDN_WIDTH), BF16),
        scratch_shapes=[pltpu.VMEM((GDN_HEADS, GDN_HEAD_DIM, GDN_HEAD_DIM), F32)],
        compiler_params=pltpu.CompilerParams(dimension_semantics=("arbitrary", "arbitrary"),
                                             vmem_limit_bytes=VMEM_LIMIT_BYTES),
        name="gdn",
    )(qkv, gate, bg, onorm_g, ltri)


def _swa_kernel(q_ref, k_ref, v_ref, bias_ref, sel_ref, o_ref, acc_scr, m_scr, s_scr, out_scr,
                *, w):
    n_res = SWA_RESIDUES
    tile = pl.program_id(1)
    n_slabs = SWA_WIDTH // LANES
    heads_per_slab = LANES // SWA_HEAD_DIM
    assert heads_per_slab == 2
    tasks_per_call = lambda dil: 2 if (w * dil // n_res) % BF16_ROWS else 1
    lane = lax.broadcasted_iota(jnp.int32, (w, LANES), 1)
    low_half = lane < SWA_HEAD_DIM

    def expand(packed):
        hi = packed.astype(BF16)
        lo = (packed - hi.astype(F32)).astype(BF16)
        return jnp.dot(jnp.concatenate([hi, lo], axis=1), sel_ref[...], preferred_element_type=F32)

    def task(pat, dil, t, sub):
        first = pat == 0
        n_chunks = n_res // dil
        c = w // n_chunks
        per_call = tasks_per_call(dil)
        t = t * per_call + sub
        res_d = jnp.bitwise_and(t, dil - 1)
        blk = jnp.right_shift(t, dil.bit_length() - 1)
        n_glob = tile * n_chunks + blk
        q0 = pl.multiple_of(blk * c, c)
        cur0 = pl.multiple_of(n_glob * c, c)
        prev0 = pl.multiple_of(jnp.maximum(n_glob - 1, 0) * c, c)
        is_first_blk = (n_glob == 0).astype(jnp.int32)
        res_of = [a * dil + res_d for a in range(n_chunks)]

        def gather(ref, lead, start):
            return jnp.concatenate([ref[lead + (res, pl.ds(start, c), slice(None))]
                                    for res in res_of], axis=0)

        def operand(ref, slab, start, shift):
            if c % BF16_ROWS == 0:
                return gather(ref, (slab,), start)
            assert BF16_ROWS == 2 * c and dil == 1 and per_call % 2 == 0
            half = (sub + shift) % 2
            tile0 = pl.multiple_of(jnp.maximum(start - half * c, 0), BF16_ROWS)
            return jnp.concatenate(
                [ref[slab, res, pl.ds(tile0, BF16_ROWS), :].astype(F32)[half * c:(half + 1) * c]
                 for res in res_of], axis=0).astype(BF16)

        logits, vps = [], []
        for slab in range(n_slabs):
            qp = operand(q_ref, slab, q0, 0)
            kp = jnp.concatenate([operand(k_ref, slab, prev0, -1), operand(k_ref, slab, cur0, 0)],
                                 axis=0)
            vps.append(jnp.concatenate([operand(v_ref, slab, prev0, -1),
                                        operand(v_ref, slab, cur0, 0)], axis=0))
            for e in range(heads_per_slab):
                mine = low_half if e == 0 else jnp.logical_not(low_half)
                qh = jnp.where(mine, qp, jnp.zeros_like(qp))
                lg = lax.dot_general(qh, kp, NT_DIMS, preferred_element_type=F32)
                logits.append(lg + bias_ref[pat, is_first_blk, slab * heads_per_slab + e])
        m_blk = jnp.zeros((w, LANES), F32)
        s_blk = jnp.zeros((w, LANES), F32)
        ps = []
        for h, lg in enumerate(logits):
            m_h = jnp.max(lg, axis=-1, keepdims=True)
            p = jnp.exp2(lg - m_h)
            ps.append(p.astype(BF16))
            m_blk = jnp.where(lane == h, m_h, m_blk)
            s_blk = jnp.where(lane == h, jnp.sum(p, axis=-1, keepdims=True), s_blk)
        nums = []
        for slab in range(n_slabs):
            h0 = slab * heads_per_slab
            outs = [jnp.dot(ps[h0 + e], vps[slab], preferred_element_type=F32)
                    for e in range(heads_per_slab)]
            nums.append(jnp.where(low_half, outs[0], outs[1]))
        if first:
            m_new, s_new = m_blk, s_blk
        else:
            m_old = gather(m_scr, (), q0)
            m_new = jnp.maximum(m_old, m_blk)
            a_old = jnp.exp2(m_old - m_new)
            a_blk = jnp.exp2(m_blk - m_new)
            s_new = a_old * gather(s_scr, (), q0) + a_blk * s_blk
            w_old, w_blk = expand(a_old), expand(a_blk)
            for slab in range(n_slabs):
                cols = slice(slab * LANES, (slab + 1) * LANES)
                nums[slab] = (w_old[:, cols] * gather(acc_scr, (slab,), q0)
                              + w_blk[:, cols] * nums[slab])
        for idx, res in enumerate(res_of):
            rows, part = pl.ds(q0, c), slice(idx * c, (idx + 1) * c)
            for slab in range(n_slabs):
                acc_scr[slab, res, rows, :] = nums[slab][part]
            m_scr[res, rows, :] = m_new[part]
            s_scr[res, rows, :] = s_new[part]

    n_tasks = n_res
    for pat, (window, dil) in enumerate(DILATED_PATTERNS):
        assert window // dil == w and n_res % dil == 0 and (w * dil) % n_res == 0

        per_call = tasks_per_call(dil)

        def body(t, carry, pat=pat, dil=dil, per_call=per_call):
            for sub in range(per_call):
                task(pat, dil, t, sub)
            return carry
        lax.fori_loop(0, n_tasks // per_call, body, 0, unroll=SWA_TASK_UNROLL // per_call)

    def finish(r, carry):
        den = expand(s_scr[r])
        for slab in range(n_slabs):
            cols = slice(slab * LANES, (slab + 1) * LANES)
            out_scr[slab, pl.ds(r, w, stride=n_res), :] = acc_scr[slab, r] / den[:, cols]
        return carry
    lax.fori_loop(0, n_res, finish, 0, unroll=4)
    for slab in range(n_slabs):
        o_ref[:, slab * LANES:(slab + 1) * LANES] = out_scr[slab].astype(o_ref.dtype)


def _swa(qb, kb, vb, bias, batch, seq):
    n_slabs = SWA_WIDTH // LANES
    n_res = SWA_RESIDUES
    w = DILATED_PATTERNS[0][0] // DILATED_PATTERNS[0][1]
    tq = n_res * w
    nq = seq // tq
    q_blk = pl.BlockSpec((n_slabs, None, n_res, w, LANES), lambda b, i: (0, b, 0, i, 0))
    seq_blk = pl.BlockSpec((n_slabs, None, n_res, seq // n_res, LANES), lambda b, i: (0, b, 0, 0, 0))
    head_of_col = jnp.arange(SWA_WIDTH) // SWA_HEAD_DIM
    sel = (jnp.arange(LANES)[:, None] == head_of_col[None, :]).astype(BF16)
    sel = jnp.concatenate([sel, sel], axis=0)
    return pl.pallas_call(
        functools.partial(_swa_kernel, w=w),
        grid=(batch, nq),
        in_specs=[q_blk, seq_blk, seq_blk, _resident(bias.shape), _resident(sel.shape)],
        out_specs=pl.BlockSpec((tq, SWA_WIDTH), lambda b, i: (b * nq + i, 0)),
        out_shape=jax.ShapeDtypeStruct((batch * seq, SWA_WIDTH), BF16),
        scratch_shapes=[pltpu.VMEM((n_slabs, n_res, w, LANES), F32),
                        pltpu.VMEM((n_res, w, LANES), F32), pltpu.VMEM((n_res, w, LANES), F32),
                        pltpu.VMEM((n_slabs, tq, LANES), F32)],
        compiler_params=pltpu.CompilerParams(dimension_semantics=("arbitrary", "arbitrary"),
                                             vmem_limit_bytes=VMEM_LIMIT_BYTES),
        name="swa",
    )(qb, kb, vb, bias, sel)


def _t5_causal_bucket(dist):
    max_exact = NUM_BUCKETS // 2
    d = jnp.maximum(dist, 1).astype(F32)
    log_b = max_exact + (jnp.log(d / max_exact) / math.log(MAX_DISTANCE / max_exact)
                         * (NUM_BUCKETS - max_exact)).astype(jnp.int32)
    return jnp.where(dist < max_exact, dist, jnp.minimum(log_b, NUM_BUCKETS - 1))


def _swa_bias_tables(rel_bias):
    w = DILATED_PATTERNS[0][0] // DILATED_PATTERNS[0][1]
    rels, steps = [], []
    for window, dil in DILATED_PATTERNS:
        n_chunks = SWA_RESIDUES // dil
        c = w // n_chunks
        pos = (np.arange(w) % c) * n_chunks + np.arange(w) // c
        rels.append(pos[:, None] + w - np.concatenate([pos, w + pos])[None, :])
        steps.append(rel_bias[_t5_causal_bucket(jnp.arange(w + 1) * dil)].astype(F32))
    rel = np.stack(rels)
    one_hot = (jnp.asarray(np.clip(rel, 0, w))[..., None] == jnp.arange(w + 1)).astype(F32)
    table = jnp.einsum("prh,pijr->phij", jnp.stack(steps), one_hot,
                       precision=lax.Precision.HIGHEST) * math.log2(math.e)
    in_band = np.logical_and(rel >= 0, rel <= w)[:, None]
    no_prev = np.arange(2 * w) >= w
    keep = np.stack([in_band, np.logical_and(in_band, no_prev)], axis=1)
    return jnp.where(jnp.asarray(keep), table[:, None], -jnp.inf)


def _mixffn_kernel(x_ref, oa_ref, ob_ref, woa_ref, wob_ref, gpost_ref, gpre_ref, gfpost_ref,
                   wg_ref, wu_ref, wd_ref, out_ref, act_scr, x1_scr, *, ff_chunk, n_sub):
    sub = x_ref.shape[0] // n_sub
    rows = [slice(i * sub, (i + 1) * sub) for i in range(n_sub)]
    d_ff = wg_ref.shape[1]
    mixes = [jnp.dot(oa_ref[r, :], woa_ref[...], preferred_element_type=F32)
             + jnp.dot(ob_ref[r, :], wob_ref[...], preferred_element_type=F32) for r in rows]
    for r, mix in zip(rows, mixes):
        x1 = x_ref[r, :] + _rms(mix, gpost_ref[...])
        x1_scr[r, :] = x1
        h = _rms(x1, gpre_ref[...]).astype(BF16)
        for c in range(d_ff // ff_chunk):
            cols = slice(c * ff_chunk, (c + 1) * ff_chunk)
            gate = jnp.dot(h, wg_ref[:, cols], preferred_element_type=F32)
            up = jnp.dot(h, wu_ref[:, cols], preferred_element_type=F32)
            act_scr[r, cols] = (_silu(gate) * up).astype(BF16)
    fs = [jnp.dot(act_scr[r, :], wd_ref[...], preferred_element_type=F32) for r in rows]
    for r, f in zip(rows, fs):
        out_ref[r, :] = x1_scr[r, :] + _rms(f, gfpost_ref[...])


def _mixffn(x2d, oa, ob, woa, wob, gpost, gpre, gfpost, wg, wu, wd, tm, sub, ff_chunk):
    T, D = x2d.shape
    d_ff = wg.shape[1]
    row = lambda n: pl.BlockSpec((tm, n), lambda i: (i, 0))
    return pl.pallas_call(
        functools.partial(_mixffn_kernel, ff_chunk=ff_chunk, n_sub=tm // sub),
        grid=(T // tm,),
        in_specs=([row(D), row(GDN_WIDTH), row(SWA_WIDTH)]
                  + [_resident(woa.shape), _resident(wob.shape)] + [_resident((1, D))] * 3
                  + [_resident(wg.shape), _resident(wu.shape), _resident(wd.shape)]),
        out_specs=row(D),
        out_shape=jax.ShapeDtypeStruct((T, D), F32),
        scratch_shapes=[pltpu.VMEM((tm, d_ff), BF16), pltpu.VMEM((tm, D), F32)],
        compiler_params=pltpu.CompilerParams(dimension_semantics=("arbitrary",),
                                             vmem_limit_bytes=VMEM_LIMIT_BYTES),
        name="mixffn",
    )(x2d, oa, ob, woa, wob, gpost, gpre, gfpost, wg, wu, wd)


def _layer(x2d, batch, seq, w_in, conv_w, a_log, dt_bias, onorm_g, rel_bias, w_out,
           g_mix_pre, g_mix_post, w_gate, w_up, w_down, g_ffn_pre, g_ffn_post):
    D = x2d.shape[1]
    gw, sw, nh = GDN_WIDTH, SWA_WIDTH, GDN_HEADS
    wb = w_in.astype(BF16)
    c_gate, c_small, c_q = 3 * gw, 4 * gw, 4 * gw + 2 * nh
    w_small = jnp.pad(wb[:, c_small:c_q], ((0, 0), (0, LANES - 2 * nh)))
    lane_pad = lambda v: jnp.pad(v.astype(F32).reshape(1, nh), ((0, 0), (nh, LANES - 2 * nh)))
    qkv, gate, qb, kb, vb, bg = _inproj(
        x2d, g_mix_pre.reshape(1, D), wb[:, :c_gate], wb[:, c_gate:c_small],
        wb[:, c_q:c_q + sw], wb[:, c_q + sw:c_q + 2 * sw], wb[:, c_q + 2 * sw:], w_small,
        conv_w.astype(F32), lane_pad(a_log), lane_pad(dt_bias), batch, seq, tm=512)

    oa = _gdn(qkv, gate, bg, onorm_g.astype(F32).reshape(1, GDN_HEAD_DIM), batch, seq, tb=512)

    ob = _swa(qb, kb, vb, _swa_bias_tables(rel_bias), batch, seq)

    wo = w_out.astype(BF16)
    return _mixffn(x2d, oa, ob, wo[:gw], wo[gw:], g_mix_post.reshape(1, D),
                   g_ffn_pre.reshape(1, D), g_ffn_post.reshape(1, D), w_gate.astype(BF16),
                   w_up.astype(BF16), w_down.astype(BF16), tm=1024, sub=256, ff_chunk=256)


def kernel(x, w_in, conv_w, a_log, dt_bias, onorm_g, rel_bias, w_out, g_mix_pre, g_mix_post,
           w_gate, w_up, w_down, g_ffn_pre, g_ffn_post):
    batch, seq, d_model = x.shape
    x2d = x.reshape(batch * seq, d_model)
    for l in range(w_in.shape[0]):
        x2d = _layer(x2d, batch, seq, w_in[l], conv_w[l], a_log[l], dt_bias[l], onorm_g[l],
                     rel_bias, w_out[l], g_mix_pre[l], g_mix_post[l], w_gate[l], w_up[l],
                     w_down[l], g_ffn_pre[l], g_ffn_post[l])
    return x2d.reshape(batch, seq, d_model)
```

```python
import functools
import math

import jax
import jax.numpy as jnp
import numpy as np
from jax import lax
from jax.experimental import pallas as pl
from jax.experimental.pallas import tpu as pltpu

F32 = jnp.float32
BF16 = jnp.bfloat16

GDN_HEADS = 4
GDN_HEAD_DIM = 128
GDN_WIDTH = GDN_HEADS * GDN_HEAD_DIM
CONV_WIDTH = 4
CONV_ROW_PITCH = 2
CHUNK = 64
SWA_HEADS = 8
SWA_HEAD_DIM = 64
SWA_WIDTH = SWA_HEADS * SWA_HEAD_DIM
DILATED_PATTERNS = ((128, 1), (512, 4), (2048, 16))
SWA_RESIDUES = max(dil for _, dil in DILATED_PATTERNS)
NUM_BUCKETS = 32
MAX_DISTANCE = 2048
RMS_EPS = 1e-6

LANES = 128
SUBLANES = 8
BF16_ROWS = 16
SWA_Q_SCALE = SWA_HEAD_DIM ** -0.5 * math.log2(math.e)
SWA_TASK_UNROLL = 8
VMEM_LIMIT_BYTES = 56 * 1024 * 1024

NT_DIMS = (((1,), (1,)), ((), ()))
TN_DIMS = (((0,), (0,)), ((), ()))


def _rms(x, g):
    return x * lax.rsqrt(jnp.mean(x * x, axis=-1, keepdims=True) + RMS_EPS) * g


def _silu(x):
    return x * jax.nn.sigmoid(x)


def _softplus(x):
    return jnp.maximum(x, 0.0) + jnp.log1p(jnp.exp(-jnp.abs(x)))


def _resident(shape):
    zeros = (0,) * len(shape)
    return pl.BlockSpec(shape, lambda *_: zeros, pipeline_mode=pl.Buffered(1))


def _inproj_kernel(x_ref, g_ref, wqkv_ref, wgate_ref, wq_ref, wk_ref, wv_ref, ws_ref,
                   convw_ref, alog_ref, dtb_ref,
                   qkv_ref, gate_ref, qb_ref, kb_ref, vb_ref, bg_ref,
                   stage_scr, stage2_scr, ext_scr, *, n_sub):
    tm = x_ref.shape[0]
    hd = GDN_HEAD_DIM
    n_groups = 3 * GDN_WIDTH // hd
    halo = SUBLANES
    sub = tm // n_sub
    rows = [slice(i * sub, (i + 1) * sub) for i in range(n_sub)]
    hs = [_rms(x_ref[r, :], g_ref[...]).astype(BF16) for r in rows]

    at = lambda first, n: pl.ds(CONV_ROW_PITCH * first, n, stride=CONV_ROW_PITCH)

    @pl.when(pl.program_id(1) == 0)
    def _():
        for j in range(n_groups):
            ext_scr[j, at(0, halo), :] = jnp.zeros((halo, hd), F32)

    @pl.when(pl.program_id(1) > 0)
    def _():
        for j in range(n_groups):
            ext_scr[j, at(0, halo), :] = ext_scr[j, at(tm, halo), :]

    pres = [jnp.dot(h, wqkv_ref[...], preferred_element_type=F32) for h in hs]
    for i, pre in enumerate(pres):
        for j in range(n_groups):
            ext_scr[j, at(halo + i * sub, sub), :] = pre[:, j * hd:(j + 1) * hd]

    def conv_act(i):
        for j in range(n_groups):
            cols = slice(j * hd, (j + 1) * hd)
            acc = None
            for tap in range(CONV_WIDTH):
                first = halo + i * sub - (CONV_WIDTH - 1) + tap
                term = convw_ref[tap:tap + 1, cols] * ext_scr[j, at(first, sub), :]
                acc = term if acc is None else acc + term
            y = _silu(acc)
            kind = j // GDN_HEADS
            if kind < 2:
                y = y * lax.rsqrt(jnp.sum(y * y, axis=-1, keepdims=True) + 1e-6)
            if kind == 0:
                y = y * (hd ** -0.5)
            qkv_ref[rows[i], cols] = y

    n_res = SWA_RESIDUES
    f = math.isqrt(n_res)
    assert f * f == n_res
    attn_proj = ((wq_ref, qb_ref, SWA_Q_SCALE), (wk_ref, kb_ref, None), (wv_ref, vb_ref, None))

    def split(i, t, o_ref):
        r0 = i * sub
        for slab in range(SWA_WIDTH // LANES):
            for a in range(f):
                stage2_scr[t, slab, a, r0 // f:(r0 + sub) // f, :] = (
                    stage_scr[t, slab, pl.ds(r0 + a, sub // f, stride=f), :])
        for slab in range(SWA_WIDTH // LANES):
            for a in range(f):
                for b in range(f):
                    o_ref[slab, f * b + a, r0 // n_res:(r0 + sub) // n_res, :] = stage2_scr[
                        t, slab, a, pl.ds(r0 // f + b, sub // n_res, stride=f), :].astype(o_ref.dtype)

    for i, h in enumerate(hs):
        gate_ref[rows[i], :] = _silu(jnp.dot(h, wgate_ref[...], preferred_element_type=F32))
        sm = jnp.dot(h, ws_ref[...], preferred_element_type=F32)
        is_beta = lax.broadcasted_iota(jnp.int32, sm.shape, 1) < GDN_HEADS
        bg_ref[rows[i], :] = jnp.where(is_beta, jax.nn.sigmoid(sm),
                                       -jnp.exp(alog_ref[...]) * _softplus(sm + dtb_ref[...]))
        for t, (w_ref, _, scale) in enumerate(attn_proj):
            res = jnp.dot(h, w_ref[...], preferred_element_type=F32)
            if scale is not None:
                res = res * scale
            for slab in range(SWA_WIDTH // LANES):
                stage_scr[t, slab, rows[i], :] = res[:, slab * LANES:(slab + 1) * LANES]
        conv_act(i)
        for t, (_, o_ref, _) in enumerate(attn_proj):
            split(i, t, o_ref)


def _inproj(x2d, g, wqkv, wgate, wq, wk, wv, ws, conv_w, alog_row, dtb_row, batch, seq, tm):
    T, D = x2d.shape
    nt = seq // tm
    row = lambda n: pl.BlockSpec((tm, n), lambda b, j: (b * nt + j, 0))
    n_slabs = SWA_WIDTH // LANES
    slabs = pl.BlockSpec((n_slabs, None, SWA_RESIDUES, tm // SWA_RESIDUES, LANES),
                         lambda b, j: (0, b, 0, j, 0))
    slab_shape = jax.ShapeDtypeStruct(
        (n_slabs, batch, SWA_RESIDUES, seq // SWA_RESIDUES, LANES), BF16)
    return pl.pallas_call(
        functools.partial(_inproj_kernel, n_sub=2),
        grid=(batch, nt),
        in_specs=([row(D), _resident((1, D))]
                  + [_resident(w.shape) for w in (wqkv, wgate, wq, wk, wv, ws, conv_w)]
                  + [_resident((1, LANES))] * 2),
        out_specs=[row(3 * GDN_WIDTH), row(GDN_WIDTH), slabs, slabs, slabs, row(LANES)],
        out_shape=[jax.ShapeDtypeStruct((T, 3 * GDN_WIDTH), F32),
                   jax.ShapeDtypeStruct((T, GDN_WIDTH), F32), slab_shape, slab_shape, slab_shape,
                   jax.ShapeDtypeStruct((T, LANES), F32)],
        scratch_shapes=[pltpu.VMEM((3, n_slabs, tm, LANES), F32),
                        pltpu.VMEM((3, n_slabs, math.isqrt(SWA_RESIDUES),
                                    tm // math.isqrt(SWA_RESIDUES), LANES), F32),
                        pltpu.VMEM((3 * GDN_WIDTH // GDN_HEAD_DIM, CONV_ROW_PITCH * (tm + SUBLANES),
                                    GDN_HEAD_DIM), F32)],
        compiler_params=pltpu.CompilerParams(dimension_semantics=("arbitrary", "arbitrary"),
                                             vmem_limit_bytes=VMEM_LIMIT_BYTES),
        name="inproj",
    )(x2d, g, wqkv, wgate, wq, wk, wv, ws, conv_w, alog_row, dtb_row)


def _unit_lower_inverses(mats, eye, xor_ij):
    n = mats[0].shape[0]
    ds = [eye - jnp.where(xor_ij < 2, a, 0.0) for a in mats]
    s = 2
    while s < n:
        band = jnp.logical_and(xor_ij >= s, xor_ij < 2 * s)
        dbs = [d.astype(BF16) for d in ds]
        eds = [jnp.dot(jnp.where(band, a, 0.0).astype(BF16), db,
                       preferred_element_type=F32).astype(BF16) for a, db in zip(mats, dbs)]
        ds = [d - jnp.dot(db, ed, preferred_element_type=F32) for d, db, ed in zip(ds, dbs, eds)]
        s *= 2
    return ds


def _gdn_kernel(qkv_ref, gate_ref, bg_ref, ong_ref, ltri_ref, o_ref, state_scr, *, tb, chunk):
    hd = GDN_HEAD_DIM

    @pl.when(pl.program_id(1) == 0)
    def _():
        state_scr[...] = jnp.zeros_like(state_scr)

    ii = lax.broadcasted_iota(jnp.int32, (chunk, chunk), 0)
    jj = lax.broadcasted_iota(jnp.int32, (chunk, chunk), 1)
    xor_ij = jnp.bitwise_xor(ii, jj)
    lower = ii >= jj
    strict = ii > jj
    eye = jnp.where(ii == jj, 1.0, 0.0).astype(F32)
    ltri = ltri_ref[...]
    ong = ong_ref[...]

    nc = tb // chunk
    heads = range(GDN_HEADS)
    rows = [slice(c * chunk, (c + 1) * chunk) for c in range(nc)]
    hcols = [slice(h * hd, (h + 1) * hd) for h in heads]
    qcols, kcols, vcols = ([slice(part * GDN_WIDTH + h * hd, part * GDN_WIDTH + (h + 1) * hd)
                            for h in heads] for part in range(3))
    probs = [(c, h) for c in range(nc) for h in heads]

    gcs = []
    for c in range(nc):
        g = bg_ref[rows[c], :]
        g_hi = g.astype(BF16)
        rem = g - g_hi.astype(F32)
        g_mid = rem.astype(BF16)
        g_lo = (rem - g_mid.astype(F32)).astype(BF16)
        gcs.append(jnp.dot(ltri, g_hi, preferred_element_type=F32)
                   + jnp.dot(ltri, g_mid, preferred_element_type=F32)
                   + jnp.dot(ltri, g_lo, preferred_element_type=F32))
    gcts = [gc.T for gc in gcs]
    lane_of = lambda h: slice(GDN_HEADS + h, GDN_HEADS + h + 1)
    gcol = {(c, h): gcs[c][:, lane_of(h)] for c, h in probs}
    glast = {(c, h): gcs[c][chunk - 1:chunk, lane_of(h)] for c, h in probs}
    bcol = {(c, h): bg_ref[rows[c], h:h + 1] for c, h in probs}

    qks = {}
    for c, h in probs:
        kcb = qkv_ref[rows[c], kcols[h]].astype(BF16)
        qkb = jnp.concatenate([qkv_ref[rows[c], qcols[h]].astype(BF16), kcb], axis=0)
        qks[c, h] = lax.dot_general(qkb, kcb, NT_DIMS, preferred_element_type=F32)
    attn, a_mats = {}, []
    for c, h in probs:
        grow = gcts[c][lane_of(h), :]
        decay = jnp.exp(jnp.where(lower, gcol[c, h] - grow, -jnp.inf))
        attn[c, h] = (qks[c, h][:chunk] * decay).astype(BF16)
        a_mats.append(jnp.where(strict, qks[c, h][chunk:] * decay * bcol[c, h], 0.0))
    t_invs = _unit_lower_inverses(a_mats, eye, xor_ij)

    uws = {}
    for (c, h), t_inv in zip(probs, t_invs):
        kc = qkv_ref[rows[c], kcols[h]]
        eg = jnp.exp(gcol[c, h])
        rhs = jnp.concatenate([qkv_ref[rows[c], vcols[h]] * bcol[c, h], kc * (bcol[c, h] * eg)],
                              axis=1).astype(BF16)
        uws[c, h] = jnp.dot(t_inv.astype(BF16), rhs, preferred_element_type=F32).astype(BF16)
    kd_uw, gq, attn_u = {}, {}, {}
    for c, h in probs:
        k_dec = (qkv_ref[rows[c], kcols[h]] * jnp.exp(glast[c, h] - gcol[c, h])).astype(BF16)
        kd_uw[c, h] = lax.dot_general(k_dec, uws[c, h], TN_DIMS, preferred_element_type=F32)
    for c, h in probs:
        a_uw = jnp.dot(attn[c, h], uws[c, h], preferred_element_type=F32)
        q_eff = qkv_ref[rows[c], qcols[h]] * jnp.exp(gcol[c, h]) - a_uw[:, hd:]
        gq[c, h] = jnp.concatenate([kd_uw[c, h][:, hd:], q_eff], axis=0).astype(BF16)
        attn_u[c, h] = a_uw[:, :hd]

    state = [state_scr[h] for h in heads]
    for c in range(nc):
        gs = [jnp.dot(gq[c, h], state[h].astype(BF16), preferred_element_type=F32) for h in heads]
        state = [state[h] * jnp.exp(glast[c, h]) - gs[h][:hd] + kd_uw[c, h][:, :hd] for h in heads]
        for h in heads:
            o = gs[h][hd:] + attn_u[c, h]
            gated = _rms(o, ong) * gate_ref[rows[c], hcols[h]]
            o_ref[rows[c], hcols[h]] = gated.astype(o_ref.dtype)
    for h in heads:
        state_scr[h] = state[h]


def _gdn(qkv, gate, bg, onorm_g, batch, seq, tb):
    T = batch * seq
    nb = seq // tb
    idx = jnp.arange(CHUNK)
    ltri = (idx[:, None] >= idx[None, :]).astype(BF16)
    row = lambda n: pl.BlockSpec((tb, n), lambda b, t: (b * nb + t, 0))
    kern = functools.partial(_gdn_kernel, tb=tb, chunk=CHUNK)
    return pl.pallas_call(
        kern,
        grid=(batch, nb),
        in_specs=[row(3 * GDN_WIDTH), row(GDN_WIDTH), row(LANES), _resident((1, GDN_HEAD_DIM)),
                  _resident((CHUNK, CHUNK))],
        out_specs=row(GDN_WIDTH),
        out_shape=jax.ShapeDtypeStruct((T, GDN_WIDTH), BF16),
        scratch_shapes=[pltpu.VMEM((GDN_HEADS, GDN_HEAD_DIM, GDN_HEAD_DIM), F32)],
        compiler_params=pltpu.CompilerParams(dimension_semantics=("arbitrary", "arbitrary"),
                                             vmem_limit_bytes=VMEM_LIMIT_BYTES),
        name="gdn",
    )(qkv, gate, bg, onorm_g, ltri)


def _swa_kernel(q_ref, k_ref, v_ref, bias_ref, sel_ref, o_ref, acc_scr, m_scr, s_scr, out_scr,
                *, w):
    n_res = SWA_RESIDUES
    tile = pl.program_id(1)
    n_slabs = SWA_WIDTH // LANES
    heads_per_slab = LANES // SWA_HEAD_DIM
    assert heads_per_slab == 2
    tasks_per_call = lambda dil: 2 if (w * dil // n_res) % BF16_ROWS else 1
    lane = lax.broadcasted_iota(jnp.int32, (w, LANES), 1)
    low_half = lane < SWA_HEAD_DIM

    def expand(packed):
        hi = packed.astype(BF16)
        lo = (packed - hi.astype(F32)).astype(BF16)
        return jnp.dot(jnp.concatenate([hi, lo], axis=1), sel_ref[...], preferred_element_type=F32)

    def task(pat, dil, t, sub):
        first = pat == 0
        n_chunks = n_res // dil
        c = w // n_chunks
        per_call = tasks_per_call(dil)
        t = t * per_call + sub
        res_d = jnp.bitwise_and(t, dil - 1)
        blk = jnp.right_shift(t, dil.bit_length() - 1)
        n_glob = tile * n_chunks + blk
        q0 = pl.multiple_of(blk * c, c)
        cur0 = pl.multiple_of(n_glob * c, c)
        prev0 = pl.multiple_of(jnp.maximum(n_glob - 1, 0) * c, c)
        is_first_blk = (n_glob == 0).astype(jnp.int32)
        res_of = [a * dil + res_d for a in range(n_chunks)]

        def gather(ref, lead, start):
            return jnp.concatenate([ref[lead + (res, pl.ds(start, c), slice(None))]
                                    for res in res_of], axis=0)

        def operand(ref, slab, start, shift):
            if c % BF16_ROWS == 0:
                return gather(ref, (slab,), start)
            assert BF16_ROWS == 2 * c and dil == 1 and per_call % 2 == 0
            half = (sub + shift) % 2
            tile0 = pl.multiple_of(jnp.maximum(start - half * c, 0), BF16_ROWS)
            return jnp.concatenate(
                [ref[slab, res, pl.ds(tile0, BF16_ROWS), :].astype(F32)[half * c:(half + 1) * c]
                 for res in res_of], axis=0).astype(BF16)

        logits, vps = [], []
        for slab in range(n_slabs):
            qp = operand(q_ref, slab, q0, 0)
            kp = jnp.concatenate([operand(k_ref, slab, prev0, -1), operand(k_ref, slab, cur0, 0)],
                                 axis=0)
            vps.append(jnp.concatenate([operand(v_ref, slab, prev0, -1),
                                        operand(v_ref, slab, cur0, 0)], axis=0))
            for e in range(heads_per_slab):
                mine = low_half if e == 0 else jnp.logical_not(low_half)
                qh = jnp.where(mine, qp, jnp.zeros_like(qp))
                lg = lax.dot_general(qh, kp, NT_DIMS, preferred_element_type=F32)
                logits.append(lg + bias_ref[pat, is_first_blk, slab * heads_per_slab + e])
        m_blk = jnp.zeros((w, LANES), F32)
        s_blk = jnp.zeros((w, LANES), F32)
        ps = []
        for h, lg in enumerate(logits):
            m_h = jnp.max(lg, axis=-1, keepdims=True)
            p = jnp.exp2(lg - m_h)
            ps.append(p.astype(BF16))
            m_blk = jnp.where(lane == h, m_h, m_blk)
            s_blk = jnp.where(lane == h, jnp.sum(p, axis=-1, keepdims=True), s_blk)
        nums = []
        for slab in range(n_slabs):
            h0 = slab * heads_per_slab
            outs = [jnp.dot(ps[h0 + e], vps[slab], preferred_element_type=F32)
                    for e in range(heads_per_slab)]
            nums.append(jnp.where(low_half, outs[0], outs[1]))
        if first:
            m_new, s_new = m_blk, s_blk
        else:
            m_old = gather(m_scr, (), q0)
            m_new = jnp.maximum(m_old, m_blk)
            a_old = jnp.exp2(m_old - m_new)
            a_blk = jnp.exp2(m_blk - m_new)
            s_new = a_old * gather(s_scr, (), q0) + a_blk * s_blk
            w_old, w_blk = expand(a_old), expand(a_blk)
            for slab in range(n_slabs):
                cols = slice(slab * LANES, (slab + 1) * LANES)
                nums[slab] = (w_old[:, cols] * gather(acc_scr, (slab,), q0)
                              + w_blk[:, cols] * nums[slab])
        for idx, res in enumerate(res_of):
            rows, part = pl.ds(q0, c), slice(idx * c, (idx + 1) * c)
            for slab in range(n_slabs):
                acc_scr[slab, res, rows, :] = nums[slab][part]
            m_scr[res, rows, :] = m_new[part]
            s_scr[res, rows, :] = s_new[part]

    n_tasks = n_res
    for pat, (window, dil) in enumerate(DILATED_PATTERNS):
        assert window // dil == w and n_res % dil == 0 and (w * dil) % n_res == 0

        per_call = tasks_per_call(dil)

        def body(t, carry, pat=pat, dil=dil, per_call=per_call):
            for sub in range(per_call):
                task(pat, dil, t, sub)
            return carry
        lax.fori_loop(0, n_tasks // per_call, body, 0, unroll=SWA_TASK_UNROLL // per_call)

    def finish(r, carry):
        den = expand(s_scr[r])
        for slab in range(n_slabs):
            cols = slice(slab * LANES, (slab + 1) * LANES)
            out_scr[slab, pl.ds(r, w, stride=n_res), :] = acc_scr[slab, r] / den[:, cols]
        return carry
    lax.fori_loop(0, n_res, finish, 0, unroll=4)
    for slab in range(n_slabs):
        o_ref[:, slab * LANES:(slab + 1) * LANES] = out_scr[slab].astype(o_ref.dtype)


def _swa(qb, kb, vb, bias, batch, seq):
    n_slabs = SWA_WIDTH // LANES
    n_res = SWA_RESIDUES
    w = DILATED_PATTERNS[0][0] // DILATED_PATTERNS[0][1]
    tq = n_res * w
    nq = seq // tq
    q_blk = pl.BlockSpec((n_slabs, None, n_res, w, LANES), lambda b, i: (0, b, 0, i, 0))
    seq_blk = pl.BlockSpec((n_slabs, None, n_res, seq // n_res, LANES), lambda b, i: (0, b, 0, 0, 0))
    head_of_col = jnp.arange(SWA_WIDTH) // SWA_HEAD_DIM
    sel = (jnp.arange(LANES)[:, None] == head_of_col[None, :]).astype(BF16)
    sel = jnp.concatenate([sel, sel], axis=0)
    return pl.pallas_call(
        functools.partial(_swa_kernel, w=w),
        grid=(batch, nq),
        in_specs=[q_blk, seq_blk, seq_blk, _resident(bias.shape), _resident(sel.shape)],
        out_specs=pl.BlockSpec((tq, SWA_WIDTH), lambda b, i: (b * nq + i, 0)),
        out_shape=jax.ShapeDtypeStruct((batch * seq, SWA_WIDTH), BF16),
        scratch_shapes=[pltpu.VMEM((n_slabs, n_res, w, LANES), F32),
                        pltpu.VMEM((n_res, w, LANES), F32), pltpu.VMEM((n_res, w, LANES), F32),
                        pltpu.VMEM((n_slabs, tq, LANES), F32)],
        compiler_params=pltpu.CompilerParams(dimension_semantics=("arbitrary", "arbitrary"),
                                             vmem_limit_bytes=VMEM_LIMIT_BYTES),
        name="swa",
    )(qb, kb, vb, bias, sel)


def _t5_causal_bucket(dist):
    max_exact = NUM_BUCKETS // 2
    d = jnp.maximum(dist, 1).astype(F32)
    log_b = max_exact + (jnp.log(d / max_exact) / math.log(MAX_DISTANCE / max_exact)
                         * (NUM_BUCKETS - max_exact)).astype(jnp.int32)
    return jnp.where(dist < max_exact, dist, jnp.minimum(log_b, NUM_BUCKETS - 1))


def _swa_bias_tables(rel_bias):
    w = DILATED_PATTERNS[0][0] // DILATED_PATTERNS[0][1]
    rels, steps = [], []
    for window, dil in DILATED_PATTERNS:
        n_chunks = SWA_RESIDUES // dil
        c = w // n_chunks
        pos = (np.arange(w) % c) * n_chunks + np.arange(w) // c
        rels.append(pos[:, None] + w - np.concatenate([pos, w + pos])[None, :])
        steps.append(rel_bias[_t5_causal_bucket(jnp.arange(w + 1) * dil)].astype(F32))
    rel = np.stack(rels)
    one_hot = (jnp.asarray(np.clip(rel, 0, w))[..., None] == jnp.arange(w + 1)).astype(F32)
    table = jnp.einsum("prh,pijr->phij", jnp.stack(steps), one_hot,
                       precision=lax.Precision.HIGHEST) * math.log2(math.e)
    in_band = np.logical_and(rel >= 0, rel <= w)[:, None]
    no_prev = np.arange(2 * w) >= w
    keep = np.stack([in_band, np.logical_and(in_band, no_prev)], axis=1)
    return jnp.where(jnp.asarray(keep), table[:, None], -jnp.inf)


def _mixffn_kernel(x_ref, oa_ref, ob_ref, woa_ref, wob_ref, gpost_ref, gpre_ref, gfpost_ref,
                   wg_ref, wu_ref, wd_ref, out_ref, act_scr, x1_scr, *, ff_chunk, n_sub):
    sub = x_ref.shape[0] // n_sub
    rows = [slice(i * sub, (i + 1) * sub) for i in range(n_sub)]
    d_ff = wg_ref.shape[1]
    mixes = [jnp.dot(oa_ref[r, :], woa_ref[...], preferred_element_type=F32)
             + jnp.dot(ob_ref[r, :], wob_ref[...], preferred_element_type=F32) for r in rows]
    for r, mix in zip(rows, mixes):
        x1 = x_ref[r, :] + _rms(mix, gpost_ref[...])
        x1_scr[r, :] = x1
        h = _rms(x1, gpre_ref[...]).astype(BF16)
        for c in range(d_ff // ff_chunk):
            cols = slice(c * ff_chunk, (c + 1) * ff_chunk)
            gate = jnp.dot(h, wg_ref[:, cols], preferred_element_type=F32)
            up = jnp.dot(h, wu_ref[:, cols], preferred_element_type=F32)
            act_scr[r, cols] = (_silu(gate) * up).astype(BF16)
    fs = [jnp.dot(act_scr[r, :], wd_ref[...], preferred_element_type=F32) for r in rows]
    for r, f in zip(rows, fs):
        out_ref[r, :] = x1_scr[r, :] + _rms(f, gfpost_ref[...])


def _mixffn(x2d, oa, ob, woa, wob, gpost, gpre, gfpost, wg, wu, wd, tm, sub, ff_chunk):
    T, D = x2d.shape
    d_ff = wg.shape[1]
    row = lambda n: pl.BlockSpec((tm, n), lambda i: (i, 0))
    return pl.pallas_call(
        functools.partial(_mixffn_kernel, ff_chunk=ff_chunk, n_sub=tm // sub),
        grid=(T // tm,),
        in_specs=([row(D), row(GDN_WIDTH), row(SWA_WIDTH)]
                  + [_resident(woa.shape), _resident(wob.shape)] + [_resident((1, D))] * 3
                  + [_resident(wg.shape), _resident(wu.shape), _resident(wd.shape)]),
        out_specs=row(D),
        out_shape=jax.ShapeDtypeStruct((T, D), F32),
        scratch_shapes=[pltpu.VMEM((tm, d_ff), BF16), pltpu.VMEM((tm, D), F32)],
        compiler_params=pltpu.CompilerParams(dimension_semantics=("arbitrary",),
                                             vmem_limit_bytes=VMEM_LIMIT_BYTES),
        name="mixffn",
    )(x2d, oa, ob, woa, wob, gpost, gpre, gfpost, wg, wu, wd)


def _layer(x2d, batch, seq, w_in, conv_w, a_log, dt_bias, onorm_g, rel_bias, w_out,
           g_mix_pre, g_mix_post, w_gate, w_up, w_down, g_ffn_pre, g_ffn_post):
    D = x2d.shape[1]
    gw, sw, nh = GDN_WIDTH, SWA_WIDTH, GDN_HEADS
    wb = w_in.astype(BF16)
    c_gate, c_small, c_q = 3 * gw, 4 * gw, 4 * gw + 2 * nh
    w_small = jnp.pad(wb[:, c_small:c_q], ((0, 0), (0, LANES - 2 * nh)))
    lane_pad = lambda v: jnp.pad(v.astype(F32).reshape(1, nh), ((0, 0), (nh, LANES - 2 * nh)))
    qkv, gate, qb, kb, vb, bg = _inproj(
        x2d, g_mix_pre.reshape(1, D), wb[:, :c_gate], wb[:, c_gate:c_small],
        wb[:, c_q:c_q + sw], wb[:, c_q + sw:c_q + 2 * sw], wb[:, c_q + 2 * sw:], w_small,
        conv_w.astype(F32), lane_pad(a_log), lane_pad(dt_bias), batch, seq, tm=512)

    oa = _gdn(qkv, gate, bg, onorm_g.astype(F32).reshape(1, GDN_HEAD_DIM), batch, seq, tb=512)

    ob = _swa(qb, kb, vb, _swa_bias_tables(rel_bias), batch, seq)

    wo = w_out.astype(BF16)
    return _mixffn(x2d, oa, ob, wo[:gw], wo[gw:], g_mix_post.reshape(1, D),
                   g_ffn_pre.reshape(1, D), g_ffn_post.reshape(1, D), w_gate.astype(BF16),
                   w_up.astype(BF16), w_down.astype(BF16), tm=1024, sub=256, ff_chunk=256)


def kernel(x, w_in, conv_w, a_log, dt_bias, onorm_g, rel_bias, w_out, g_mix_pre, g_mix_post,
           w_gate, w_up, w_down, g_ffn_pre, g_ffn_post):
    batch, seq, d_model = x.shape
    x2d = x.reshape(batch * seq, d_model)
    for l in range(w_in.shape[0]):
        x2d = _layer(x2d, batch, seq, w_in[l], conv_w[l], a_log[l], dt_bias[l], onorm_g[l],
                     rel_bias, w_out[l], g_mix_pre[l], g_mix_post[l], w_gate[l], w_up[l],
                     w_down[l], g_ffn_pre[l], g_ffn_post[l])
    return x2d.reshape(batch, seq, d_model)
```

```python
import functools
import math

import jax
import jax.numpy as jnp
import numpy as np
from jax import lax
from jax.experimental import pallas as pl
from jax.experimental.pallas import tpu as pltpu

F32 = jnp.float32
BF16 = jnp.bfloat16

GDN_HEADS = 4
GDN_HEAD_DIM = 128
GDN_WIDTH = GDN_HEADS * GDN_HEAD_DIM
CONV_WIDTH = 4
CONV_ROW_PITCH = 2
CHUNK = 64
SWA_HEADS = 8
SWA_HEAD_DIM = 64
SWA_WIDTH = SWA_HEADS * SWA_HEAD_DIM
DILATED_PATTERNS = ((128, 1), (512, 4), (2048, 16))
SWA_RESIDUES = max(dil for _, dil in DILATED_PATTERNS)
NUM_BUCKETS = 32
MAX_DISTANCE = 2048
RMS_EPS = 1e-6

LANES = 128
SUBLANES = 8
BF16_ROWS = 16
SWA_Q_SCALE = SWA_HEAD_DIM ** -0.5 * math.log2(math.e)
SWA_TASK_UNROLL = 8
VMEM_LIMIT_BYTES = 56 * 1024 * 1024

NT_DIMS = (((1,), (1,)), ((), ()))
TN_DIMS = (((0,), (0,)), ((), ()))


def _rms(x, g):
    return x * lax.rsqrt(jnp.mean(x * x, axis=-1, keepdims=True) + RMS_EPS) * g


def _silu(x):
    return x * jax.nn.sigmoid(x)


def _softplus(x):
    return jnp.maximum(x, 0.0) + jnp.log1p(jnp.exp(-jnp.abs(x)))


def _resident(shape):
    zeros = (0,) * len(shape)
    return pl.BlockSpec(shape, lambda *_: zeros, pipeline_mode=pl.Buffered(1))


def _inproj_kernel(x_ref, g_ref, wqkv_ref, wgate_ref, wq_ref, wk_ref, wv_ref, ws_ref,
                   convw_ref, alog_ref, dtb_ref,
                   qkv_ref, gate_ref, qb_ref, kb_ref, vb_ref, bg_ref,
                   stage_scr, stage2_scr, ext_scr, *, n_sub):
    tm = x_ref.shape[0]
    hd = GDN_HEAD_DIM
    n_groups = 3 * GDN_WIDTH // hd
    halo = SUBLANES
    sub = tm // n_sub
    rows = [slice(i * sub, (i + 1) * sub) for i in range(n_sub)]
    hs = [_rms(x_ref[r, :], g_ref[...]).astype(BF16) for r in rows]

    at = lambda first, n: pl.ds(CONV_ROW_PITCH * first, n, stride=CONV_ROW_PITCH)

    @pl.when(pl.program_id(1) == 0)
    def _():
        for j in range(n_groups):
            ext_scr[j, at(0, halo), :] = jnp.zeros((halo, hd), F32)

    @pl.when(pl.program_id(1) > 0)
    def _():
        for j in range(n_groups):
            ext_scr[j, at(0, halo), :] = ext_scr[j, at(tm, halo), :]

    pres = [jnp.dot(h, wqkv_ref[...], preferred_element_type=F32) for h in hs]
    for i, pre in enumerate(pres):
        for j in range(n_groups):
            ext_scr[j, at(halo + i * sub, sub), :] = pre[:, j * hd:(j + 1) * hd]

    def conv_act(i):
        for j in range(n_groups):
            cols = slice(j * hd, (j + 1) * hd)
            acc = None
            for tap in range(CONV_WIDTH):
                first = halo + i * sub - (CONV_WIDTH - 1) + tap
                term = convw_ref[tap:tap + 1, cols] * ext_scr[j, at(first, sub), :]
                acc = term if acc is None else acc + term
            y = _silu(acc)
            kind = j // GDN_HEADS
            if kind < 2:
                y = y * lax.rsqrt(jnp.sum(y * y, axis=-1, keepdims=True) + 1e-6)
            if kind == 0:
                y = y * (hd ** -0.5)
            qkv_ref[rows[i], cols] = y

    n_res = SWA_RESIDUES
    f = math.isqrt(n_res)
    assert f * f == n_res
    attn_proj = ((wq_ref, qb_ref, SWA_Q_SCALE), (wk_ref, kb_ref, None), (wv_ref, vb_ref, None))

    def split(i, t, o_ref):
        r0 = i * sub
        for slab in range(SWA_WIDTH // LANES):
            for a in range(f):
                stage2_scr[t, slab, a, r0 // f:(r0 + sub) // f, :] = (
                    stage_scr[t, slab, pl.ds(r0 + a, sub // f, stride=f), :])
        for slab in range(SWA_WIDTH // LANES):
            for a in range(f):
                for b in range(f):
                    o_ref[slab, f * b + a, r0 // n_res:(r0 + sub) // n_res, :] = stage2_scr[
                        t, slab, a, pl.ds(r0 // f + b, sub // n_res, stride=f), :].astype(o_ref.dtype)

    for i, h in enumerate(hs):
        gate_ref[rows[i], :] = _silu(jnp.dot(h, wgate_ref[...], preferred_element_type=F32))
        sm = jnp.dot(h, ws_ref[...], preferred_element_type=F32)
        is_beta = lax.broadcasted_iota(jnp.int32, sm.shape, 1) < GDN_HEADS
        bg_ref[rows[i], :] = jnp.where(is_beta, jax.nn.sigmoid(sm),
                                       -jnp.exp(alog_ref[...]) * _softplus(sm + dtb_ref[...]))
        for t, (w_ref, _, scale) in enumerate(attn_proj):
            res = jnp.dot(h, w_ref[...], preferred_element_type=F32)
            if scale is not None:
                res = res * scale
            for slab in range(SWA_WIDTH // LANES):
                stage_scr[t, slab, rows[i], :] = res[:, slab * LANES:(slab + 1) * LANES]
        conv_act(i)
        for t, (_, o_ref, _) in enumerate(attn_proj):
            split(i, t, o_ref)


def _inproj(x2d, g, wqkv, wgate, wq, wk, wv, ws, conv_w, alog_row, dtb_row, batch, seq, tm):
    T, D = x2d.shape
    nt = seq // tm
    row = lambda n: pl.BlockSpec((tm, n), lambda b, j: (b * nt + j, 0))
    n_slabs = SWA_WIDTH // LANES
    slabs = pl.BlockSpec((n_slabs, None, SWA_RESIDUES, tm // SWA_RESIDUES, LANES),
                         lambda b, j: (0, b, 0, j, 0))
    slab_shape = jax.ShapeDtypeStruct(
        (n_slabs, batch, SWA_RESIDUES, seq // SWA_RESIDUES, LANES), BF16)
    return pl.pallas_call(
        functools.partial(_inproj_kernel, n_sub=2),
        grid=(batch, nt),
        in_specs=([row(D), _resident((1, D))]
                  + [_resident(w.shape) for w in (wqkv, wgate, wq, wk, wv, ws, conv_w)]
                  + [_resident((1, LANES))] * 2),
        out_specs=[row(3 * GDN_WIDTH), row(GDN_WIDTH), slabs, slabs, slabs, row(LANES)],
        out_shape=[jax.ShapeDtypeStruct((T, 3 * GDN_WIDTH), F32),
                   jax.ShapeDtypeStruct((T, GDN_WIDTH), F32), slab_shape, slab_shape, slab_shape,
                   jax.ShapeDtypeStruct((T, LANES), F32)],
        scratch_shapes=[pltpu.VMEM((3, n_slabs, tm, LANES), F32),
                        pltpu.VMEM((3, n_slabs, math.isqrt(SWA_RESIDUES),
                                    tm // math.isqrt(SWA_RESIDUES), LANES), F32),
                        pltpu.VMEM((3 * GDN_WIDTH // GDN_HEAD_DIM, CONV_ROW_PITCH * (tm + SUBLANES),
                                    GDN_HEAD_DIM), F32)],
        compiler_params=pltpu.CompilerParams(dimension_semantics=("arbitrary", "arbitrary"),
                                             vmem_limit_bytes=VMEM_LIMIT_BYTES),
        name="inproj",
    )(x2d, g, wqkv, wgate, wq, wk, wv, ws, conv_w, alog_row, dtb_row)


def _unit_lower_inverses(mats, eye, xor_ij):
    n = mats[0].shape[0]
    ds = [eye - jnp.where(xor_ij < 2, a, 0.0) for a in mats]
    s = 2
    while s < n:
        band = jnp.logical_and(xor_ij >= s, xor_ij < 2 * s)
        dbs = [d.astype(BF16) for d in ds]
        eds = [jnp.dot(jnp.where(band, a, 0.0).astype(BF16), db,
                       preferred_element_type=F32).astype(BF16) for a, db in zip(mats, dbs)]
        ds = [d - jnp.dot(db, ed, preferred_element_type=F32) for d, db, ed in zip(ds, dbs, eds)]
        s *= 2
    return ds


def _gdn_kernel(qkv_ref, gate_ref, bg_ref, ong_ref, ltri_ref, o_ref, state_scr, *, tb, chunk):
    hd = GDN_HEAD_DIM

    @pl.when(pl.program_id(1) == 0)
    def _():
        state_scr[...] = jnp.zeros_like(state_scr)

    ii = lax.broadcasted_iota(jnp.int32, (chunk, chunk), 0)
    jj = lax.broadcasted_iota(jnp.int32, (chunk, chunk), 1)
    xor_ij = jnp.bitwise_xor(ii, jj)
    lower = ii >= jj
    strict = ii > jj
    eye = jnp.where(ii == jj, 1.0, 0.0).astype(F32)
    ltri = ltri_ref[...]
    ong = ong_ref[...]

    nc = tb // chunk
    heads = range(GDN_HEADS)
    rows = [slice(c * chunk, (c + 1) * chunk) for c in range(nc)]
    hcols = [slice(h * hd, (h + 1) * hd) for h in heads]
    qcols, kcols, vcols = ([slice(part * GDN_WIDTH + h * hd, part * GDN_WIDTH + (h + 1) * hd)
                            for h in heads] for part in range(3))
    seqs = range(qkv_ref.shape[0])
    probs = [(s, c, h) for s in seqs for c in range(nc) for h in heads]

    gcs = {}
    for s in seqs:
        for c in range(nc):
            g = bg_ref[s, rows[c], :]
            g_hi = g.astype(BF16)
            rem = g - g_hi.astype(F32)
            g_mid = rem.astype(BF16)
            g_lo = (rem - g_mid.astype(F32)).astype(BF16)
            gcs[s, c] = (jnp.dot(ltri, g_hi, preferred_element_type=F32)
                         + jnp.dot(ltri, g_mid, preferred_element_type=F32)
                         + jnp.dot(ltri, g_lo, preferred_element_type=F32))
    gcts = {sc: gc.T for sc, gc in gcs.items()}
    lane_of = lambda h: slice(GDN_HEADS + h, GDN_HEADS + h + 1)
    gcol = {(s, c, h): gcs[s, c][:, lane_of(h)] for s, c, h in probs}
    glast = {(s, c, h): gcs[s, c][chunk - 1:chunk, lane_of(h)] for s, c, h in probs}
    bcol = {(s, c, h): bg_ref[s, rows[c], h:h + 1] for s, c, h in probs}

    qks = {}
    for s, c, h in probs:
        kcb = qkv_ref[s, rows[c], kcols[h]].astype(BF16)
        qkb = jnp.concatenate([qkv_ref[s, rows[c], qcols[h]].astype(BF16), kcb], axis=0)
        qks[s, c, h] = lax.dot_general(qkb, kcb, NT_DIMS, preferred_element_type=F32)
    attn, a_mats = {}, []
    for s, c, h in probs:
        grow = gcts[s, c][lane_of(h), :]
        decay = jnp.exp(jnp.where(lower, gcol[s, c, h] - grow, -jnp.inf))
        attn[s, c, h] = (qks[s, c, h][:chunk] * decay).astype(BF16)
        a_mats.append(jnp.where(strict, qks[s, c, h][chunk:] * decay * bcol[s, c, h], 0.0))
    t_invs = _unit_lower_inverses(a_mats, eye, xor_ij)

    uws = {}
    for (s, c, h), t_inv in zip(probs, t_invs):
        kc = qkv_ref[s, rows[c], kcols[h]]
        eg = jnp.exp(gcol[s, c, h])
        rhs = jnp.concatenate([qkv_ref[s, rows[c], vcols[h]] * bcol[s, c, h],
                               kc * (bcol[s, c, h] * eg)], axis=1).astype(BF16)
        uws[s, c, h] = jnp.dot(t_inv.astype(BF16), rhs, preferred_element_type=F32).astype(BF16)
    kd_uw, gq, attn_u = {}, {}, {}
    for s, c, h in probs:
        k_dec = qkv_ref[s, rows[c], kcols[h]] * jnp.exp(glast[s, c, h] - gcol[s, c, h])
        kd_uw[s, c, h] = lax.dot_general(k_dec.astype(BF16), uws[s, c, h], TN_DIMS,
                                         preferred_element_type=F32)
    for s, c, h in probs:
        a_uw = jnp.dot(attn[s, c, h], uws[s, c, h], preferred_element_type=F32)
        q_eff = qkv_ref[s, rows[c], qcols[h]] * jnp.exp(gcol[s, c, h]) - a_uw[:, hd:]
        gq[s, c, h] = jnp.concatenate([kd_uw[s, c, h][:, hd:], q_eff], axis=0).astype(BF16)
        attn_u[s, c, h] = a_uw[:, :hd]

    chains = [(s, h) for s in seqs for h in heads]
    state = {sh: state_scr[sh] for sh in chains}
    for c in range(nc):
        gs = {(s, h): jnp.dot(gq[s, c, h], state[s, h].astype(BF16), preferred_element_type=F32)
              for s, h in chains}
        state = {(s, h): (state[s, h] * jnp.exp(glast[s, c, h]) - gs[s, h][:hd]
                          + kd_uw[s, c, h][:, :hd]) for s, h in chains}
        for s, h in chains:
            o = gs[s, h][hd:] + attn_u[s, c, h]
            gated = _rms(o, ong) * gate_ref[s, rows[c], hcols[h]]
            o_ref[s, rows[c], hcols[h]] = gated.astype(o_ref.dtype)
    for sh in chains:
        state_scr[sh] = state[sh]


def _gdn(qkv, gate, bg, onorm_g, batch, seq, tb, n_seq):
    idx = jnp.arange(CHUNK)
    ltri = (idx[:, None] >= idx[None, :]).astype(BF16)
    blk = lambda n: pl.BlockSpec((n_seq, tb, n), lambda b, t: (b, t, 0))
    per_seq = lambda a: a.reshape(batch, seq, a.shape[-1])
    kern = functools.partial(_gdn_kernel, tb=tb, chunk=CHUNK)
    out = pl.pallas_call(
        kern,
        grid=(batch // n_seq, seq // tb),
        in_specs=[blk(3 * GDN_WIDTH), blk(GDN_WIDTH), blk(LANES), _resident((1, GDN_HEAD_DIM)),
                  _resident((CHUNK, CHUNK))],
        out_specs=blk(GDN_WIDTH),
        out_shape=jax.ShapeDtypeStruct((batch, seq, GDN_WIDTH), BF16),
        scratch_shapes=[pltpu.VMEM((n_seq, GDN_HEADS, GDN_HEAD_DIM, GDN_HEAD_DIM), F32)],
        compiler_params=pltpu.CompilerParams(dimension_semantics=("arbitrary", "arbitrary"),
                                             vmem_limit_bytes=VMEM_LIMIT_BYTES),
        name="gdn",
    )(per_seq(qkv), per_seq(gate), per_seq(bg), onorm_g, ltri)
    return out.reshape(batch * seq, GDN_WIDTH)


def _swa_kernel(q_ref, k_ref, v_ref, bias_ref, sel_ref, o_ref, acc_scr, m_scr, s_scr, out_scr,
                *, w):
    n_res = SWA_RESIDUES
    tile = pl.program_id(1)
    n_slabs = SWA_WIDTH // LANES
    heads_per_slab = LANES // SWA_HEAD_DIM
    assert heads_per_slab == 2
    tasks_per_call = lambda dil: 2 if (w * dil // n_res) % BF16_ROWS else 1
    lane = lax.broadcasted_iota(jnp.int32, (w, LANES), 1)
    low_half = lane < SWA_HEAD_DIM

    def expand(packed):
        hi = packed.astype(BF16)
        lo = (packed - hi.astype(F32)).astype(BF16)
        return jnp.dot(jnp.concatenate([hi, lo], axis=1), sel_ref[...], preferred_element_type=F32)

    def task(pat, dil, t, sub):
        first = pat == 0
        n_chunks = n_res // dil
        c = w // n_chunks
        per_call = tasks_per_call(dil)
        t = t * per_call + sub
        res_d = jnp.bitwise_and(t, dil - 1)
        blk = jnp.right_shift(t, dil.bit_length() - 1)
        n_glob = tile * n_chunks + blk
        q0 = pl.multiple_of(blk * c, c)
        cur0 = pl.multiple_of(n_glob * c, c)
        prev0 = pl.multiple_of(jnp.maximum(n_glob - 1, 0) * c, c)
        is_first_blk = (n_glob == 0).astype(jnp.int32)
        res_of = [a * dil + res_d for a in range(n_chunks)]

        def gather(ref, lead, start):
            return jnp.concatenate([ref[lead + (res, pl.ds(start, c), slice(None))]
                                    for res in res_of], axis=0)

        def operand(ref, slab, start, shift):
            if c % BF16_ROWS == 0:
                return gather(ref, (slab,), start)
            assert BF16_ROWS == 2 * c and dil == 1 and per_call % 2 == 0
            half = (sub + shift) % 2
            tile0 = pl.multiple_of(jnp.maximum(start - half * c, 0), BF16_ROWS)
            return jnp.concatenate(
                [ref[slab, res, pl.ds(tile0, BF16_ROWS), :].astype(F32)[half * c:(half + 1) * c]
                 for res in res_of], axis=0).astype(BF16)

        logits, vps = [], []
        for slab in range(n_slabs):
            qp = operand(q_ref, slab, q0, 0)
            kp = jnp.concatenate([operand(k_ref, slab, prev0, -1), operand(k_ref, slab, cur0, 0)],
                                 axis=0)
            vps.append(jnp.concatenate([operand(v_ref, slab, prev0, -1),
                                        operand(v_ref, slab, cur0, 0)], axis=0))
            for e in range(heads_per_slab):
                mine = low_half if e == 0 else jnp.logical_not(low_half)
                qh = jnp.where(mine, qp, jnp.zeros_like(qp))
                lg = lax.dot_general(qh, kp, NT_DIMS, preferred_element_type=F32)
                logits.append(lg + bias_ref[pat, is_first_blk, slab * heads_per_slab + e])
        m_blk = jnp.zeros((w, LANES), F32)
        s_blk = jnp.zeros((w, LANES), F32)
        ps = []
        for h, lg in enumerate(logits):
            m_h = jnp.max(lg, axis=-1, keepdims=True)
            p = jnp.exp2(lg - m_h)
            ps.append(p.astype(BF16))
            m_blk = jnp.where(lane == h, m_h, m_blk)
            s_blk = jnp.where(lane == h, jnp.sum(p, axis=-1, keepdims=True), s_blk)
        nums = []
        for slab in range(n_slabs):
            h0 = slab * heads_per_slab
            outs = [jnp.dot(ps[h0 + e], vps[slab], preferred_element_type=F32)
                    for e in range(heads_per_slab)]
            nums.append(jnp.where(low_half, outs[0], outs[1]))
        if first:
            m_new, s_new = m_blk, s_blk
        else:
            m_old = gather(m_scr, (), q0)
            m_new = jnp.maximum(m_old, m_blk)
            a_old = jnp.exp2(m_old - m_new)
            a_blk = jnp.exp2(m_blk - m_new)
            s_new = a_old * gather(s_scr, (), q0) + a_blk * s_blk
            w_old, w_blk = expand(a_old), expand(a_blk)
            for slab in range(n_slabs):
                cols = slice(slab * LANES, (slab + 1) * LANES)
                nums[slab] = (w_old[:, cols] * gather(acc_scr, (slab,), q0)
                              + w_blk[:, cols] * nums[slab])
        for idx, res in enumerate(res_of):
            rows, part = pl.ds(q0, c), slice(idx * c, (idx + 1) * c)
            for slab in range(n_slabs):
                acc_scr[slab, res, rows, :] = nums[slab][part]
            m_scr[res, rows, :] = m_new[part]
            s_scr[res, rows, :] = s_new[part]

    n_tasks = n_res
    for pat, (window, dil) in enumerate(DILATED_PATTERNS):
        assert window // dil == w and n_res % dil == 0 and (w * dil) % n_res == 0

        per_call = tasks_per_call(dil)

        def body(t, carry, pat=pat, dil=dil, per_call=per_call):
            for sub in range(per_call):
                task(pat, dil, t, sub)
            return carry
        lax.fori_loop(0, n_tasks // per_call, body, 0, unroll=SWA_TASK_UNROLL // per_call)

    def finish(r, carry):
        den = expand(s_scr[r])
        for slab in range(n_slabs):
            cols = slice(slab * LANES, (slab + 1) * LANES)
            out_scr[slab, pl.ds(r, w, stride=n_res), :] = acc_scr[slab, r] / den[:, cols]
        return carry
    lax.fori_loop(0, n_res, finish, 0, unroll=4)
    for slab in range(n_slabs):
        o_ref[:, slab * LANES:(slab + 1) * LANES] = out_scr[slab].astype(o_ref.dtype)


def _swa(qb, kb, vb, bias, batch, seq):
    n_slabs = SWA_WIDTH // LANES
    n_res = SWA_RESIDUES
    w = DILATED_PATTERNS[0][0] // DILATED_PATTERNS[0][1]
    tq = n_res * w
    nq = seq // tq
    q_blk = pl.BlockSpec((n_slabs, None, n_res, w, LANES), lambda b, i: (0, b, 0, i, 0))
    seq_blk = pl.BlockSpec((n_slabs, None, n_res, seq // n_res, LANES), lambda b, i: (0, b, 0, 0, 0))
    head_of_col = jnp.arange(SWA_WIDTH) // SWA_HEAD_DIM
    sel = (jnp.arange(LANES)[:, None] == head_of_col[None, :]).astype(BF16)
    sel = jnp.concatenate([sel, sel], axis=0)
    return pl.pallas_call(
        functools.partial(_swa_kernel, w=w),
        grid=(batch, nq),
        in_specs=[q_blk, seq_blk, seq_blk, _resident(bias.shape), _resident(sel.shape)],
        out_specs=pl.BlockSpec((tq, SWA_WIDTH), lambda b, i: (b * nq + i, 0)),
        out_shape=jax.ShapeDtypeStruct((batch * seq, SWA_WIDTH), BF16),
        scratch_shapes=[pltpu.VMEM((n_slabs, n_res, w, LANES), F32),
                        pltpu.VMEM((n_res, w, LANES), F32), pltpu.VMEM((n_res, w, LANES), F32),
                        pltpu.VMEM((n_slabs, tq, LANES), F32)],
        compiler_params=pltpu.CompilerParams(dimension_semantics=("arbitrary", "arbitrary"),
                                             vmem_limit_bytes=VMEM_LIMIT_BYTES),
        name="swa",
    )(qb, kb, vb, bias, sel)


def _t5_causal_bucket(dist):
    max_exact = NUM_BUCKETS // 2
    d = jnp.maximum(dist, 1).astype(F32)
    log_b = max_exact + (jnp.log(d / max_exact) / math.log(MAX_DISTANCE / max_exact)
                         * (NUM_BUCKETS - max_exact)).astype(jnp.int32)
    return jnp.where(dist < max_exact, dist, jnp.minimum(log_b, NUM_BUCKETS - 1))


def _swa_bias_tables(rel_bias):
    w = DILATED_PATTERNS[0][0] // DILATED_PATTERNS[0][1]
    rels, steps = [], []
    for window, dil in DILATED_PATTERNS:
        n_chunks = SWA_RESIDUES // dil
        c = w // n_chunks
        pos = (np.arange(w) % c) * n_chunks + np.arange(w) // c
        rels.append(pos[:, None] + w - np.concatenate([pos, w + pos])[None, :])
        steps.append(rel_bias[_t5_causal_bucket(jnp.arange(w + 1) * dil)].astype(F32))
    rel = np.stack(rels)
    one_hot = (jnp.asarray(np.clip(rel, 0, w))[..., None] == jnp.arange(w + 1)).astype(F32)
    table = jnp.einsum("prh,pijr->phij", jnp.stack(steps), one_hot,
                       precision=lax.Precision.HIGHEST) * math.log2(math.e)
    in_band = np.logical_and(rel >= 0, rel <= w)[:, None]
    no_prev = np.arange(2 * w) >= w
    keep = np.stack([in_band, np.logical_and(in_band, no_prev)], axis=1)
    return jnp.where(jnp.asarray(keep), table[:, None], -jnp.inf)


def _mixffn_kernel(x_ref, oa_ref, ob_ref, woa_ref, wob_ref, gpost_ref, gpre_ref, gfpost_ref,
                   wg_ref, wu_ref, wd_ref, out_ref, act_scr, x1_scr, *, ff_chunk, n_sub):
    sub = x_ref.shape[0] // n_sub
    rows = [slice(i * sub, (i + 1) * sub) for i in range(n_sub)]
    d_ff = wg_ref.shape[1]
    mixes = [jnp.dot(oa_ref[r, :], woa_ref[...], preferred_element_type=F32)
             + jnp.dot(ob_ref[r, :], wob_ref[...], preferred_element_type=F32) for r in rows]
    for r, mix in zip(rows, mixes):
        x1 = x_ref[r, :] + _rms(mix, gpost_ref[...])
        x1_scr[r, :] = x1
        h = _rms(x1, gpre_ref[...]).astype(BF16)
        for c in range(d_ff // ff_chunk):
            cols = slice(c * ff_chunk, (c + 1) * ff_chunk)
            gate = jnp.dot(h, wg_ref[:, cols], preferred_element_type=F32)
            up = jnp.dot(h, wu_ref[:, cols], preferred_element_type=F32)
            act_scr[r, cols] = (_silu(gate) * up).astype(BF16)
    fs = [jnp.dot(act_scr[r, :], wd_ref[...], preferred_element_type=F32) for r in rows]
    for r, f in zip(rows, fs):
        out_ref[r, :] = x1_scr[r, :] + _rms(f, gfpost_ref[...])


def _mixffn(x2d, oa, ob, woa, wob, gpost, gpre, gfpost, wg, wu, wd, tm, sub, ff_chunk):
    T, D = x2d.shape
    d_ff = wg.shape[1]
    row = lambda n: pl.BlockSpec((tm, n), lambda i: (i, 0))
    return pl.pallas_call(
        functools.partial(_mixffn_kernel, ff_chunk=ff_chunk, n_sub=tm // sub),
        grid=(T // tm,),
        in_specs=([row(D), row(GDN_WIDTH), row(SWA_WIDTH)]
                  + [_resident(woa.shape), _resident(wob.shape)] + [_resident((1, D))] * 3
                  + [_resident(wg.shape), _resident(wu.shape), _resident(wd.shape)]),
        out_specs=row(D),
        out_shape=jax.ShapeDtypeStruct((T, D), F32),
        scratch_shapes=[pltpu.VMEM((tm, d_ff), BF16), pltpu.VMEM((tm, D), F32)],
        compiler_params=pltpu.CompilerParams(dimension_semantics=("arbitrary",),
                                             vmem_limit_bytes=VMEM_LIMIT_BYTES),
        name="mixffn",
    )(x2d, oa, ob, woa, wob, gpost, gpre, gfpost, wg, wu, wd)


def _layer(x2d, batch, seq, w_in, conv_w, a_log, dt_bias, onorm_g, rel_bias, w_out,
           g_mix_pre, g_mix_post, w_gate, w_up, w_down, g_ffn_pre, g_ffn_post):
    D = x2d.shape[1]
    gw, sw, nh = GDN_WIDTH, SWA_WIDTH, GDN_HEADS
    wb = w_in.astype(BF16)
    c_gate, c_small, c_q = 3 * gw, 4 * gw, 4 * gw + 2 * nh
    w_small = jnp.pad(wb[:, c_small:c_q], ((0, 0), (0, LANES - 2 * nh)))
    lane_pad = lambda v: jnp.pad(v.astype(F32).reshape(1, nh), ((0, 0), (nh, LANES - 2 * nh)))
    qkv, gate, qb, kb, vb, bg = _inproj(
        x2d, g_mix_pre.reshape(1, D), wb[:, :c_gate], wb[:, c_gate:c_small],
        wb[:, c_q:c_q + sw], wb[:, c_q + sw:c_q + 2 * sw], wb[:, c_q + 2 * sw:], w_small,
        conv_w.astype(F32), lane_pad(a_log), lane_pad(dt_bias), batch, seq, tm=512)

    oa = _gdn(qkv, gate, bg, onorm_g.astype(F32).reshape(1, GDN_HEAD_DIM), batch, seq, tb=256, n_seq=2)

    ob = _swa(qb, kb, vb, _swa_bias_tables(rel_bias), batch, seq)

    wo = w_out.astype(BF16)
    return _mixffn(x2d, oa, ob, wo[:gw], wo[gw:], g_mix_post.reshape(1, D),
                   g_ffn_pre.reshape(1, D), g_ffn_post.reshape(1, D), w_gate.astype(BF16),
                   w_up.astype(BF16), w_down.astype(BF16), tm=1024, sub=256, ff_chunk=256)


def kernel(x, w_in, conv_w, a_log, dt_bias, onorm_g, rel_bias, w_out, g_mix_pre, g_mix_post,
           w_gate, w_up, w_down, g_ffn_pre, g_ffn_post):
    batch, seq, d_model = x.shape
    x2d = x.reshape(batch * seq, d_model)
    for l in range(w_in.shape[0]):
        x2d = _layer(x2d, batch, seq, w_in[l], conv_w[l], a_log[l], dt_bias[l], onorm_g[l],
                     rel_bias, w_out[l], g_mix_pre[l], g_mix_post[l], w_gate[l], w_up[l],
                     w_down[l], g_ffn_pre[l], g_ffn_post[l])
    return x2d.reshape(batch, seq, d_model)
```

```python
import functools
import math

import jax
import jax.numpy as jnp
import numpy as np
from jax import lax
from jax.experimental import pallas as pl
from jax.experimental.pallas import tpu as pltpu

F32 = jnp.float32
BF16 = jnp.bfloat16

GDN_HEADS = 4
GDN_HEAD_DIM = 128
GDN_WIDTH = GDN_HEADS * GDN_HEAD_DIM
CONV_WIDTH = 4
CONV_ROW_PITCH = 2
CHUNK = 64
SWA_HEADS = 8
SWA_HEAD_DIM = 64
SWA_WIDTH = SWA_HEADS * SWA_HEAD_DIM
DILATED_PATTERNS = ((128, 1), (512, 4), (2048, 16))
SWA_RESIDUES = max(dil for _, dil in DILATED_PATTERNS)
NUM_BUCKETS = 32
MAX_DISTANCE = 2048
RMS_EPS = 1e-6

LANES = 128
SUBLANES = 8
BF16_ROWS = 16
SWA_Q_SCALE = SWA_HEAD_DIM ** -0.5 * math.log2(math.e)
SWA_TASK_UNROLL = 8
VMEM_LIMIT_BYTES = 56 * 1024 * 1024

NT_DIMS = (((1,), (1,)), ((), ()))
TN_DIMS = (((0,), (0,)), ((), ()))


def _rms(x, g):
    return x * lax.rsqrt(jnp.mean(x * x, axis=-1, keepdims=True) + RMS_EPS) * g


def _silu(x):
    return x * jax.nn.sigmoid(x)


def _softplus(x):
    return jnp.maximum(x, 0.0) + jnp.log1p(jnp.exp(-jnp.abs(x)))


def _resident(shape):
    zeros = (0,) * len(shape)
    return pl.BlockSpec(shape, lambda *_: zeros, pipeline_mode=pl.Buffered(1))


def _inproj_kernel(x_ref, g_ref, wqkv_ref, wgate_ref, wq_ref, wk_ref, wv_ref, ws_ref,
                   convw_ref, alog_ref, dtb_ref,
                   qkv_ref, gate_ref, qb_ref, kb_ref, vb_ref, bg_ref,
                   stage_scr, stage2_scr, ext_scr, *, n_sub):
    tm = x_ref.shape[0]
    hd = GDN_HEAD_DIM
    n_groups = 3 * GDN_WIDTH // hd
    halo = SUBLANES
    sub = tm // n_sub
    rows = [slice(i * sub, (i + 1) * sub) for i in range(n_sub)]
    hs = [_rms(x_ref[r, :], g_ref[...]).astype(BF16) for r in rows]

    at = lambda first, n: pl.ds(CONV_ROW_PITCH * first, n, stride=CONV_ROW_PITCH)

    @pl.when(pl.program_id(1) == 0)
    def _():
        for j in range(n_groups):
            ext_scr[j, at(0, halo), :] = jnp.zeros((halo, hd), F32)

    @pl.when(pl.program_id(1) > 0)
    def _():
        for j in range(n_groups):
            ext_scr[j, at(0, halo), :] = ext_scr[j, at(tm, halo), :]

    pres = [jnp.dot(h, wqkv_ref[...], preferred_element_type=F32) for h in hs]
    for i, pre in enumerate(pres):
        for j in range(n_groups):
            ext_scr[j, at(halo + i * sub, sub), :] = pre[:, j * hd:(j + 1) * hd]

    def conv_act(i):
        for j in range(n_groups):
            cols = slice(j * hd, (j + 1) * hd)
            acc = None
            for tap in range(CONV_WIDTH):
                first = halo + i * sub - (CONV_WIDTH - 1) + tap
                term = convw_ref[tap:tap + 1, cols] * ext_scr[j, at(first, sub), :]
                acc = term if acc is None else acc + term
            y = _silu(acc)
            kind = j // GDN_HEADS
            if kind < 2:
                y = y * lax.rsqrt(jnp.sum(y * y, axis=-1, keepdims=True) + 1e-6)
            if kind == 0:
                y = y * (hd ** -0.5)
            qkv_ref[rows[i], cols] = y

    n_res = SWA_RESIDUES
    f = math.isqrt(n_res)
    assert f * f == n_res
    attn_proj = ((wq_ref, qb_ref, SWA_Q_SCALE), (wk_ref, kb_ref, None), (wv_ref, vb_ref, None))

    def split(i, t, o_ref):
        r0 = i * sub
        for slab in range(SWA_WIDTH // LANES):
            for a in range(f):
                stage2_scr[t, slab, a, r0 // f:(r0 + sub) // f, :] = (
                    stage_scr[t, slab, pl.ds(r0 + a, sub // f, stride=f), :])
        for slab in range(SWA_WIDTH // LANES):
            for a in range(f):
                for b in range(f):
                    o_ref[slab, f * b + a, r0 // n_res:(r0 + sub) // n_res, :] = stage2_scr[
                        t, slab, a, pl.ds(r0 // f + b, sub // n_res, stride=f), :].astype(o_ref.dtype)

    for i, h in enumerate(hs):
        gate_ref[rows[i], :] = _silu(jnp.dot(h, wgate_ref[...], preferred_element_type=F32))
        sm = jnp.dot(h, ws_ref[...], preferred_element_type=F32)
        is_beta = lax.broadcasted_iota(jnp.int32, sm.shape, 1) < GDN_HEADS
        bg_ref[rows[i], :] = jnp.where(is_beta, jax.nn.sigmoid(sm),
                                       -jnp.exp(alog_ref[...]) * _softplus(sm + dtb_ref[...]))
        for t, (w_ref, _, scale) in enumerate(attn_proj):
            res = jnp.dot(h, w_ref[...], preferred_element_type=F32)
            if scale is not None:
                res = res * scale
            for slab in range(SWA_WIDTH // LANES):
                stage_scr[t, slab, rows[i], :] = res[:, slab * LANES:(slab + 1) * LANES]
        conv_act(i)
        for t, (_, o_ref, _) in enumerate(attn_proj):
            split(i, t, o_ref)


def _inproj(x2d, g, wqkv, wgate, wq, wk, wv, ws, conv_w, alog_row, dtb_row, batch, seq, tm):
    T, D = x2d.shape
    nt = seq // tm
    row = lambda n: pl.BlockSpec((tm, n), lambda b, j: (b * nt + j, 0))
    n_slabs = SWA_WIDTH // LANES
    slabs = pl.BlockSpec((n_slabs, None, SWA_RESIDUES, tm // SWA_RESIDUES, LANES),
                         lambda b, j: (0, b, 0, j, 0))
    slab_shape = jax.ShapeDtypeStruct(
        (n_slabs, batch, SWA_RESIDUES, seq // SWA_RESIDUES, LANES), BF16)
    return pl.pallas_call(
        functools.partial(_inproj_kernel, n_sub=2),
        grid=(batch, nt),
        in_specs=([row(D), _resident((1, D))]
                  + [_resident(w.shape) for w in (wqkv, wgate, wq, wk, wv, ws, conv_w)]
                  + [_resident((1, LANES))] * 2),
        out_specs=[row(3 * GDN_WIDTH), row(GDN_WIDTH), slabs, slabs, slabs, row(LANES)],
        out_shape=[jax.ShapeDtypeStruct((T, 3 * GDN_WIDTH), F32),
                   jax.ShapeDtypeStruct((T, GDN_WIDTH), F32), slab_shape, slab_shape, slab_shape,
                   jax.ShapeDtypeStruct((T, LANES), F32)],
        scratch_shapes=[pltpu.VMEM((3, n_slabs, tm, LANES), F32),
                        pltpu.VMEM((3, n_slabs, math.isqrt(SWA_RESIDUES),
                                    tm // math.isqrt(SWA_RESIDUES), LANES), F32),
                        pltpu.VMEM((3 * GDN_WIDTH // GDN_HEAD_DIM, CONV_ROW_PITCH * (tm + SUBLANES),
                                    GDN_HEAD_DIM), F32)],
        compiler_params=pltpu.CompilerParams(dimension_semantics=("arbitrary", "arbitrary"),
                                             vmem_limit_bytes=VMEM_LIMIT_BYTES),
        name="inproj",
    )(x2d, g, wqkv, wgate, wq, wk, wv, ws, conv_w, alog_row, dtb_row)


def _unit_lower_inverses(mats, eye, xor_ij):
    n = mats[0].shape[0]
    ds = [eye - jnp.where(xor_ij < 2, a, 0.0) for a in mats]
    s = 2
    while s < n:
        band = jnp.logical_and(xor_ij >= s, xor_ij < 2 * s)
        dbs = [d.astype(BF16) for d in ds]
        eds = [jnp.dot(jnp.where(band, a, 0.0).astype(BF16), db,
                       preferred_element_type=F32).astype(BF16) for a, db in zip(mats, dbs)]
        ds = [d - jnp.dot(db, ed, preferred_element_type=F32) for d, db, ed in zip(ds, dbs, eds)]
        s *= 2
    return ds


def _gdn_kernel(qkv_ref, gate_ref, bg_ref, ong_ref, ltri_ref, o_ref, state_scr, *, tb, chunk):
    hd = GDN_HEAD_DIM

    @pl.when(pl.program_id(1) == 0)
    def _():
        state_scr[...] = jnp.zeros_like(state_scr)

    ii = lax.broadcasted_iota(jnp.int32, (chunk, chunk), 0)
    jj = lax.broadcasted_iota(jnp.int32, (chunk, chunk), 1)
    xor_ij = jnp.bitwise_xor(ii, jj)
    lower = ii >= jj
    strict = ii > jj
    eye = jnp.where(ii == jj, 1.0, 0.0).astype(F32)
    ltri = ltri_ref[...]
    ong = ong_ref[...]

    nc = tb // chunk
    heads = range(GDN_HEADS)
    rows = [slice(c * chunk, (c + 1) * chunk) for c in range(nc)]
    hcols = [slice(h * hd, (h + 1) * hd) for h in heads]
    qcols, kcols, vcols = ([slice(part * GDN_WIDTH + h * hd, part * GDN_WIDTH + (h + 1) * hd)
                            for h in heads] for part in range(3))
    seqs = range(qkv_ref.shape[0])
    probs = [(s, c, h) for s in seqs for c in range(nc) for h in heads]

    gcs = {}
    for s in seqs:
        for c in range(nc):
            g = bg_ref[s, rows[c], :]
            g_hi = g.astype(BF16)
            rem = g - g_hi.astype(F32)
            g_mid = rem.astype(BF16)
            g_lo = (rem - g_mid.astype(F32)).astype(BF16)
            gcs[s, c] = (jnp.dot(ltri, g_hi, preferred_element_type=F32)
                         + jnp.dot(ltri, g_mid, preferred_element_type=F32)
                         + jnp.dot(ltri, g_lo, preferred_element_type=F32))
    gcts = {sc: gc.T for sc, gc in gcs.items()}
    lane_of = lambda h: slice(GDN_HEADS + h, GDN_HEADS + h + 1)
    gcol = {(s, c, h): gcs[s, c][:, lane_of(h)] for s, c, h in probs}
    glast = {(s, c, h): gcs[s, c][chunk - 1:chunk, lane_of(h)] for s, c, h in probs}
    bcol = {(s, c, h): bg_ref[s, rows[c], h:h + 1] for s, c, h in probs}

    qks = {}
    for s, c, h in probs:
        kcb = qkv_ref[s, rows[c], kcols[h]].astype(BF16)
        qkb = jnp.concatenate([qkv_ref[s, rows[c], qcols[h]].astype(BF16), kcb], axis=0)
        qks[s, c, h] = lax.dot_general(qkb, kcb, NT_DIMS, preferred_element_type=F32)
    attn, a_mats = {}, []
    for s, c, h in probs:
        grow = gcts[s, c][lane_of(h), :]
        decay = jnp.exp(jnp.where(lower, gcol[s, c, h] - grow, -jnp.inf))
        attn[s, c, h] = (qks[s, c, h][:chunk] * decay).astype(BF16)
        a_mats.append(jnp.where(strict, qks[s, c, h][chunk:] * decay * bcol[s, c, h], 0.0))
    t_invs = _unit_lower_inverses(a_mats, eye, xor_ij)

    uws = {}
    for (s, c, h), t_inv in zip(probs, t_invs):
        kc = qkv_ref[s, rows[c], kcols[h]]
        eg = jnp.exp(gcol[s, c, h])
        rhs = jnp.concatenate([qkv_ref[s, rows[c], vcols[h]] * bcol[s, c, h],
                               kc * (bcol[s, c, h] * eg)], axis=1).astype(BF16)
        uws[s, c, h] = jnp.dot(t_inv.astype(BF16), rhs, preferred_element_type=F32).astype(BF16)
    kd_uw, gq, attn_u = {}, {}, {}
    for s, c, h in probs:
        k_dec = qkv_ref[s, rows[c], kcols[h]] * jnp.exp(glast[s, c, h] - gcol[s, c, h])
        kd_uw[s, c, h] = lax.dot_general(k_dec.astype(BF16), uws[s, c, h], TN_DIMS,
                                         preferred_element_type=F32)
    for s, c, h in probs:
        a_uw = jnp.dot(attn[s, c, h], uws[s, c, h], preferred_element_type=F32)
        q_eff = qkv_ref[s, rows[c], qcols[h]] * jnp.exp(gcol[s, c, h]) - a_uw[:, hd:]
        gq[s, c, h] = jnp.concatenate([kd_uw[s, c, h][:, hd:], q_eff], axis=0).astype(BF16)
        attn_u[s, c, h] = a_uw[:, :hd]

    chains = [(s, h) for s in seqs for h in heads]
    state = {sh: state_scr[sh] for sh in chains}
    for c in range(nc):
        gs = {(s, h): jnp.dot(gq[s, c, h], state[s, h].astype(BF16), preferred_element_type=F32)
              for s, h in chains}
        state = {(s, h): (state[s, h] * jnp.exp(glast[s, c, h]) - gs[s, h][:hd]
                          + kd_uw[s, c, h][:, :hd]) for s, h in chains}
        for s, h in chains:
            o = gs[s, h][hd:] + attn_u[s, c, h]
            gated = _rms(o, ong) * gate_ref[s, rows[c], hcols[h]]
            o_ref[s, rows[c], hcols[h]] = gated.astype(o_ref.dtype)
    for sh in chains:
        state_scr[sh] = state[sh]


def _gdn(qkv, gate, bg, onorm_g, batch, seq, tb, n_seq):
    idx = jnp.arange(CHUNK)
    ltri = (idx[:, None] >= idx[None, :]).astype(BF16)
    blk = lambda n: pl.BlockSpec((n_seq, tb, n), lambda b, t: (b, t, 0))
    per_seq = lambda a: a.reshape(batch, seq, a.shape[-1])
    kern = functools.partial(_gdn_kernel, tb=tb, chunk=CHUNK)
    out = pl.pallas_call(
        kern,
        grid=(batch // n_seq, seq // tb),
        in_specs=[blk(3 * GDN_WIDTH), blk(GDN_WIDTH), blk(LANES), _resident((1, GDN_HEAD_DIM)),
                  _resident((CHUNK, CHUNK))],
        out_specs=blk(GDN_WIDTH),
        out_shape=jax.ShapeDtypeStruct((batch, seq, GDN_WIDTH), BF16),
        scratch_shapes=[pltpu.VMEM((n_seq, GDN_HEADS, GDN_HEAD_DIM, GDN_HEAD_DIM), F32)],
        compiler_params=pltpu.CompilerParams(dimension_semantics=("arbitrary", "arbitrary"),
                                             vmem_limit_bytes=VMEM_LIMIT_BYTES),
        name="gdn",
    )(per_seq(qkv), per_seq(gate), per_seq(bg), onorm_g, ltri)
    return out.reshape(batch * seq, GDN_WIDTH)


def _swa_kernel(q_ref, k_ref, v_ref, bias_ref, sel_ref, o_ref, acc_scr, m_scr, s_scr, out_scr,
                *, w):
    n_res = SWA_RESIDUES
    tile = pl.program_id(1)
    n_slabs = SWA_WIDTH // LANES
    heads_per_slab = LANES // SWA_HEAD_DIM
    assert heads_per_slab == 2
    tasks_per_call = lambda dil: 2 if (w * dil // n_res) % BF16_ROWS else 1
    lane = lax.broadcasted_iota(jnp.int32, (w, LANES), 1)
    low_half = lane < SWA_HEAD_DIM

    def expand(packed):
        hi = packed.astype(BF16)
        lo = (packed - hi.astype(F32)).astype(BF16)
        return jnp.dot(jnp.concatenate([hi, lo], axis=1), sel_ref[...], preferred_element_type=F32)

    def task(pat, dil, t, sub):
        n_chunks = n_res // dil
        c = w // n_chunks
        per_call = tasks_per_call(dil)
        t = t * per_call + sub
        res_d = jnp.bitwise_and(t, dil - 1)
        blk = jnp.right_shift(t, dil.bit_length() - 1)
        n_glob = tile * n_chunks + blk
        q0 = pl.multiple_of(blk * c, c)
        cur0 = pl.multiple_of(n_glob * c, c)
        prev0 = pl.multiple_of(jnp.maximum(n_glob - 1, 0) * c, c)
        is_first_blk = (n_glob == 0).astype(jnp.int32)
        res_of = [a * dil + res_d for a in range(n_chunks)]

        def gather(ref, lead, start):
            return jnp.concatenate([ref[lead + (res, pl.ds(start, c), slice(None))]
                                    for res in res_of], axis=0)

        def operand(ref, slab, start, shift):
            if c % BF16_ROWS == 0:
                return gather(ref, (slab,), start)
            assert BF16_ROWS == 2 * c and dil == 1 and per_call % 2 == 0
            half = (sub + shift) % 2
            tile0 = pl.multiple_of(jnp.maximum(start - half * c, 0), BF16_ROWS)
            return jnp.concatenate(
                [ref[slab, res, pl.ds(tile0, BF16_ROWS), :].astype(F32)[half * c:(half + 1) * c]
                 for res in res_of], axis=0).astype(BF16)

        logits, vps = [], []
        for slab in range(n_slabs):
            qp = operand(q_ref, slab, q0, 0)
            kp = jnp.concatenate([operand(k_ref, slab, prev0, -1), operand(k_ref, slab, cur0, 0)],
                                 axis=0)
            vps.append(jnp.concatenate([operand(v_ref, slab, prev0, -1),
                                        operand(v_ref, slab, cur0, 0)], axis=0))
            for e in range(heads_per_slab):
                mine = low_half if e == 0 else jnp.logical_not(low_half)
                qh = jnp.where(mine, qp, jnp.zeros_like(qp))
                lg = lax.dot_general(qh, kp, NT_DIMS, preferred_element_type=F32)
                logits.append(lg + bias_ref[pat, is_first_blk, slab * heads_per_slab + e])
        m_blk = jnp.zeros((w, LANES), F32)
        s_blk = jnp.ones((w, LANES), F32)
        ps = []
        for h, lg in enumerate(logits):
            m_h = jnp.max(lg, axis=-1, keepdims=True)
            p = jnp.exp2(lg - m_h)
            ps.append(p.astype(BF16))
            m_blk = jnp.where(lane == h, m_h, m_blk)
            s_blk = jnp.where(lane == h, jnp.sum(p, axis=-1, keepdims=True), s_blk)
        nums = []
        for slab in range(n_slabs):
            h0 = slab * heads_per_slab
            outs = [jnp.dot(ps[h0 + e], vps[slab], preferred_element_type=F32)
                    for e in range(heads_per_slab)]
            nums.append(jnp.where(low_half, outs[0], outs[1]))
        for idx, res in enumerate(res_of):
            rows, part = pl.ds(q0, c), slice(idx * c, (idx + 1) * c)
            for slab in range(n_slabs):
                acc_scr[pat, slab, res, rows, :] = nums[slab][part]
            m_scr[pat, res, rows, :] = m_blk[part]
            s_scr[pat, res, rows, :] = s_blk[part]

    n_tasks = n_res
    for pat, (window, dil) in enumerate(DILATED_PATTERNS):
        assert window // dil == w and n_res % dil == 0 and (w * dil) % n_res == 0

        per_call = tasks_per_call(dil)

        def body(t, carry, pat=pat, dil=dil, per_call=per_call):
            for sub in range(per_call):
                task(pat, dil, t, sub)
            return carry
        lax.fori_loop(0, n_tasks // per_call, body, 0, unroll=SWA_TASK_UNROLL // per_call)

    n_pat = len(DILATED_PATTERNS)

    def finish(r, carry):
        ms = [m_scr[p, r] for p in range(n_pat)]
        m_max = functools.reduce(jnp.maximum, ms)
        wts = [jnp.exp2(m - m_max) for m in ms]
        den = functools.reduce(jnp.add, [wt * s_scr[p, r] for p, wt in enumerate(wts)])
        coefs = [expand(wt / den) for wt in wts]
        for slab in range(n_slabs):
            cols = slice(slab * LANES, (slab + 1) * LANES)
            out_scr[slab, pl.ds(r, w, stride=n_res), :] = functools.reduce(
                jnp.add, [coefs[p][:, cols] * acc_scr[p, slab, r] for p in range(n_pat)])
        return carry
    lax.fori_loop(0, n_res, finish, 0, unroll=4)
    for slab in range(n_slabs):
        o_ref[:, slab * LANES:(slab + 1) * LANES] = out_scr[slab].astype(o_ref.dtype)


def _swa(qb, kb, vb, bias, batch, seq):
    n_slabs = SWA_WIDTH // LANES
    n_res = SWA_RESIDUES
    n_pat = len(DILATED_PATTERNS)
    w = DILATED_PATTERNS[0][0] // DILATED_PATTERNS[0][1]
    tq = n_res * w
    nq = seq // tq
    q_blk = pl.BlockSpec((n_slabs, None, n_res, w, LANES), lambda b, i: (0, b, 0, i, 0))
    seq_blk = pl.BlockSpec((n_slabs, None, n_res, seq // n_res, LANES), lambda b, i: (0, b, 0, 0, 0))
    head_of_col = jnp.arange(SWA_WIDTH) // SWA_HEAD_DIM
    sel = (jnp.arange(LANES)[:, None] == head_of_col[None, :]).astype(BF16)
    sel = jnp.concatenate([sel, sel], axis=0)
    return pl.pallas_call(
        functools.partial(_swa_kernel, w=w),
        grid=(batch, nq),
        in_specs=[q_blk, seq_blk, seq_blk, _resident(bias.shape), _resident(sel.shape)],
        out_specs=pl.BlockSpec((tq, SWA_WIDTH), lambda b, i: (b * nq + i, 0)),
        out_shape=jax.ShapeDtypeStruct((batch * seq, SWA_WIDTH), BF16),
        scratch_shapes=[pltpu.VMEM((n_pat, n_slabs, n_res, w, LANES), F32),
                        pltpu.VMEM((n_pat, n_res, w, LANES), F32),
                        pltpu.VMEM((n_pat, n_res, w, LANES), F32),
                        pltpu.VMEM((n_slabs, tq, LANES), F32)],
        compiler_params=pltpu.CompilerParams(dimension_semantics=("arbitrary", "arbitrary"),
                                             vmem_limit_bytes=VMEM_LIMIT_BYTES),
        name="swa",
    )(qb, kb, vb, bias, sel)


def _t5_causal_bucket(dist):
    max_exact = NUM_BUCKETS // 2
    d = jnp.maximum(dist, 1).astype(F32)
    log_b = max_exact + (jnp.log(d / max_exact) / math.log(MAX_DISTANCE / max_exact)
                         * (NUM_BUCKETS - max_exact)).astype(jnp.int32)
    return jnp.where(dist < max_exact, dist, jnp.minimum(log_b, NUM_BUCKETS - 1))


def _swa_bias_tables(rel_bias):
    w = DILATED_PATTERNS[0][0] // DILATED_PATTERNS[0][1]
    rels, steps = [], []
    for window, dil in DILATED_PATTERNS:
        n_chunks = SWA_RESIDUES // dil
        c = w // n_chunks
        pos = (np.arange(w) % c) * n_chunks + np.arange(w) // c
        rels.append(pos[:, None] + w - np.concatenate([pos, w + pos])[None, :])
        steps.append(rel_bias[_t5_causal_bucket(jnp.arange(w + 1) * dil)].astype(F32))
    rel = np.stack(rels)
    one_hot = (jnp.asarray(np.clip(rel, 0, w))[..., None] == jnp.arange(w + 1)).astype(F32)
    table = jnp.einsum("prh,pijr->phij", jnp.stack(steps), one_hot,
                       precision=lax.Precision.HIGHEST) * math.log2(math.e)
    in_band = np.logical_and(rel >= 0, rel <= w)[:, None]
    no_prev = np.arange(2 * w) >= w
    keep = np.stack([in_band, np.logical_and(in_band, no_prev)], axis=1)
    return jnp.where(jnp.asarray(keep), table[:, None], -jnp.inf)


def _mixffn_kernel(x_ref, oa_ref, ob_ref, woa_ref, wob_ref, gpost_ref, gpre_ref, gfpost_ref,
                   wg_ref, wu_ref, wd_ref, out_ref, act_scr, x1_scr, *, ff_chunk, n_sub):
    sub = x_ref.shape[0] // n_sub
    rows = [slice(i * sub, (i + 1) * sub) for i in range(n_sub)]
    d_ff = wg_ref.shape[1]
    mixes = [jnp.dot(oa_ref[r, :], woa_ref[...], preferred_element_type=F32)
             + jnp.dot(ob_ref[r, :], wob_ref[...], preferred_element_type=F32) for r in rows]
    for r, mix in zip(rows, mixes):
        x1 = x_ref[r, :] + _rms(mix, gpost_ref[...])
        x1_scr[r, :] = x1
        h = _rms(x1, gpre_ref[...]).astype(BF16)
        for c in range(d_ff // ff_chunk):
            cols = slice(c * ff_chunk, (c + 1) * ff_chunk)
            gate = jnp.dot(h, wg_ref[:, cols], preferred_element_type=F32)
            up = jnp.dot(h, wu_ref[:, cols], preferred_element_type=F32)
            act_scr[r, cols] = (_silu(gate) * up).astype(BF16)
    fs = [jnp.dot(act_scr[r, :], wd_ref[...], preferred_element_type=F32) for r in rows]
    for r, f in zip(rows, fs):
        out_ref[r, :] = x1_scr[r, :] + _rms(f, gfpost_ref[...])


def _mixffn(x2d, oa, ob, woa, wob, gpost, gpre, gfpost, wg, wu, wd, tm, sub, ff_chunk):
    T, D = x2d.shape
    d_ff = wg.shape[1]
    row = lambda n: pl.BlockSpec((tm, n), lambda i: (i, 0))
    return pl.pallas_call(
        functools.partial(_mixffn_kernel, ff_chunk=ff_chunk, n_sub=tm // sub),
        grid=(T // tm,),
        in_specs=([row(D), row(GDN_WIDTH), row(SWA_WIDTH)]
                  + [_resident(woa.shape), _resident(wob.shape)] + [_resident((1, D))] * 3
                  + [_resident(wg.shape), _resident(wu.shape), _resident(wd.shape)]),
        out_specs=row(D),
        out_shape=jax.ShapeDtypeStruct((T, D), F32),
        scratch_shapes=[pltpu.VMEM((tm, d_ff), BF16), pltpu.VMEM((tm, D), F32)],
        compiler_params=pltpu.CompilerParams(dimension_semantics=("arbitrary",),
                                             vmem_limit_bytes=VMEM_LIMIT_BYTES),
        name="mixffn",
    )(x2d, oa, ob, woa, wob, gpost, gpre, gfpost, wg, wu, wd)


def _layer(x2d, batch, seq, w_in, conv_w, a_log, dt_bias, onorm_g, rel_bias, w_out,
           g_mix_pre, g_mix_post, w_gate, w_up, w_down, g_ffn_pre, g_ffn_post):
    D = x2d.shape[1]
    gw, sw, nh = GDN_WIDTH, SWA_WIDTH, GDN_HEADS
    wb = w_in.astype(BF16)
    c_gate, c_small, c_q = 3 * gw, 4 * gw, 4 * gw + 2 * nh
    w_small = jnp.pad(wb[:, c_small:c_q], ((0, 0), (0, LANES - 2 * nh)))
    lane_pad = lambda v: jnp.pad(v.astype(F32).reshape(1, nh), ((0, 0), (nh, LANES - 2 * nh)))
    qkv, gate, qb, kb, vb, bg = _inproj(
        x2d, g_mix_pre.reshape(1, D), wb[:, :c_gate], wb[:, c_gate:c_small],
        wb[:, c_q:c_q + sw], wb[:, c_q + sw:c_q + 2 * sw], wb[:, c_q + 2 * sw:], w_small,
        conv_w.astype(F32), lane_pad(a_log), lane_pad(dt_bias), batch, seq, tm=512)

    oa = _gdn(qkv, gate, bg, onorm_g.astype(F32).reshape(1, GDN_HEAD_DIM), batch, seq, tb=256, n_seq=2)

    ob = _swa(qb, kb, vb, _swa_bias_tables(rel_bias), batch, seq)

    wo = w_out.astype(BF16)
    return _mixffn(x2d, oa, ob, wo[:gw], wo[gw:], g_mix_post.reshape(1, D),
                   g_ffn_pre.reshape(1, D), g_ffn_post.reshape(1, D), w_gate.astype(BF16),
                   w_up.astype(BF16), w_down.astype(BF16), tm=1024, sub=256, ff_chunk=256)


def kernel(x, w_in, conv_w, a_log, dt_bias, onorm_g, rel_bias, w_out, g_mix_pre, g_mix_post,
           w_gate, w_up, w_down, g_ffn_pre, g_ffn_post):
    batch, seq, d_model = x.shape
    x2d = x.reshape(batch * seq, d_model)
    for l in range(w_in.shape[0]):
        x2d = _layer(x2d, batch, seq, w_in[l], conv_w[l], a_log[l], dt_bias[l], onorm_g[l],
                     rel_bias, w_out[l], g_mix_pre[l], g_mix_post[l], w_gate[l], w_up[l],
                     w_down[l], g_ffn_pre[l], g_ffn_post[l])
    return x2d.reshape(batch, seq, d_model)
```

```python
import functools
import math

import jax
import jax.numpy as jnp
import numpy as np
from jax import lax
from jax.experimental import pallas as pl
from jax.experimental.pallas import tpu as pltpu

F32 = jnp.float32
BF16 = jnp.bfloat16

GDN_HEADS = 4
GDN_HEAD_DIM = 128
GDN_WIDTH = GDN_HEADS * GDN_HEAD_DIM
CONV_WIDTH = 4
CONV_ROW_PITCH = 2
CHUNK = 64
SWA_HEADS = 8
SWA_HEAD_DIM = 64
SWA_WIDTH = SWA_HEADS * SWA_HEAD_DIM
DILATED_PATTERNS = ((128, 1), (512, 4), (2048, 16))
SWA_RESIDUES = max(dil for _, dil in DILATED_PATTERNS)
NUM_BUCKETS = 32
MAX_DISTANCE = 2048
RMS_EPS = 1e-6

LANES = 128
SUBLANES = 8
BF16_ROWS = 16
SWA_Q_SCALE = SWA_HEAD_DIM ** -0.5 * math.log2(math.e)
SWA_TASK_UNROLL = 8

INPROJ_ROWS, INPROJ_SUBTILES = 512, 2
GDN_ROWS, GDN_SEQS = 256, 2
FFN_ROWS, FFN_SUB_ROWS, FFN_COL_CHUNK = 1024, 256, 256
VMEM_LIMIT_BYTES = 56 * 1024 * 1024

NT_DIMS = (((1,), (1,)), ((), ()))
TN_DIMS = (((0,), (0,)), ((), ()))


def _rms(x, g):
    return x * lax.rsqrt(jnp.mean(x * x, axis=-1, keepdims=True) + RMS_EPS) * g


def _silu(x):
    return x * jax.nn.sigmoid(x)


def _softplus(x):
    return jnp.maximum(x, 0.0) + jnp.log1p(jnp.exp(-jnp.abs(x)))


def _resident(shape):
    zeros = (0,) * len(shape)
    return pl.BlockSpec(shape, lambda *_: zeros, pipeline_mode=pl.Buffered(1))


def _inproj_kernel(x_ref, g_ref, wqkv_ref, wgate_ref, wq_ref, wk_ref, wv_ref, ws_ref,
                   convw_ref, alog_ref, dtb_ref,
                   qkv_ref, gate_ref, qb_ref, kb_ref, vb_ref, bg_ref,
                   stage_scr, stage2_scr, ext_scr, *, n_sub):
    tm = x_ref.shape[0]
    hd = GDN_HEAD_DIM
    n_groups = 3 * GDN_WIDTH // hd
    halo = SUBLANES
    sub = tm // n_sub
    rows = [slice(i * sub, (i + 1) * sub) for i in range(n_sub)]
    hs = [_rms(x_ref[r, :], g_ref[...]).astype(BF16) for r in rows]

    at = lambda first, n: pl.ds(CONV_ROW_PITCH * first, n, stride=CONV_ROW_PITCH)

    @pl.when(pl.program_id(1) == 0)
    def _():
        for j in range(n_groups):
            ext_scr[j, at(0, halo), :] = jnp.zeros((halo, hd), F32)

    @pl.when(pl.program_id(1) > 0)
    def _():
        for j in range(n_groups):
            ext_scr[j, at(0, halo), :] = ext_scr[j, at(tm, halo), :]

    pres = [jnp.dot(h, wqkv_ref[...], preferred_element_type=F32) for h in hs]
    for i, pre in enumerate(pres):
        for j in range(n_groups):
            ext_scr[j, at(halo + i * sub, sub), :] = pre[:, j * hd:(j + 1) * hd]

    def conv_act(i):
        for j in range(n_groups):
            cols = slice(j * hd, (j + 1) * hd)
            acc = None
            for tap in range(CONV_WIDTH):
                first = halo + i * sub - (CONV_WIDTH - 1) + tap
                term = convw_ref[tap:tap + 1, cols] * ext_scr[j, at(first, sub), :]
                acc = term if acc is None else acc + term
            y = _silu(acc)
            kind = j // GDN_HEADS
            if kind < 2:
                y = y * lax.rsqrt(jnp.sum(y * y, axis=-1, keepdims=True) + 1e-6)
            if kind == 0:
                y = y * (hd ** -0.5)
            qkv_ref[rows[i], cols] = y

    n_res = SWA_RESIDUES
    f = math.isqrt(n_res)
    assert f * f == n_res
    attn_proj = ((wq_ref, qb_ref, SWA_Q_SCALE), (wk_ref, kb_ref, None), (wv_ref, vb_ref, None))

    def split(i, t, o_ref):
        r0 = i * sub
        for slab in range(SWA_WIDTH // LANES):
            for a in range(f):
                stage2_scr[t, slab, a, r0 // f:(r0 + sub) // f, :] = (
                    stage_scr[t, slab, pl.ds(r0 + a, sub // f, stride=f), :])
        for slab in range(SWA_WIDTH // LANES):
            for a in range(f):
                for b in range(f):
                    o_ref[slab, f * b + a, r0 // n_res:(r0 + sub) // n_res, :] = stage2_scr[
                        t, slab, a, pl.ds(r0 // f + b, sub // n_res, stride=f), :].astype(o_ref.dtype)

    for i, h in enumerate(hs):
        gate_ref[rows[i], :] = _silu(jnp.dot(h, wgate_ref[...], preferred_element_type=F32))
        sm = jnp.dot(h, ws_ref[...], preferred_element_type=F32)
        is_beta = lax.broadcasted_iota(jnp.int32, sm.shape, 1) < GDN_HEADS
        bg_ref[rows[i], :] = jnp.where(is_beta, jax.nn.sigmoid(sm),
                                       -jnp.exp(alog_ref[...]) * _softplus(sm + dtb_ref[...]))
        for t, (w_ref, _, scale) in enumerate(attn_proj):
            res = jnp.dot(h, w_ref[...], preferred_element_type=F32)
            if scale is not None:
                res = res * scale
            for slab in range(SWA_WIDTH // LANES):
                stage_scr[t, slab, rows[i], :] = res[:, slab * LANES:(slab + 1) * LANES]
        conv_act(i)
        for t, (_, o_ref, _) in enumerate(attn_proj):
            split(i, t, o_ref)


def _inproj(x2d, g, wqkv, wgate, wq, wk, wv, ws, conv_w, alog_row, dtb_row, batch, seq, tm):
    T, D = x2d.shape
    nt = seq // tm
    row = lambda n: pl.BlockSpec((tm, n), lambda b, j: (b * nt + j, 0))
    n_slabs = SWA_WIDTH // LANES
    slabs = pl.BlockSpec((n_slabs, None, SWA_RESIDUES, tm // SWA_RESIDUES, LANES),
                         lambda b, j: (0, b, 0, j, 0))
    slab_shape = jax.ShapeDtypeStruct(
        (n_slabs, batch, SWA_RESIDUES, seq // SWA_RESIDUES, LANES), BF16)
    return pl.pallas_call(
        functools.partial(_inproj_kernel, n_sub=INPROJ_SUBTILES),
        grid=(batch, nt),
        in_specs=([row(D), _resident((1, D))]
                  + [_resident(w.shape) for w in (wqkv, wgate, wq, wk, wv, ws, conv_w)]
                  + [_resident((1, LANES))] * 2),
        out_specs=[row(3 * GDN_WIDTH), row(GDN_WIDTH), slabs, slabs, slabs, row(LANES)],
        out_shape=[jax.ShapeDtypeStruct((T, 3 * GDN_WIDTH), F32),
                   jax.ShapeDtypeStruct((T, GDN_WIDTH), F32), slab_shape, slab_shape, slab_shape,
                   jax.ShapeDtypeStruct((T, LANES), F32)],
        scratch_shapes=[pltpu.VMEM((3, n_slabs, tm, LANES), F32),
                        pltpu.VMEM((3, n_slabs, math.isqrt(SWA_RESIDUES),
                                    tm // math.isqrt(SWA_RESIDUES), LANES), F32),
                        pltpu.VMEM((3 * GDN_WIDTH // GDN_HEAD_DIM, CONV_ROW_PITCH * (tm + SUBLANES),
                                    GDN_HEAD_DIM), F32)],
        compiler_params=pltpu.CompilerParams(dimension_semantics=("arbitrary", "arbitrary"),
                                             vmem_limit_bytes=VMEM_LIMIT_BYTES),
        name="inproj",
    )(x2d, g, wqkv, wgate, wq, wk, wv, ws, conv_w, alog_row, dtb_row)


def _unit_lower_inverses(mats, eye, xor_ij):
    n = mats[0].shape[0]
    ds = [eye - jnp.where(xor_ij < 2, a, 0.0) for a in mats]
    s = 2
    while s < n:
        band = jnp.logical_and(xor_ij >= s, xor_ij < 2 * s)
        dbs = [d.astype(BF16) for d in ds]
        eds = [jnp.dot(jnp.where(band, a, 0.0).astype(BF16), db,
                       preferred_element_type=F32).astype(BF16) for a, db in zip(mats, dbs)]
        ds = [d - jnp.dot(db, ed, preferred_element_type=F32) for d, db, ed in zip(ds, dbs, eds)]
        s *= 2
    return ds


def _gdn_kernel(qkv_ref, gate_ref, bg_ref, ong_ref, ltri_ref, o_ref, state_scr, *, tb, chunk):
    hd = GDN_HEAD_DIM

    @pl.when(pl.program_id(1) == 0)
    def _():
        state_scr[...] = jnp.zeros_like(state_scr)

    ii = lax.broadcasted_iota(jnp.int32, (chunk, chunk), 0)
    jj = lax.broadcasted_iota(jnp.int32, (chunk, chunk), 1)
    xor_ij = jnp.bitwise_xor(ii, jj)
    lower = ii >= jj
    strict = ii > jj
    eye = jnp.where(ii == jj, 1.0, 0.0).astype(F32)
    ltri = ltri_ref[...]
    ong = ong_ref[...]

    nc = tb // chunk
    heads = range(GDN_HEADS)
    rows = [slice(c * chunk, (c + 1) * chunk) for c in range(nc)]
    hcols = [slice(h * hd, (h + 1) * hd) for h in heads]
    qcols, kcols, vcols = ([slice(part * GDN_WIDTH + h * hd, part * GDN_WIDTH + (h + 1) * hd)
                            for h in heads] for part in range(3))
    seqs = range(qkv_ref.shape[0])
    probs = [(s, c, h) for s in seqs for c in range(nc) for h in heads]

    gcs = {}
    for s in seqs:
        for c in range(nc):
            g = bg_ref[s, rows[c], :]
            g_hi = g.astype(BF16)
            rem = g - g_hi.astype(F32)
            g_mid = rem.astype(BF16)
            g_lo = (rem - g_mid.astype(F32)).astype(BF16)
            gcs[s, c] = (jnp.dot(ltri, g_hi, preferred_element_type=F32)
                         + jnp.dot(ltri, g_mid, preferred_element_type=F32)
                         + jnp.dot(ltri, g_lo, preferred_element_type=F32))
    gcts = {sc: gc.T for sc, gc in gcs.items()}
    lane_of = lambda h: slice(GDN_HEADS + h, GDN_HEADS + h + 1)
    gcol = {(s, c, h): gcs[s, c][:, lane_of(h)] for s, c, h in probs}
    glast = {(s, c, h): gcs[s, c][chunk - 1:chunk, lane_of(h)] for s, c, h in probs}
    bcol = {(s, c, h): bg_ref[s, rows[c], h:h + 1] for s, c, h in probs}

    qks = {}
    for s, c, h in probs:
        kcb = qkv_ref[s, rows[c], kcols[h]].astype(BF16)
        qkb = jnp.concatenate([qkv_ref[s, rows[c], qcols[h]].astype(BF16), kcb], axis=0)
        qks[s, c, h] = lax.dot_general(qkb, kcb, NT_DIMS, preferred_element_type=F32)
    attn, a_mats = {}, []
    for s, c, h in probs:
        grow = gcts[s, c][lane_of(h), :]
        decay = jnp.exp(jnp.where(lower, gcol[s, c, h] - grow, -jnp.inf))
        attn[s, c, h] = (qks[s, c, h][:chunk] * decay).astype(BF16)
        a_mats.append(jnp.where(strict, qks[s, c, h][chunk:] * decay * bcol[s, c, h], 0.0))
    t_invs = _unit_lower_inverses(a_mats, eye, xor_ij)

    uws = {}
    for (s, c, h), t_inv in zip(probs, t_invs):
        kc = qkv_ref[s, rows[c], kcols[h]]
        eg = jnp.exp(gcol[s, c, h])
        rhs = jnp.concatenate([qkv_ref[s, rows[c], vcols[h]] * bcol[s, c, h],
                               kc * (bcol[s, c, h] * eg)], axis=1).astype(BF16)
        uws[s, c, h] = jnp.dot(t_inv.astype(BF16), rhs, preferred_element_type=F32).astype(BF16)
    kd_uw, gq, attn_u = {}, {}, {}
    for s, c, h in probs:
        k_dec = qkv_ref[s, rows[c], kcols[h]] * jnp.exp(glast[s, c, h] - gcol[s, c, h])
        kd_uw[s, c, h] = lax.dot_general(k_dec.astype(BF16), uws[s, c, h], TN_DIMS,
                                         preferred_element_type=F32)
    for s, c, h in probs:
        a_uw = jnp.dot(attn[s, c, h], uws[s, c, h], preferred_element_type=F32)
        q_eff = qkv_ref[s, rows[c], qcols[h]] * jnp.exp(gcol[s, c, h]) - a_uw[:, hd:]
        gq[s, c, h] = jnp.concatenate([kd_uw[s, c, h][:, hd:], q_eff], axis=0).astype(BF16)
        attn_u[s, c, h] = a_uw[:, :hd]

    chains = [(s, h) for s in seqs for h in heads]
    state = {sh: state_scr[sh] for sh in chains}
    for c in range(nc):
        gs = {(s, h): jnp.dot(gq[s, c, h], state[s, h].astype(BF16), preferred_element_type=F32)
              for s, h in chains}
        state = {(s, h): (state[s, h] * jnp.exp(glast[s, c, h]) - gs[s, h][:hd]
                          + kd_uw[s, c, h][:, :hd]) for s, h in chains}
        for s, h in chains:
            o = gs[s, h][hd:] + attn_u[s, c, h]
            gated = _rms(o, ong) * gate_ref[s, rows[c], hcols[h]]
            o_ref[s, rows[c], hcols[h]] = gated.astype(o_ref.dtype)
    for sh in chains:
        state_scr[sh] = state[sh]


def _gdn(qkv, gate, bg, onorm_g, batch, seq, tb, n_seq):
    idx = jnp.arange(CHUNK)
    ltri = (idx[:, None] >= idx[None, :]).astype(BF16)
    blk = lambda n: pl.BlockSpec((n_seq, tb, n), lambda b, t: (b, t, 0))
    per_seq = lambda a: a.reshape(batch, seq, a.shape[-1])
    kern = functools.partial(_gdn_kernel, tb=tb, chunk=CHUNK)
    out = pl.pallas_call(
        kern,
        grid=(batch // n_seq, seq // tb),
        in_specs=[blk(3 * GDN_WIDTH), blk(GDN_WIDTH), blk(LANES), _resident((1, GDN_HEAD_DIM)),
                  _resident((CHUNK, CHUNK))],
        out_specs=blk(GDN_WIDTH),
        out_shape=jax.ShapeDtypeStruct((batch, seq, GDN_WIDTH), BF16),
        scratch_shapes=[pltpu.VMEM((n_seq, GDN_HEADS, GDN_HEAD_DIM, GDN_HEAD_DIM), F32)],
        compiler_params=pltpu.CompilerParams(dimension_semantics=("arbitrary", "arbitrary"),
                                             vmem_limit_bytes=VMEM_LIMIT_BYTES),
        name="gdn",
    )(per_seq(qkv), per_seq(gate), per_seq(bg), onorm_g, ltri)
    return out.reshape(batch * seq, GDN_WIDTH)


def _swa_kernel(q_ref, k_ref, v_ref, bias_ref, sel_ref, o_ref, acc_scr, m_scr, s_scr, out_scr,
                *, w):
    n_res = SWA_RESIDUES
    tile = pl.program_id(1)
    n_slabs = SWA_WIDTH // LANES
    heads_per_slab = LANES // SWA_HEAD_DIM
    assert heads_per_slab == 2
    tasks_per_call = lambda dil: 2 if (w * dil // n_res) % BF16_ROWS else 1
    lane = lax.broadcasted_iota(jnp.int32, (w, LANES), 1)
    low_half = lane < SWA_HEAD_DIM

    def expand(packed):
        hi = packed.astype(BF16)
        lo = (packed - hi.astype(F32)).astype(BF16)
        return jnp.dot(jnp.concatenate([hi, lo], axis=1), sel_ref[...], preferred_element_type=F32)

    def task(pat, dil, t, sub):
        n_chunks = n_res // dil
        c = w // n_chunks
        per_call = tasks_per_call(dil)
        t = t * per_call + sub
        res_d = jnp.bitwise_and(t, dil - 1)
        blk = jnp.right_shift(t, dil.bit_length() - 1)
        n_glob = tile * n_chunks + blk
        q0 = pl.multiple_of(blk * c, c)
        cur0 = pl.multiple_of(n_glob * c, c)
        prev0 = pl.multiple_of(jnp.maximum(n_glob - 1, 0) * c, c)
        is_first_blk = (n_glob == 0).astype(jnp.int32)
        res_of = [a * dil + res_d for a in range(n_chunks)]

        def gather(ref, lead, start):
            return jnp.concatenate([ref[lead + (res, pl.ds(start, c), slice(None))]
                                    for res in res_of], axis=0)

        def operand(ref, slab, start, shift):
            if c % BF16_ROWS == 0:
                return gather(ref, (slab,), start)
            assert BF16_ROWS == 2 * c and dil == 1 and per_call % 2 == 0
            half = (sub + shift) % 2
            tile0 = pl.multiple_of(jnp.maximum(start - half * c, 0), BF16_ROWS)
            return jnp.concatenate(
                [ref[slab, res, pl.ds(tile0, BF16_ROWS), :].astype(F32)[half * c:(half + 1) * c]
                 for res in res_of], axis=0).astype(BF16)

        logits, vps = [], []
        for slab in range(n_slabs):
            qp = operand(q_ref, slab, q0, 0)
            kp = jnp.concatenate([operand(k_ref, slab, prev0, -1), operand(k_ref, slab, cur0, 0)],
                                 axis=0)
            vps.append(jnp.concatenate([operand(v_ref, slab, prev0, -1),
                                        operand(v_ref, slab, cur0, 0)], axis=0))
            for e in range(heads_per_slab):
                mine = low_half if e == 0 else jnp.logical_not(low_half)
                qh = jnp.where(mine, qp, jnp.zeros_like(qp))
                lg = lax.dot_general(qh, kp, NT_DIMS, preferred_element_type=F32)
                logits.append(lg + bias_ref[pat, is_first_blk, slab * heads_per_slab + e])
        m_blk = jnp.zeros((w, LANES), F32)
        s_blk = jnp.ones((w, LANES), F32)
        ps = []
        for h, lg in enumerate(logits):
            m_h = jnp.max(lg, axis=-1, keepdims=True)
            p = jnp.exp2(lg - m_h)
            ps.append(p.astype(BF16))
            m_blk = jnp.where(lane == h, m_h, m_blk)
            s_blk = jnp.where(lane == h, jnp.sum(p, axis=-1, keepdims=True), s_blk)
        nums = []
        for slab in range(n_slabs):
            h0 = slab * heads_per_slab
            outs = [jnp.dot(ps[h0 + e], vps[slab], preferred_element_type=F32)
                    for e in range(heads_per_slab)]
            nums.append(jnp.where(low_half, outs[0], outs[1]))
        for idx, res in enumerate(res_of):
            rows, part = pl.ds(q0, c), slice(idx * c, (idx + 1) * c)
            for slab in range(n_slabs):
                acc_scr[pat, slab, res, rows, :] = nums[slab][part]
            m_scr[pat, res, rows, :] = m_blk[part]
            s_scr[pat, res, rows, :] = s_blk[part]

    n_tasks = n_res
    for pat, (window, dil) in enumerate(DILATED_PATTERNS):
        assert window // dil == w and n_res % dil == 0 and (w * dil) % n_res == 0

        per_call = tasks_per_call(dil)

        def body(t, carry, pat=pat, dil=dil, per_call=per_call):
            for sub in range(per_call):
                task(pat, dil, t, sub)
            return carry
        lax.fori_loop(0, n_tasks // per_call, body, 0, unroll=SWA_TASK_UNROLL // per_call)

    n_pat = len(DILATED_PATTERNS)

    def finish(r, carry):
        ms = [m_scr[p, r] for p in range(n_pat)]
        m_max = functools.reduce(jnp.maximum, ms)
        wts = [jnp.exp2(m - m_max) for m in ms]
        den = functools.reduce(jnp.add, [wt * s_scr[p, r] for p, wt in enumerate(wts)])
        coefs = expand(jnp.concatenate([wt / den for wt in wts], axis=0))
        for slab in range(n_slabs):
            cols = slice(slab * LANES, (slab + 1) * LANES)
            out_scr[slab, pl.ds(r, w, stride=n_res), :] = functools.reduce(
                jnp.add, [coefs[p * w:(p + 1) * w, cols] * acc_scr[p, slab, r]
                          for p in range(n_pat)])
        return carry
    lax.fori_loop(0, n_res, finish, 0, unroll=4)
    for slab in range(n_slabs):
        o_ref[:, slab * LANES:(slab + 1) * LANES] = out_scr[slab].astype(o_ref.dtype)


def _swa(qb, kb, vb, bias, batch, seq):
    n_slabs = SWA_WIDTH // LANES
    n_res = SWA_RESIDUES
    n_pat = len(DILATED_PATTERNS)
    w = DILATED_PATTERNS[0][0] // DILATED_PATTERNS[0][1]
    tq = n_res * w
    nq = seq // tq
    q_blk = pl.BlockSpec((n_slabs, None, n_res, w, LANES), lambda b, i: (0, b, 0, i, 0))
    seq_blk = pl.BlockSpec((n_slabs, None, n_res, seq // n_res, LANES), lambda b, i: (0, b, 0, 0, 0))
    head_of_col = jnp.arange(SWA_WIDTH) // SWA_HEAD_DIM
    sel = (jnp.arange(LANES)[:, None] == head_of_col[None, :]).astype(BF16)
    sel = jnp.concatenate([sel, sel], axis=0)
    return pl.pallas_call(
        functools.partial(_swa_kernel, w=w),
        grid=(batch, nq),
        in_specs=[q_blk, seq_blk, seq_blk, _resident(bias.shape), _resident(sel.shape)],
        out_specs=pl.BlockSpec((tq, SWA_WIDTH), lambda b, i: (b * nq + i, 0)),
        out_shape=jax.ShapeDtypeStruct((batch * seq, SWA_WIDTH), BF16),
        scratch_shapes=[pltpu.VMEM((n_pat, n_slabs, n_res, w, LANES), F32),
                        pltpu.VMEM((n_pat, n_res, w, LANES), F32),
                        pltpu.VMEM((n_pat, n_res, w, LANES), F32),
                        pltpu.VMEM((n_slabs, tq, LANES), F32)],
        compiler_params=pltpu.CompilerParams(dimension_semantics=("arbitrary", "arbitrary"),
                                             vmem_limit_bytes=VMEM_LIMIT_BYTES),
        name="swa",
    )(qb, kb, vb, bias, sel)


def _t5_causal_bucket(dist):
    max_exact = NUM_BUCKETS // 2
    d = jnp.maximum(dist, 1).astype(F32)
    log_b = max_exact + (jnp.log(d / max_exact) / math.log(MAX_DISTANCE / max_exact)
                         * (NUM_BUCKETS - max_exact)).astype(jnp.int32)
    return jnp.where(dist < max_exact, dist, jnp.minimum(log_b, NUM_BUCKETS - 1))


def _swa_bias_tables(rel_bias):
    w = DILATED_PATTERNS[0][0] // DILATED_PATTERNS[0][1]
    rels, steps = [], []
    for window, dil in DILATED_PATTERNS:
        n_chunks = SWA_RESIDUES // dil
        c = w // n_chunks
        pos = (np.arange(w) % c) * n_chunks + np.arange(w) // c
        rels.append(pos[:, None] + w - np.concatenate([pos, w + pos])[None, :])
        steps.append(rel_bias[_t5_causal_bucket(jnp.arange(w + 1) * dil)].astype(F32))
    rel = np.stack(rels)
    one_hot = (jnp.asarray(np.clip(rel, 0, w))[..., None] == jnp.arange(w + 1)).astype(F32)
    table = jnp.einsum("prh,pijr->phij", jnp.stack(steps), one_hot,
                       precision=lax.Precision.HIGHEST) * math.log2(math.e)
    in_band = np.logical_and(rel >= 0, rel <= w)[:, None]
    no_prev = np.arange(2 * w) >= w
    keep = np.stack([in_band, np.logical_and(in_band, no_prev)], axis=1)
    return jnp.where(jnp.asarray(keep), table[:, None], -jnp.inf)


def _mixffn_kernel(x_ref, oa_ref, ob_ref, woa_ref, wob_ref, gpost_ref, gpre_ref, gfpost_ref,
                   wg_ref, wu_ref, wd_ref, out_ref, act_scr, x1_scr, *, ff_chunk, n_sub):
    sub = x_ref.shape[0] // n_sub
    rows = [slice(i * sub, (i + 1) * sub) for i in range(n_sub)]
    d_ff = wg_ref.shape[1]
    mixes = [jnp.dot(oa_ref[r, :], woa_ref[...], preferred_element_type=F32)
             + jnp.dot(ob_ref[r, :], wob_ref[...], preferred_element_type=F32) for r in rows]
    for r, mix in zip(rows, mixes):
        x1 = x_ref[r, :] + _rms(mix, gpost_ref[...])
        x1_scr[r, :] = x1
        h = _rms(x1, gpre_ref[...]).astype(BF16)
        for c in range(d_ff // ff_chunk):
            cols = slice(c * ff_chunk, (c + 1) * ff_chunk)
            gate = jnp.dot(h, wg_ref[:, cols], preferred_element_type=F32)
            up = jnp.dot(h, wu_ref[:, cols], preferred_element_type=F32)
            act_scr[r, cols] = (_silu(gate) * up).astype(BF16)
    fs = [jnp.dot(act_scr[r, :], wd_ref[...], preferred_element_type=F32) for r in rows]
    for r, f in zip(rows, fs):
        out_ref[r, :] = x1_scr[r, :] + _rms(f, gfpost_ref[...])


def _mixffn(x2d, oa, ob, woa, wob, gpost, gpre, gfpost, wg, wu, wd, tm, sub, ff_chunk):
    T, D = x2d.shape
    d_ff = wg.shape[1]
    row = lambda n: pl.BlockSpec((tm, n), lambda i: (i, 0))
    return pl.pallas_call(
        functools.partial(_mixffn_kernel, ff_chunk=ff_chunk, n_sub=tm // sub),
        grid=(T // tm,),
        in_specs=([row(D), row(GDN_WIDTH), row(SWA_WIDTH)]
                  + [_resident(woa.shape), _resident(wob.shape)] + [_resident((1, D))] * 3
                  + [_resident(wg.shape), _resident(wu.shape), _resident(wd.shape)]),
        out_specs=row(D),
        out_shape=jax.ShapeDtypeStruct((T, D), F32),
        scratch_shapes=[pltpu.VMEM((tm, d_ff), BF16), pltpu.VMEM((tm, D), F32)],
        compiler_params=pltpu.CompilerParams(dimension_semantics=("arbitrary",),
                                             vmem_limit_bytes=VMEM_LIMIT_BYTES),
        name="mixffn",
    )(x2d, oa, ob, woa, wob, gpost, gpre, gfpost, wg, wu, wd)


def _layer(x2d, batch, seq, w_in, conv_w, a_log, dt_bias, onorm_g, rel_bias, w_out,
           g_mix_pre, g_mix_post, w_gate, w_up, w_down, g_ffn_pre, g_ffn_post):
    D = x2d.shape[1]
    gw, sw, nh = GDN_WIDTH, SWA_WIDTH, GDN_HEADS
    wb = w_in.astype(BF16)
    c_gate, c_small, c_q = 3 * gw, 4 * gw, 4 * gw + 2 * nh
    w_small = jnp.pad(wb[:, c_small:c_q], ((0, 0), (0, LANES - 2 * nh)))
    lane_pad = lambda v: jnp.pad(v.astype(F32).reshape(1, nh), ((0, 0), (nh, LANES - 2 * nh)))
    qkv, gate, qb, kb, vb, bg = _inproj(
        x2d, g_mix_pre.reshape(1, D), wb[:, :c_gate], wb[:, c_gate:c_small],
        wb[:, c_q:c_q + sw], wb[:, c_q + sw:c_q + 2 * sw], wb[:, c_q + 2 * sw:], w_small,
        conv_w.astype(F32), lane_pad(a_log), lane_pad(dt_bias), batch, seq, tm=INPROJ_ROWS)

    oa = _gdn(qkv, gate, bg, onorm_g.astype(F32).reshape(1, GDN_HEAD_DIM), batch, seq,
              tb=GDN_ROWS, n_seq=GDN_SEQS)

    ob = _swa(qb, kb, vb, _swa_bias_tables(rel_bias), batch, seq)

    wo = w_out.astype(BF16)
    return _mixffn(x2d, oa, ob, wo[:gw], wo[gw:], g_mix_post.reshape(1, D),
                   g_ffn_pre.reshape(1, D), g_ffn_post.reshape(1, D), w_gate.astype(BF16),
                   w_up.astype(BF16), w_down.astype(BF16), tm=FFN_ROWS, sub=FFN_SUB_ROWS,
                   ff_chunk=FFN_COL_CHUNK)


def kernel(x, w_in, conv_w, a_log, dt_bias, onorm_g, rel_bias, w_out, g_mix_pre, g_mix_post,
           w_gate, w_up, w_down, g_ffn_pre, g_ffn_post):
    batch, seq, d_model = x.shape
    x2d = x.reshape(batch * seq, d_model)
    for l in range(w_in.shape[0]):
        x2d = _layer(x2d, batch, seq, w_in[l], conv_w[l], a_log[l], dt_bias[l], onorm_g[l],
                     rel_bias, w_out[l], g_mix_pre[l], g_mix_post[l], w_gate[l], w_up[l],
                     w_down[l], g_ffn_pre[l], g_ffn_post[l])
    return x2d.reshape(batch, seq, d_model)
```

```python
import functools
import math

import jax
import jax.numpy as jnp
import numpy as np
from jax import lax
from jax.experimental import pallas as pl
from jax.experimental.pallas import tpu as pltpu

F32 = jnp.float32
BF16 = jnp.bfloat16

GDN_HEADS = 4
GDN_HEAD_DIM = 128
GDN_WIDTH = GDN_HEADS * GDN_HEAD_DIM
CONV_WIDTH = 4
CONV_ROW_PITCH = 2
CHUNK = 64
SWA_HEADS = 8
SWA_HEAD_DIM = 64
SWA_WIDTH = SWA_HEADS * SWA_HEAD_DIM
DILATED_PATTERNS = ((128, 1), (512, 4), (2048, 16))
SWA_RESIDUES = max(dil for _, dil in DILATED_PATTERNS)
NUM_BUCKETS = 32
MAX_DISTANCE = 2048
RMS_EPS = 1e-6

LANES = 128
SUBLANES = 8
BF16_ROWS = 16
SWA_Q_SCALE = SWA_HEAD_DIM ** -0.5 * math.log2(math.e)
SWA_TASK_UNROLL = 8

INPROJ_ROWS, INPROJ_SUBTILES = 512, 2
GDN_ROWS, GDN_SEQS = 256, 2
FFN_ROWS, FFN_SUB_ROWS, FFN_COL_CHUNK = 1024, 256, 256
VMEM_LIMIT_BYTES = 56 * 1024 * 1024

NT_DIMS = (((1,), (1,)), ((), ()))
TN_DIMS = (((0,), (0,)), ((), ()))


def _rms(x, g):
    return x * lax.rsqrt(jnp.mean(x * x, axis=-1, keepdims=True) + RMS_EPS) * g


def _silu(x):
    return x * jax.nn.sigmoid(x)


def _softplus(x):
    return jnp.maximum(x, 0.0) + jnp.log1p(jnp.exp(-jnp.abs(x)))


def _resident(shape):
    zeros = (0,) * len(shape)
    return pl.BlockSpec(shape, lambda *_: zeros, pipeline_mode=pl.Buffered(1))


def _inproj_kernel(x_ref, g_ref, wqkv_ref, wgate_ref, wq_ref, wk_ref, wv_ref, ws_ref,
                   convw_ref, alog_ref, dtb_ref,
                   qkv_ref, gate_ref, qb_ref, kb_ref, vb_ref, bg_ref,
                   stage_scr, stage2_scr, ext_scr, *, n_sub):
    tm = x_ref.shape[0]
    hd = GDN_HEAD_DIM
    n_groups = 3 * GDN_WIDTH // hd
    halo = SUBLANES
    sub = tm // n_sub
    rows = [slice(i * sub, (i + 1) * sub) for i in range(n_sub)]
    hs = [_rms(x_ref[r, :], g_ref[...]).astype(BF16) for r in rows]

    at = lambda first, n: pl.ds(CONV_ROW_PITCH * first, n, stride=CONV_ROW_PITCH)

    @pl.when(pl.program_id(1) == 0)
    def _():
        for j in range(n_groups):
            ext_scr[j, at(0, halo), :] = jnp.zeros((halo, hd), F32)

    @pl.when(pl.program_id(1) > 0)
    def _():
        for j in range(n_groups):
            ext_scr[j, at(0, halo), :] = ext_scr[j, at(tm, halo), :]

    pres = [jnp.dot(h, wqkv_ref[...], preferred_element_type=F32) for h in hs]
    for i, pre in enumerate(pres):
        for j in range(n_groups):
            ext_scr[j, at(halo + i * sub, sub), :] = pre[:, j * hd:(j + 1) * hd]

    def conv_act(i):
        for j in range(n_groups):
            cols = slice(j * hd, (j + 1) * hd)
            acc = None
            for tap in range(CONV_WIDTH):
                first = halo + i * sub - (CONV_WIDTH - 1) + tap
                term = convw_ref[tap:tap + 1, cols] * ext_scr[j, at(first, sub), :]
                acc = term if acc is None else acc + term
            y = _silu(acc)
            kind = j // GDN_HEADS
            if kind < 2:
                y = y * lax.rsqrt(jnp.sum(y * y, axis=-1, keepdims=True) + 1e-6)
            if kind == 0:
                y = y * (hd ** -0.5)
            qkv_ref[rows[i], cols] = y

    n_res = SWA_RESIDUES
    f = math.isqrt(n_res)
    assert f * f == n_res
    attn_proj = ((wq_ref, qb_ref, SWA_Q_SCALE), (wk_ref, kb_ref, None), (wv_ref, vb_ref, None))

    def split(i, t, o_ref):
        r0 = i * sub
        for slab in range(SWA_WIDTH // LANES):
            for a in range(f):
                stage2_scr[t, slab, a, r0 // f:(r0 + sub) // f, :] = (
                    stage_scr[t, slab, pl.ds(r0 + a, sub // f, stride=f), :])
        for slab in range(SWA_WIDTH // LANES):
            for a in range(f):
                for b in range(f):
                    o_ref[slab, f * b + a, r0 // n_res:(r0 + sub) // n_res, :] = stage2_scr[
                        t, slab, a, pl.ds(r0 // f + b, sub // n_res, stride=f), :].astype(o_ref.dtype)

    for i, h in enumerate(hs):
        gate_ref[rows[i], :] = _silu(jnp.dot(h, wgate_ref[...], preferred_element_type=F32))
        sm = jnp.dot(h, ws_ref[...], preferred_element_type=F32)
        is_beta = lax.broadcasted_iota(jnp.int32, sm.shape, 1) < GDN_HEADS
        bg_ref[rows[i], :] = jnp.where(is_beta, jax.nn.sigmoid(sm),
                                       -jnp.exp(alog_ref[...]) * _softplus(sm + dtb_ref[...]))
        for t, (w_ref, _, scale) in enumerate(attn_proj):
            res = jnp.dot(h, w_ref[...], preferred_element_type=F32)
            if scale is not None:
                res = res * scale
            for slab in range(SWA_WIDTH // LANES):
                stage_scr[t, slab, rows[i], :] = res[:, slab * LANES:(slab + 1) * LANES]
        conv_act(i)
        for t, (_, o_ref, _) in enumerate(attn_proj):
            split(i, t, o_ref)


def _inproj(x2d, g, wqkv, wgate, wq, wk, wv, ws, conv_w, alog_row, dtb_row, batch, seq, tm):
    T, D = x2d.shape
    nt = seq // tm
    row = lambda n: pl.BlockSpec((tm, n), lambda b, j: (b * nt + j, 0))
    n_slabs = SWA_WIDTH // LANES
    slabs = pl.BlockSpec((n_slabs, None, SWA_RESIDUES, tm // SWA_RESIDUES, LANES),
                         lambda b, j: (0, b, 0, j, 0))
    slab_shape = jax.ShapeDtypeStruct(
        (n_slabs, batch, SWA_RESIDUES, seq // SWA_RESIDUES, LANES), BF16)
    return pl.pallas_call(
        functools.partial(_inproj_kernel, n_sub=INPROJ_SUBTILES),
        grid=(batch, nt),
        in_specs=([row(D), _resident((1, D))]
                  + [_resident(w.shape) for w in (wqkv, wgate, wq, wk, wv, ws, conv_w)]
                  + [_resident((1, LANES))] * 2),
        out_specs=[row(3 * GDN_WIDTH), row(GDN_WIDTH), slabs, slabs, slabs, row(LANES)],
        out_shape=[jax.ShapeDtypeStruct((T, 3 * GDN_WIDTH), F32),
                   jax.ShapeDtypeStruct((T, GDN_WIDTH), F32), slab_shape, slab_shape, slab_shape,
                   jax.ShapeDtypeStruct((T, LANES), F32)],
        scratch_shapes=[pltpu.VMEM((3, n_slabs, tm, LANES), F32),
                        pltpu.VMEM((3, n_slabs, math.isqrt(SWA_RESIDUES),
                                    tm // math.isqrt(SWA_RESIDUES), LANES), F32),
                        pltpu.VMEM((3 * GDN_WIDTH // GDN_HEAD_DIM, CONV_ROW_PITCH * (tm + SUBLANES),
                                    GDN_HEAD_DIM), F32)],
        compiler_params=pltpu.CompilerParams(dimension_semantics=("arbitrary", "arbitrary"),
                                             vmem_limit_bytes=VMEM_LIMIT_BYTES),
        name="inproj",
    )(x2d, g, wqkv, wgate, wq, wk, wv, ws, conv_w, alog_row, dtb_row)


def _unit_lower_inverses(mats, eye, xor_ij):
    n = mats[0].shape[0]
    ds = [eye - jnp.where(xor_ij < 2, a, 0.0) for a in mats]
    s = 2
    while s < n:
        band = jnp.logical_and(xor_ij >= s, xor_ij < 2 * s)
        dbs = [d.astype(BF16) for d in ds]
        eds = [jnp.dot(jnp.where(band, a, 0.0).astype(BF16), db,
                       preferred_element_type=F32).astype(BF16) for a, db in zip(mats, dbs)]
        ds = [d - jnp.dot(db, ed, preferred_element_type=F32) for d, db, ed in zip(ds, dbs, eds)]
        s *= 2
    return ds


def _gdn_kernel(qkv_ref, gate_ref, bg_ref, ong_ref, ltri_ref, o_ref, state_scr, *, tb, chunk):
    hd = GDN_HEAD_DIM

    @pl.when(pl.program_id(1) == 0)
    def _():
        state_scr[...] = jnp.zeros_like(state_scr)

    ii = lax.broadcasted_iota(jnp.int32, (chunk, chunk), 0)
    jj = lax.broadcasted_iota(jnp.int32, (chunk, chunk), 1)
    xor_ij = jnp.bitwise_xor(ii, jj)
    lower = ii >= jj
    strict = ii > jj
    eye = jnp.where(ii == jj, 1.0, 0.0).astype(F32)
    ltri = ltri_ref[...]
    ong = ong_ref[...]

    nc = tb // chunk
    heads = range(GDN_HEADS)
    rows = [slice(c * chunk, (c + 1) * chunk) for c in range(nc)]
    hcols = [slice(h * hd, (h + 1) * hd) for h in heads]
    qcols, kcols, vcols = ([slice(part * GDN_WIDTH + h * hd, part * GDN_WIDTH + (h + 1) * hd)
                            for h in heads] for part in range(3))
    seqs = range(qkv_ref.shape[0])
    probs = [(s, c, h) for s in seqs for c in range(nc) for h in heads]

    gcs = {}
    for s in seqs:
        for c in range(nc):
            g = bg_ref[s, rows[c], :]
            g_hi = g.astype(BF16)
            rem = g - g_hi.astype(F32)
            g_mid = rem.astype(BF16)
            g_lo = (rem - g_mid.astype(F32)).astype(BF16)
            gcs[s, c] = (jnp.dot(ltri, g_hi, preferred_element_type=F32)
                         + jnp.dot(ltri, g_mid, preferred_element_type=F32)
                         + jnp.dot(ltri, g_lo, preferred_element_type=F32))
    gcts = {sc: gc.T for sc, gc in gcs.items()}
    lane_of = lambda h: slice(GDN_HEADS + h, GDN_HEADS + h + 1)
    gcol = {(s, c, h): gcs[s, c][:, lane_of(h)] for s, c, h in probs}
    glast = {(s, c, h): gcs[s, c][chunk - 1:chunk, lane_of(h)] for s, c, h in probs}
    bcol = {(s, c, h): bg_ref[s, rows[c], h:h + 1] for s, c, h in probs}

    qks = {}
    for s, c, h in probs:
        kcb = qkv_ref[s, rows[c], kcols[h]].astype(BF16)
        qkb = jnp.concatenate([qkv_ref[s, rows[c], qcols[h]].astype(BF16), kcb], axis=0)
        qks[s, c, h] = lax.dot_general(qkb, kcb, NT_DIMS, preferred_element_type=F32)
    attn, a_mats = {}, []
    for s, c, h in probs:
        grow = gcts[s, c][lane_of(h), :]
        decay = jnp.exp(jnp.where(lower, gcol[s, c, h] - grow, -jnp.inf))
        attn[s, c, h] = (qks[s, c, h][:chunk] * decay).astype(BF16)
        a_mats.append(jnp.where(strict, qks[s, c, h][chunk:] * decay * bcol[s, c, h], 0.0))
    t_invs = _unit_lower_inverses(a_mats, eye, xor_ij)

    uws = {}
    for (s, c, h), t_inv in zip(probs, t_invs):
        kc = qkv_ref[s, rows[c], kcols[h]]
        eg = jnp.exp(gcol[s, c, h])
        rhs = jnp.concatenate([qkv_ref[s, rows[c], vcols[h]] * bcol[s, c, h],
                               kc * (bcol[s, c, h] * eg)], axis=1).astype(BF16)
        uws[s, c, h] = jnp.dot(t_inv.astype(BF16), rhs, preferred_element_type=F32).astype(BF16)
    kd_uw, gq, attn_u = {}, {}, {}
    for s, c, h in probs:
        k_dec = qkv_ref[s, rows[c], kcols[h]] * jnp.exp(glast[s, c, h] - gcol[s, c, h])
        kd_uw[s, c, h] = lax.dot_general(k_dec.astype(BF16), uws[s, c, h], TN_DIMS,
                                         preferred_element_type=F32)
    for s, c, h in probs:
        a_uw = jnp.dot(attn[s, c, h], uws[s, c, h], preferred_element_type=F32)
        q_eff = qkv_ref[s, rows[c], qcols[h]] * jnp.exp(gcol[s, c, h]) - a_uw[:, hd:]
        gq[s, c, h] = jnp.concatenate([kd_uw[s, c, h][:, hd:], q_eff], axis=0).astype(BF16)
        attn_u[s, c, h] = a_uw[:, :hd]

    chains = [(s, h) for s in seqs for h in heads]
    state = {sh: state_scr[sh] for sh in chains}
    for c in range(nc):
        gs = {(s, h): jnp.dot(gq[s, c, h], state[s, h].astype(BF16), preferred_element_type=F32)
              for s, h in chains}
        state = {(s, h): (state[s, h] * jnp.exp(glast[s, c, h]) - gs[s, h][:hd]
                          + kd_uw[s, c, h][:, :hd]) for s, h in chains}
        for s, h in chains:
            o = gs[s, h][hd:] + attn_u[s, c, h]
            gated = _rms(o, ong) * gate_ref[s, rows[c], hcols[h]]
            o_ref[s, rows[c], hcols[h]] = gated.astype(o_ref.dtype)
    for sh in chains:
        state_scr[sh] = state[sh]


def _gdn(qkv, gate, bg, onorm_g, batch, seq, tb, n_seq):
    idx = jnp.arange(CHUNK)
    ltri = (idx[:, None] >= idx[None, :]).astype(BF16)
    blk = lambda n: pl.BlockSpec((n_seq, tb, n), lambda b, t: (b, t, 0))
    per_seq = lambda a: a.reshape(batch, seq, a.shape[-1])
    kern = functools.partial(_gdn_kernel, tb=tb, chunk=CHUNK)
    out = pl.pallas_call(
        kern,
        grid=(batch // n_seq, seq // tb),
        in_specs=[blk(3 * GDN_WIDTH), blk(GDN_WIDTH), blk(LANES), _resident((1, GDN_HEAD_DIM)),
                  _resident((CHUNK, CHUNK))],
        out_specs=blk(GDN_WIDTH),
        out_shape=jax.ShapeDtypeStruct((batch, seq, GDN_WIDTH), BF16),
        scratch_shapes=[pltpu.VMEM((n_seq, GDN_HEADS, GDN_HEAD_DIM, GDN_HEAD_DIM), F32)],
        compiler_params=pltpu.CompilerParams(dimension_semantics=("arbitrary", "arbitrary"),
                                             vmem_limit_bytes=VMEM_LIMIT_BYTES),
        name="gdn",
    )(per_seq(qkv), per_seq(gate), per_seq(bg), onorm_g, ltri)
    return out.reshape(batch * seq, GDN_WIDTH)


def _swa_kernel(q_ref, k_ref, v_ref, bias_ref, sel_ref, o_ref, acc_scr, m_scr, s_scr, out_scr,
                *, w):
    n_res = SWA_RESIDUES
    tile = pl.program_id(1)
    n_slabs = SWA_WIDTH // LANES
    heads_per_slab = LANES // SWA_HEAD_DIM
    assert heads_per_slab == 2
    tasks_per_call = lambda dil: 2 if (w * dil // n_res) % BF16_ROWS else 1
    lane = lax.broadcasted_iota(jnp.int32, (w, LANES), 1)
    low_half = lane < SWA_HEAD_DIM

    def expand(packed):
        hi = packed.astype(BF16)
        lo = (packed - hi.astype(F32)).astype(BF16)
        return jnp.dot(jnp.concatenate([hi, lo], axis=1), sel_ref[...], preferred_element_type=F32)

    def task(pat, dil, t, sub):
        n_chunks = n_res // dil
        c = w // n_chunks
        per_call = tasks_per_call(dil)
        t = t * per_call + sub
        res_d = jnp.bitwise_and(t, dil - 1)
        blk = jnp.right_shift(t, dil.bit_length() - 1)
        n_glob = tile * n_chunks + blk
        q0 = pl.multiple_of(blk * c, c)
        cur0 = pl.multiple_of(n_glob * c, c)
        prev0 = pl.multiple_of(jnp.maximum(n_glob - 1, 0) * c, c)
        is_first_blk = (n_glob == 0).astype(jnp.int32)
        res_of = [a * dil + res_d for a in range(n_chunks)]

        def gather(ref, lead, start):
            return jnp.concatenate([ref[lead + (res, pl.ds(start, c), slice(None))]
                                    for res in res_of], axis=0)

        def operand(ref, slab, start, shift):
            if c % BF16_ROWS == 0:
                return gather(ref, (slab,), start)
            assert BF16_ROWS == 2 * c and dil == 1 and per_call % 2 == 0
            half = (sub + shift) % 2
            tile0 = pl.multiple_of(jnp.maximum(start - half * c, 0), BF16_ROWS)
            return jnp.concatenate(
                [ref[slab, res, pl.ds(tile0, BF16_ROWS), :].astype(F32)[half * c:(half + 1) * c]
                 for res in res_of], axis=0).astype(BF16)

        logits, vps = [], []
        for slab in range(n_slabs):
            qp = operand(q_ref, slab, q0, 0)
            kp = jnp.concatenate([operand(k_ref, slab, prev0, -1), operand(k_ref, slab, cur0, 0)],
                                 axis=0)
            vps.append(jnp.concatenate([operand(v_ref, slab, prev0, -1),
                                        operand(v_ref, slab, cur0, 0)], axis=0))
            for e in range(heads_per_slab):
                mine = low_half if e == 0 else jnp.logical_not(low_half)
                qh = jnp.where(mine, qp, jnp.zeros_like(qp))
                lg = lax.dot_general(qh, kp, NT_DIMS, preferred_element_type=F32)
                logits.append(lg + bias_ref[pat, is_first_blk, slab * heads_per_slab + e])
        m_blk = jnp.zeros((w, LANES), F32)
        s_blk = jnp.ones((w, LANES), F32)
        ps = []
        for h, lg in enumerate(logits):
            m_h = jnp.max(lg, axis=-1, keepdims=True)
            p = jnp.exp2(lg - m_h)
            ps.append(p.astype(BF16))
            m_blk = jnp.where(lane == h, m_h, m_blk)
            s_blk = jnp.where(lane == h, jnp.sum(p, axis=-1, keepdims=True), s_blk)
        nums = []
        for slab in range(n_slabs):
            h0 = slab * heads_per_slab
            outs = [jnp.dot(ps[h0 + e], vps[slab], preferred_element_type=F32)
                    for e in range(heads_per_slab)]
            nums.append(jnp.where(low_half, outs[0], outs[1]))
        for idx, res in enumerate(res_of):
            rows, part = pl.ds(q0, c), slice(idx * c, (idx + 1) * c)
            for slab in range(n_slabs):
                acc_scr[pat, slab, res, rows, :] = nums[slab][part]
            m_scr[pat, res, rows, :] = m_blk[part]
            s_scr[pat, res, rows, :] = s_blk[part]

    n_tasks = n_res
    for pat, (window, dil) in enumerate(DILATED_PATTERNS):
        assert window // dil == w and n_res % dil == 0 and (w * dil) % n_res == 0

        per_call = tasks_per_call(dil)

        def body(t, carry, pat=pat, dil=dil, per_call=per_call):
            for sub in range(per_call):
                task(pat, dil, t, sub)
            return carry
        lax.fori_loop(0, n_tasks // per_call, body, 0, unroll=SWA_TASK_UNROLL // per_call)

    n_pat = len(DILATED_PATTERNS)

    def finish(r, carry):
        ms = [m_scr[p, r] for p in range(n_pat)]
        m_max = functools.reduce(jnp.maximum, ms)
        wts = [jnp.exp2(m - m_max) for m in ms]
        den = functools.reduce(jnp.add, [wt * s_scr[p, r] for p, wt in enumerate(wts)])
        coefs = expand(jnp.concatenate([wt / den for wt in wts], axis=0))
        for slab in range(n_slabs):
            cols = slice(slab * LANES, (slab + 1) * LANES)
            out_scr[slab, pl.ds(r, w, stride=n_res), :] = functools.reduce(
                jnp.add, [coefs[p * w:(p + 1) * w, cols] * acc_scr[p, slab, r]
                          for p in range(n_pat)])
        return carry
    lax.fori_loop(0, n_res, finish, 0, unroll=4)
    for slab in range(n_slabs):
        o_ref[:, slab * LANES:(slab + 1) * LANES] = out_scr[slab].astype(o_ref.dtype)


def _swa(qb, kb, vb, bias, batch, seq):
    n_slabs = SWA_WIDTH // LANES
    n_res = SWA_RESIDUES
    n_pat = len(DILATED_PATTERNS)
    w = DILATED_PATTERNS[0][0] // DILATED_PATTERNS[0][1]
    tq = n_res * w
    nq = seq // tq
    q_blk = pl.BlockSpec((n_slabs, None, n_res, w, LANES), lambda b, i: (0, b, 0, i, 0))
    seq_blk = pl.BlockSpec((n_slabs, None, n_res, seq // n_res, LANES), lambda b, i: (0, b, 0, 0, 0))
    head_of_col = jnp.arange(SWA_WIDTH) // SWA_HEAD_DIM
    sel = (jnp.arange(LANES)[:, None] == head_of_col[None, :]).astype(BF16)
    sel = jnp.concatenate([sel, sel], axis=0)
    return pl.pallas_call(
        functools.partial(_swa_kernel, w=w),
        grid=(batch, nq),
        in_specs=[q_blk, seq_blk, seq_blk, _resident(bias.shape), _resident(sel.shape)],
        out_specs=pl.BlockSpec((tq, SWA_WIDTH), lambda b, i: (b * nq + i, 0)),
        out_shape=jax.ShapeDtypeStruct((batch * seq, SWA_WIDTH), BF16),
        scratch_shapes=[pltpu.VMEM((n_pat, n_slabs, n_res, w, LANES), F32),
                        pltpu.VMEM((n_pat, n_res, w, LANES), F32),
                        pltpu.VMEM((n_pat, n_res, w, LANES), F32),
                        pltpu.VMEM((n_slabs, tq, LANES), F32)],
        compiler_params=pltpu.CompilerParams(dimension_semantics=("arbitrary", "arbitrary"),
                                             vmem_limit_bytes=VMEM_LIMIT_BYTES),
        name="swa",
    )(qb, kb, vb, bias, sel)


def _t5_causal_bucket(dist):
    max_exact = NUM_BUCKETS // 2
    d = jnp.maximum(dist, 1).astype(F32)
    log_b = max_exact + (jnp.log(d / max_exact) / math.log(MAX_DISTANCE / max_exact)
                         * (NUM_BUCKETS - max_exact)).astype(jnp.int32)
    return jnp.where(dist < max_exact, dist, jnp.minimum(log_b, NUM_BUCKETS - 1))


def _swa_bias_tables(rel_bias):
    w = DILATED_PATTERNS[0][0] // DILATED_PATTERNS[0][1]
    rels, steps = [], []
    for window, dil in DILATED_PATTERNS:
        n_chunks = SWA_RESIDUES // dil
        c = w // n_chunks
        pos = (np.arange(w) % c) * n_chunks + np.arange(w) // c
        rels.append(pos[:, None] + w - np.concatenate([pos, w + pos])[None, :])
        steps.append(rel_bias[_t5_causal_bucket(jnp.arange(w + 1) * dil)].astype(F32))
    rel = np.stack(rels)
    one_hot = (jnp.asarray(np.clip(rel, 0, w))[..., None] == jnp.arange(w + 1)).astype(F32)
    table = jnp.einsum("prh,pijr->phij", jnp.stack(steps), one_hot,
                       precision=lax.Precision.HIGHEST) * math.log2(math.e)
    in_band = np.logical_and(rel >= 0, rel <= w)[:, None]
    no_prev = np.arange(2 * w) >= w
    keep = np.stack([in_band, np.logical_and(in_band, no_prev)], axis=1)
    return jnp.where(jnp.asarray(keep), table[:, None], -jnp.inf)


def _mixffn_kernel(x_ref, oa_ref, ob_ref, woa_ref, wob_ref, gpost_ref, gpre_ref, gfpost_ref,
                   wg_ref, wu_ref, wd_ref, out_ref, act_scr, x1_scr, *, ff_chunk, n_sub):
    sub = x_ref.shape[0] // n_sub
    rows = [slice(i * sub, (i + 1) * sub) for i in range(n_sub)]
    d_ff = wg_ref.shape[1]
    mixes = [jnp.dot(oa_ref[r, :], woa_ref[...], preferred_element_type=F32)
             + jnp.dot(ob_ref[r, :], wob_ref[...], preferred_element_type=F32) for r in rows]
    for r, mix in zip(rows, mixes):
        x1 = x_ref[r, :] + _rms(mix, gpost_ref[...])
        x1_scr[r, :] = x1
        h = _rms(x1, gpre_ref[...]).astype(BF16)
        for c in range(d_ff // ff_chunk):
            cols = slice(c * ff_chunk, (c + 1) * ff_chunk)
            gate = jnp.dot(h, wg_ref[:, cols], preferred_element_type=F32)
            up = jnp.dot(h, wu_ref[:, cols], preferred_element_type=F32)
            act_scr[r, cols] = (_silu(gate) * up).astype(BF16)
    fs = [jnp.dot(act_scr[r, :], wd_ref[...], preferred_element_type=F32) for r in rows]
    for r, f in zip(rows, fs):
        out_ref[r, :] = x1_scr[r, :] + _rms(f, gfpost_ref[...])


def _mixffn(x2d, oa, ob, woa, wob, gpost, gpre, gfpost, wg, wu, wd, tm, sub, ff_chunk):
    T, D = x2d.shape
    d_ff = wg.shape[1]
    row = lambda n: pl.BlockSpec((tm, n), lambda i: (i, 0))
    return pl.pallas_call(
        functools.partial(_mixffn_kernel, ff_chunk=ff_chunk, n_sub=tm // sub),
        grid=(T // tm,),
        in_specs=([row(D), row(GDN_WIDTH), row(SWA_WIDTH)]
                  + [_resident(woa.shape), _resident(wob.shape)] + [_resident((1, D))] * 3
                  + [_resident(wg.shape), _resident(wu.shape), _resident(wd.shape)]),
        out_specs=row(D),
        out_shape=jax.ShapeDtypeStruct((T, D), F32),
        scratch_shapes=[pltpu.VMEM((tm, d_ff), BF16), pltpu.VMEM((tm, D), F32)],
        compiler_params=pltpu.CompilerParams(dimension_semantics=("arbitrary",),
                                             vmem_limit_bytes=VMEM_LIMIT_BYTES),
        name="mixffn",
    )(x2d, oa, ob, woa, wob, gpost, gpre, gfpost, wg, wu, wd)


def _layer(x2d, batch, seq, w_in, conv_w, a_log, dt_bias, onorm_g, rel_bias, w_out,
           g_mix_pre, g_mix_post, w_gate, w_up, w_down, g_ffn_pre, g_ffn_post):
    D = x2d.shape[1]
    gw, sw, nh = GDN_WIDTH, SWA_WIDTH, GDN_HEADS
    cols = lambda a, b: w_in[:, a:b].astype(BF16)
    c_gate, c_small, c_q = 3 * gw, 4 * gw, 4 * gw + 2 * nh
    w_small = jnp.pad(cols(c_small, c_q), ((0, 0), (0, LANES - 2 * nh)))
    lane_pad = lambda v: jnp.pad(v.astype(F32).reshape(1, nh), ((0, 0), (nh, LANES - 2 * nh)))
    qkv, gate, qb, kb, vb, bg = _inproj(
        x2d, g_mix_pre.reshape(1, D), cols(0, c_gate), cols(c_gate, c_small),
        cols(c_q, c_q + sw), cols(c_q + sw, c_q + 2 * sw), cols(c_q + 2 * sw, c_q + 3 * sw), w_small,
        conv_w.astype(F32), lane_pad(a_log), lane_pad(dt_bias), batch, seq, tm=INPROJ_ROWS)

    oa = _gdn(qkv, gate, bg, onorm_g.astype(F32).reshape(1, GDN_HEAD_DIM), batch, seq,
              tb=GDN_ROWS, n_seq=GDN_SEQS)

    ob = _swa(qb, kb, vb, _swa_bias_tables(rel_bias), batch, seq)

    wo = w_out.astype(BF16)
    return _mixffn(x2d, oa, ob, wo[:gw], wo[gw:], g_mix_post.reshape(1, D),
                   g_ffn_pre.reshape(1, D), g_ffn_post.reshape(1, D), w_gate.astype(BF16),
                   w_up.astype(BF16), w_down.astype(BF16), tm=FFN_ROWS, sub=FFN_SUB_ROWS,
                   ff_chunk=FFN_COL_CHUNK)


def kernel(x, w_in, conv_w, a_log, dt_bias, onorm_g, rel_bias, w_out, g_mix_pre, g_mix_post,
           w_gate, w_up, w_down, g_ffn_pre, g_ffn_post):
    batch, seq, d_model = x.shape
    x2d = x.reshape(batch * seq, d_model)
    for l in range(w_in.shape[0]):
        x2d = _layer(x2d, batch, seq, w_in[l], conv_w[l], a_log[l], dt_bias[l], onorm_g[l],
                     rel_bias, w_out[l], g_mix_pre[l], g_mix_post[l], w_gate[l], w_up[l],
                     w_down[l], g_ffn_pre[l], g_ffn_post[l])
    return x2d.reshape(batch, seq, d_model)
```

```python
import functools
import math

import jax
import jax.numpy as jnp
import numpy as np
from jax import lax
from jax.experimental import pallas as pl
from jax.experimental.pallas import tpu as pltpu

F32 = jnp.float32
BF16 = jnp.bfloat16

GDN_HEADS = 4
GDN_HEAD_DIM = 128
GDN_WIDTH = GDN_HEADS * GDN_HEAD_DIM
CONV_WIDTH = 4
CONV_ROW_PITCH = 2
CHUNK = 64
SWA_HEADS = 8
SWA_HEAD_DIM = 64
SWA_WIDTH = SWA_HEADS * SWA_HEAD_DIM
DILATED_PATTERNS = ((128, 1), (512, 4), (2048, 16))
SWA_RESIDUES = max(dil for _, dil in DILATED_PATTERNS)
NUM_BUCKETS = 32
MAX_DISTANCE = 2048
RMS_EPS = 1e-6

LANES = 128
SUBLANES = 8
BF16_ROWS = 16
SWA_Q_SCALE = SWA_HEAD_DIM ** -0.5 * math.log2(math.e)
SWA_TASK_UNROLL = 8

INPROJ_ROWS, INPROJ_SUBTILES = 512, 2
GDN_ROWS, GDN_SEQS = 256, 2
FFN_ROWS, FFN_SUB_ROWS, FFN_COL_CHUNK = 1024, 256, 256
VMEM_LIMIT_BYTES = 56 * 1024 * 1024

NT_DIMS = (((1,), (1,)), ((), ()))
TN_DIMS = (((0,), (0,)), ((), ()))


def _rms(x, g):
    return x * lax.rsqrt(jnp.mean(x * x, axis=-1, keepdims=True) + RMS_EPS) * g


def _silu(x):
    return x * jax.nn.sigmoid(x)


def _softplus(x):
    return jnp.maximum(x, 0.0) + jnp.log1p(jnp.exp(-jnp.abs(x)))


def _resident(shape):
    zeros = (0,) * len(shape)
    return pl.BlockSpec(shape, lambda *_: zeros, pipeline_mode=pl.Buffered(1))


def _inproj_kernel(x_ref, g_ref, wqkv_ref, wgate_ref, wq_ref, wk_ref, wv_ref, ws_ref,
                   convw_ref, alog_ref, dtb_ref,
                   qkv_ref, gate_ref, qb_ref, kb_ref, vb_ref, bg_ref,
                   stage_scr, stage2_scr, ext_scr, *, n_sub):
    tm = x_ref.shape[0]
    hd = GDN_HEAD_DIM
    n_groups = 3 * GDN_WIDTH // hd
    halo = SUBLANES
    sub = tm // n_sub
    rows = [slice(i * sub, (i + 1) * sub) for i in range(n_sub)]
    hs = [_rms(x_ref[r, :], g_ref[...]).astype(BF16) for r in rows]

    at = lambda first, n: pl.ds(CONV_ROW_PITCH * first, n, stride=CONV_ROW_PITCH)

    @pl.when(pl.program_id(1) == 0)
    def _():
        for j in range(n_groups):
            ext_scr[j, at(0, halo), :] = jnp.zeros((halo, hd), F32)

    @pl.when(pl.program_id(1) > 0)
    def _():
        for j in range(n_groups):
            ext_scr[j, at(0, halo), :] = ext_scr[j, at(tm, halo), :]

    pres = [jnp.dot(h, wqkv_ref[...], preferred_element_type=F32) for h in hs]
    for i, pre in enumerate(pres):
        for j in range(n_groups):
            ext_scr[j, at(halo + i * sub, sub), :] = pre[:, j * hd:(j + 1) * hd]

    def conv_act(i):
        for j in range(n_groups):
            cols = slice(j * hd, (j + 1) * hd)
            acc = None
            for tap in range(CONV_WIDTH):
                first = halo + i * sub - (CONV_WIDTH - 1) + tap
                term = convw_ref[tap:tap + 1, cols] * ext_scr[j, at(first, sub), :]
                acc = term if acc is None else acc + term
            y = _silu(acc)
            kind = j // GDN_HEADS
            if kind < 2:
                y = y * lax.rsqrt(jnp.sum(y * y, axis=-1, keepdims=True) + 1e-6)
            if kind == 0:
                y = y * (hd ** -0.5)
            qkv_ref[rows[i], cols] = y

    n_res = SWA_RESIDUES
    f = math.isqrt(n_res)
    assert f * f == n_res
    attn_proj = ((wq_ref, qb_ref, SWA_Q_SCALE), (wk_ref, kb_ref, None), (wv_ref, vb_ref, None))

    def split(i, t, o_ref):
        r0 = i * sub
        for slab in range(SWA_WIDTH // LANES):
            for a in range(f):
                stage2_scr[t, slab, a, r0 // f:(r0 + sub) // f, :] = (
                    stage_scr[t, slab, pl.ds(r0 + a, sub // f, stride=f), :])
        for slab in range(SWA_WIDTH // LANES):
            for a in range(f):
                for b in range(f):
                    o_ref[slab, f * b + a, r0 // n_res:(r0 + sub) // n_res, :] = stage2_scr[
                        t, slab, a, pl.ds(r0 // f + b, sub // n_res, stride=f), :].astype(o_ref.dtype)

    for i, h in enumerate(hs):
        gate_ref[rows[i], :] = _silu(jnp.dot(h, wgate_ref[...], preferred_element_type=F32))
        sm = jnp.dot(h, ws_ref[...], preferred_element_type=F32)
        is_beta = lax.broadcasted_iota(jnp.int32, sm.shape, 1) < GDN_HEADS
        bg_ref[rows[i], :] = jnp.where(is_beta, jax.nn.sigmoid(sm),
                                       -jnp.exp(alog_ref[...]) * _softplus(sm + dtb_ref[...]))
        for t, (w_ref, _, scale) in enumerate(attn_proj):
            res = jnp.dot(h, w_ref[...], preferred_element_type=F32)
            if scale is not None:
                res = res * scale
            for slab in range(SWA_WIDTH // LANES):
                stage_scr[t, slab, rows[i], :] = res[:, slab * LANES:(slab + 1) * LANES]
        conv_act(i)
        for t, (_, o_ref, _) in enumerate(attn_proj):
            split(i, t, o_ref)


def _inproj(x2d, g, wqkv, wgate, wq, wk, wv, ws, conv_w, alog_row, dtb_row, batch, seq, tm):
    T, D = x2d.shape
    nt = seq // tm
    row = lambda n: pl.BlockSpec((tm, n), lambda b, j: (b * nt + j, 0))
    n_slabs = SWA_WIDTH // LANES
    slabs = pl.BlockSpec((n_slabs, None, SWA_RESIDUES, tm // SWA_RESIDUES, LANES),
                         lambda b, j: (0, b, 0, j, 0))
    slab_shape = jax.ShapeDtypeStruct(
        (n_slabs, batch, SWA_RESIDUES, seq // SWA_RESIDUES, LANES), BF16)
    return pl.pallas_call(
        functools.partial(_inproj_kernel, n_sub=INPROJ_SUBTILES),
        grid=(batch, nt),
        in_specs=([row(D), _resident((1, D))]
                  + [_resident(w.shape) for w in (wqkv, wgate, wq, wk, wv, ws, conv_w)]
                  + [_resident((1, LANES))] * 2),
        out_specs=[row(3 * GDN_WIDTH), row(GDN_WIDTH), slabs, slabs, slabs, row(LANES)],
        out_shape=[jax.ShapeDtypeStruct((T, 3 * GDN_WIDTH), F32),
                   jax.ShapeDtypeStruct((T, GDN_WIDTH), F32), slab_shape, slab_shape, slab_shape,
                   jax.ShapeDtypeStruct((T, LANES), F32)],
        scratch_shapes=[pltpu.VMEM((3, n_slabs, tm, LANES), F32),
                        pltpu.VMEM((3, n_slabs, math.isqrt(SWA_RESIDUES),
                                    tm // math.isqrt(SWA_RESIDUES), LANES), F32),
                        pltpu.VMEM((3 * GDN_WIDTH // GDN_HEAD_DIM, CONV_ROW_PITCH * (tm + SUBLANES),
                                    GDN_HEAD_DIM), F32)],
        compiler_params=pltpu.CompilerParams(dimension_semantics=("arbitrary", "arbitrary"),
                                             vmem_limit_bytes=VMEM_LIMIT_BYTES),
        name="inproj",
    )(x2d, g, wqkv, wgate, wq, wk, wv, ws, conv_w, alog_row, dtb_row)


def _unit_lower_inverses(mats, eye, xor_ij):
    n = mats[0].shape[0]
    ds = [eye - jnp.where(xor_ij < 2, a, 0.0) for a in mats]
    s = 2
    while s < n:
        band = jnp.logical_and(xor_ij >= s, xor_ij < 2 * s)
        dbs = [d.astype(BF16) for d in ds]
        eds = [jnp.dot(jnp.where(band, a, 0.0).astype(BF16), db,
                       preferred_element_type=F32).astype(BF16) for a, db in zip(mats, dbs)]
        ds = [d - jnp.dot(db, ed, preferred_element_type=F32) for d, db, ed in zip(ds, dbs, eds)]
        s *= 2
    return ds


def _gdn_kernel(qkv_ref, gate_ref, bg_ref, ong_ref, ltri_ref, o_ref, state_scr, *, tb, chunk):
    hd = GDN_HEAD_DIM

    @pl.when(pl.program_id(1) == 0)
    def _():
        state_scr[...] = jnp.zeros_like(state_scr)

    ii = lax.broadcasted_iota(jnp.int32, (chunk, chunk), 0)
    jj = lax.broadcasted_iota(jnp.int32, (chunk, chunk), 1)
    xor_ij = jnp.bitwise_xor(ii, jj)
    lower = ii >= jj
    strict = ii > jj
    eye = jnp.where(ii == jj, 1.0, 0.0).astype(F32)
    ltri = ltri_ref[...]
    ong = ong_ref[...]

    nc = tb // chunk
    heads = range(GDN_HEADS)
    rows = [slice(c * chunk, (c + 1) * chunk) for c in range(nc)]
    hcols = [slice(h * hd, (h + 1) * hd) for h in heads]
    qcols, kcols, vcols = ([slice(part * GDN_WIDTH + h * hd, part * GDN_WIDTH + (h + 1) * hd)
                            for h in heads] for part in range(3))
    seqs = range(qkv_ref.shape[0])
    probs = [(s, c, h) for s in seqs for c in range(nc) for h in heads]

    gcs = {}
    for s in seqs:
        for c in range(nc):
            g = bg_ref[s, rows[c], :]
            g_hi = g.astype(BF16)
            rem = g - g_hi.astype(F32)
            g_mid = rem.astype(BF16)
            g_lo = (rem - g_mid.astype(F32)).astype(BF16)
            gcs[s, c] = (jnp.dot(ltri, g_hi, preferred_element_type=F32)
                         + jnp.dot(ltri, g_mid, preferred_element_type=F32)
                         + jnp.dot(ltri, g_lo, preferred_element_type=F32))
    gcts = {sc: gc.T for sc, gc in gcs.items()}
    lane_of = lambda h: slice(GDN_HEADS + h, GDN_HEADS + h + 1)
    gcol = {(s, c, h): gcs[s, c][:, lane_of(h)] for s, c, h in probs}
    glast = {(s, c, h): gcs[s, c][chunk - 1:chunk, lane_of(h)] for s, c, h in probs}
    bcol = {(s, c, h): bg_ref[s, rows[c], h:h + 1] for s, c, h in probs}

    qks = {}
    for s, c, h in probs:
        kcb = qkv_ref[s, rows[c], kcols[h]].astype(BF16)
        qkb = jnp.concatenate([qkv_ref[s, rows[c], qcols[h]].astype(BF16), kcb], axis=0)
        qks[s, c, h] = lax.dot_general(qkb, kcb, NT_DIMS, preferred_element_type=F32)
    attn, a_mats = {}, []
    for s, c, h in probs:
        grow = gcts[s, c][lane_of(h), :]
        decay = jnp.exp(jnp.where(lower, gcol[s, c, h] - grow, -jnp.inf))
        attn[s, c, h] = (qks[s, c, h][:chunk] * decay).astype(BF16)
        a_mats.append(jnp.where(strict, qks[s, c, h][chunk:] * decay * bcol[s, c, h], 0.0))
    t_invs = _unit_lower_inverses(a_mats, eye, xor_ij)

    uws = {}
    for (s, c, h), t_inv in zip(probs, t_invs):
        kc = qkv_ref[s, rows[c], kcols[h]]
        eg = jnp.exp(gcol[s, c, h])
        rhs = jnp.concatenate([qkv_ref[s, rows[c], vcols[h]] * bcol[s, c, h],
                               kc * (bcol[s, c, h] * eg)], axis=1).astype(BF16)
        uws[s, c, h] = jnp.dot(t_inv.astype(BF16), rhs, preferred_element_type=F32).astype(BF16)
    kd_uw, gq, attn_u = {}, {}, {}
    for s, c, h in probs:
        k_dec = qkv_ref[s, rows[c], kcols[h]] * jnp.exp(glast[s, c, h] - gcol[s, c, h])
        kd_uw[s, c, h] = lax.dot_general(k_dec.astype(BF16), uws[s, c, h], TN_DIMS,
                                         preferred_element_type=F32)
    for s, c, h in probs:
        a_uw = jnp.dot(attn[s, c, h], uws[s, c, h], preferred_element_type=F32)
        q_eff = qkv_ref[s, rows[c], qcols[h]] * jnp.exp(gcol[s, c, h]) - a_uw[:, hd:]
        gq[s, c, h] = jnp.concatenate([kd_uw[s, c, h][:, hd:], q_eff], axis=0).astype(BF16)
        attn_u[s, c, h] = a_uw[:, :hd]

    chains = [(s, h) for s in seqs for h in heads]
    state = {sh: state_scr[sh] for sh in chains}
    for c in range(nc):
        gs = {(s, h): jnp.dot(gq[s, c, h], state[s, h].astype(BF16), preferred_element_type=F32)
              for s, h in chains}
        state = {(s, h): (state[s, h] * jnp.exp(glast[s, c, h]) - gs[s, h][:hd]
                          + kd_uw[s, c, h][:, :hd]) for s, h in chains}
        for s, h in chains:
            o = gs[s, h][hd:] + attn_u[s, c, h]
            gated = _rms(o, ong) * gate_ref[s, rows[c], hcols[h]]
            o_ref[s, rows[c], hcols[h]] = gated.astype(o_ref.dtype)
    for sh in chains:
        state_scr[sh] = state[sh]


def _gdn(qkv, gate, bg, onorm_g, batch, seq, tb, n_seq):
    idx = jnp.arange(CHUNK)
    ltri = (idx[:, None] >= idx[None, :]).astype(BF16)
    blk = lambda n: pl.BlockSpec((n_seq, tb, n), lambda b, t: (b, t, 0))
    per_seq = lambda a: a.reshape(batch, seq, a.shape[-1])
    kern = functools.partial(_gdn_kernel, tb=tb, chunk=CHUNK)
    out = pl.pallas_call(
        kern,
        grid=(batch // n_seq, seq // tb),
        in_specs=[blk(3 * GDN_WIDTH), blk(GDN_WIDTH), blk(LANES), _resident((1, GDN_HEAD_DIM)),
                  _resident((CHUNK, CHUNK))],
        out_specs=blk(GDN_WIDTH),
        out_shape=jax.ShapeDtypeStruct((batch, seq, GDN_WIDTH), BF16),
        scratch_shapes=[pltpu.VMEM((n_seq, GDN_HEADS, GDN_HEAD_DIM, GDN_HEAD_DIM), F32)],
        compiler_params=pltpu.CompilerParams(dimension_semantics=("arbitrary", "arbitrary"),
                                             vmem_limit_bytes=VMEM_LIMIT_BYTES),
        name="gdn",
    )(per_seq(qkv), per_seq(gate), per_seq(bg), onorm_g, ltri)
    return out.reshape(batch * seq, GDN_WIDTH)


def _swa_kernel(q_ref, k_ref, v_ref, bias_ref, sel_ref, o_ref, acc_scr, m_scr, s_scr, out_scr,
                *, w):
    n_res = SWA_RESIDUES
    tile = pl.program_id(1)
    n_slabs = SWA_WIDTH // LANES
    heads_per_slab = LANES // SWA_HEAD_DIM
    assert heads_per_slab == 2
    tasks_per_call = lambda dil: 2 if (w * dil // n_res) % BF16_ROWS else 1
    lane = lax.broadcasted_iota(jnp.int32, (w, LANES), 1)
    low_half = lane < SWA_HEAD_DIM

    def expand(packed):
        hi = packed.astype(BF16)
        lo = (packed - hi.astype(F32)).astype(BF16)
        return jnp.dot(jnp.concatenate([hi, lo], axis=1), sel_ref[...], preferred_element_type=F32)

    def task(pat, dil, t, sub):
        n_chunks = n_res // dil
        c = w // n_chunks
        per_call = tasks_per_call(dil)
        t = t * per_call + sub
        res_d = jnp.bitwise_and(t, dil - 1)
        blk = jnp.right_shift(t, dil.bit_length() - 1)
        n_glob = tile * n_chunks + blk
        q0 = pl.multiple_of(blk * c, c)
        cur0 = pl.multiple_of(n_glob * c, c)
        prev0 = pl.multiple_of(jnp.maximum(n_glob - 1, 0) * c, c)
        is_first_blk = (n_glob == 0).astype(jnp.int32)
        res_of = [a * dil + res_d for a in range(n_chunks)]

        def gather(ref, lead, start):
            return jnp.concatenate([ref[lead + (res, pl.ds(start, c), slice(None))]
                                    for res in res_of], axis=0)

        def operand(ref, slab, start, shift):
            if c % BF16_ROWS == 0:
                return gather(ref, (slab,), start)
            assert BF16_ROWS == 2 * c and dil == 1 and per_call % 2 == 0
            half = (sub + shift) % 2
            tile0 = pl.multiple_of(jnp.maximum(start - half * c, 0), BF16_ROWS)
            return jnp.concatenate(
                [ref[slab, res, pl.ds(tile0, BF16_ROWS), :].astype(F32)[half * c:(half + 1) * c]
                 for res in res_of], axis=0).astype(BF16)

        logits, vps = [], []
        for slab in range(n_slabs):
            qp = operand(q_ref, slab, q0, 0)
            kp = jnp.concatenate([operand(k_ref, slab, prev0, -1), operand(k_ref, slab, cur0, 0)],
                                 axis=0)
            vps.append(jnp.concatenate([operand(v_ref, slab, prev0, -1),
                                        operand(v_ref, slab, cur0, 0)], axis=0))
            for e in range(heads_per_slab):
                mine = low_half if e == 0 else jnp.logical_not(low_half)
                qh = jnp.where(mine, qp, jnp.zeros_like(qp))
                lg = lax.dot_general(qh, kp, NT_DIMS, preferred_element_type=F32)
                logits.append(lg + bias_ref[pat, is_first_blk, slab * heads_per_slab + e])
        m_blk = jnp.zeros((w, LANES), F32)
        s_blk = jnp.ones((w, LANES), F32)
        ps = []
        for h, lg in enumerate(logits):
            m_h = jnp.max(lg, axis=-1, keepdims=True)
            p = jnp.exp2(lg - m_h)
            ps.append(p.astype(BF16))
            m_blk = jnp.where(lane == h, m_h, m_blk)
            s_blk = jnp.where(lane == h, jnp.sum(p, axis=-1, keepdims=True), s_blk)
        nums = []
        for slab in range(n_slabs):
            h0 = slab * heads_per_slab
            outs = [jnp.dot(ps[h0 + e], vps[slab], preferred_element_type=F32)
                    for e in range(heads_per_slab)]
            nums.append(jnp.where(low_half, outs[0], outs[1]))
        for idx, res in enumerate(res_of):
            rows, part = pl.ds(q0, c), slice(idx * c, (idx + 1) * c)
            for slab in range(n_slabs):
                acc_scr[pat, slab, res, rows, :] = nums[slab][part]
            m_scr[pat, res, rows, :] = m_blk[part]
            s_scr[pat, res, rows, :] = s_blk[part]

    n_tasks = n_res
    for pat, (window, dil) in enumerate(DILATED_PATTERNS):
        assert window // dil == w and n_res % dil == 0 and (w * dil) % n_res == 0

        per_call = tasks_per_call(dil)

        def body(t, carry, pat=pat, dil=dil, per_call=per_call):
            for sub in range(per_call):
                task(pat, dil, t, sub)
            return carry
        lax.fori_loop(0, n_tasks // per_call, body, 0, unroll=SWA_TASK_UNROLL // per_call)

    n_pat = len(DILATED_PATTERNS)

    def finish(r, carry):
        ms = [m_scr[p, r] for p in range(n_pat)]
        m_max = functools.reduce(jnp.maximum, ms)
        wts = [jnp.exp2(m - m_max) for m in ms]
        den = functools.reduce(jnp.add, [wt * s_scr[p, r] for p, wt in enumerate(wts)])
        coefs = expand(jnp.concatenate([wt / den for wt in wts], axis=0))
        for slab in range(n_slabs):
            cols = slice(slab * LANES, (slab + 1) * LANES)
            out_scr[slab, pl.ds(r, w, stride=n_res), :] = functools.reduce(
                jnp.add, [coefs[p * w:(p + 1) * w, cols] * acc_scr[p, slab, r]
                          for p in range(n_pat)])
        return carry
    lax.fori_loop(0, n_res, finish, 0, unroll=4)
    for slab in range(n_slabs):
        o_ref[:, slab * LANES:(slab + 1) * LANES] = out_scr[slab].astype(o_ref.dtype)


def _swa(qb, kb, vb, bias, batch, seq):
    n_slabs = SWA_WIDTH // LANES
    n_res = SWA_RESIDUES
    n_pat = len(DILATED_PATTERNS)
    w = DILATED_PATTERNS[0][0] // DILATED_PATTERNS[0][1]
    tq = n_res * w
    nq = seq // tq
    q_blk = pl.BlockSpec((n_slabs, None, n_res, w, LANES), lambda b, i: (0, b, 0, i, 0))
    seq_blk = pl.BlockSpec((n_slabs, None, n_res, seq // n_res, LANES), lambda b, i: (0, b, 0, 0, 0))
    head_of_col = jnp.arange(SWA_WIDTH) // SWA_HEAD_DIM
    sel = (jnp.arange(LANES)[:, None] == head_of_col[None, :]).astype(BF16)
    sel = jnp.concatenate([sel, sel], axis=0)
    return pl.pallas_call(
        functools.partial(_swa_kernel, w=w),
        grid=(batch, nq),
        in_specs=[q_blk, seq_blk, seq_blk, _resident(bias.shape), _resident(sel.shape)],
        out_specs=pl.BlockSpec((tq, SWA_WIDTH), lambda b, i: (b * nq + i, 0)),
        out_shape=jax.ShapeDtypeStruct((batch * seq, SWA_WIDTH), BF16),
        scratch_shapes=[pltpu.VMEM((n_pat, n_slabs, n_res, w, LANES), F32),
                        pltpu.VMEM((n_pat, n_res, w, LANES), F32),
                        pltpu.VMEM((n_pat, n_res, w, LANES), F32),
                        pltpu.VMEM((n_slabs, tq, LANES), F32)],
        compiler_params=pltpu.CompilerParams(dimension_semantics=("arbitrary", "arbitrary"),
                                             vmem_limit_bytes=VMEM_LIMIT_BYTES),
        name="swa",
    )(qb, kb, vb, bias, sel)


def _t5_causal_bucket(dist):
    max_exact = NUM_BUCKETS // 2
    d = jnp.maximum(dist, 1).astype(F32)
    log_b = max_exact + (jnp.log(d / max_exact) / math.log(MAX_DISTANCE / max_exact)
                         * (NUM_BUCKETS - max_exact)).astype(jnp.int32)
    return jnp.where(dist < max_exact, dist, jnp.minimum(log_b, NUM_BUCKETS - 1))


def _swa_bias_tables(rel_bias):
    w = DILATED_PATTERNS[0][0] // DILATED_PATTERNS[0][1]
    rels, steps = [], []
    for window, dil in DILATED_PATTERNS:
        n_chunks = SWA_RESIDUES // dil
        c = w // n_chunks
        pos = (np.arange(w) % c) * n_chunks + np.arange(w) // c
        rels.append(pos[:, None] + w - np.concatenate([pos, w + pos])[None, :])
        steps.append(rel_bias[_t5_causal_bucket(jnp.arange(w + 1) * dil)].astype(F32))
    rel = np.stack(rels)
    one_hot = (jnp.asarray(np.clip(rel, 0, w))[..., None] == jnp.arange(w + 1)).astype(F32)
    table = jnp.einsum("prh,pijr->phij", jnp.stack(steps), one_hot,
                       precision=lax.Precision.HIGHEST) * math.log2(math.e)
    in_band = np.logical_and(rel >= 0, rel <= w)[:, None]
    no_prev = np.arange(2 * w) >= w
    keep = np.stack([in_band, np.logical_and(in_band, no_prev)], axis=1)
    return jnp.where(jnp.asarray(keep), table[:, None], -jnp.inf)


def _mixffn_kernel(x_ref, oa_ref, ob_ref, woa_ref, wob_ref, gpost_ref, gpre_ref, gfpost_ref,
                   wg_ref, wu_ref, wd_ref, out_ref, act_scr, x1_scr, *, ff_chunk, n_sub):
    sub = x_ref.shape[0] // n_sub
    rows = [slice(i * sub, (i + 1) * sub) for i in range(n_sub)]
    d_ff = wg_ref.shape[1]
    mixes = [jnp.dot(oa_ref[r, :], woa_ref[...], preferred_element_type=F32)
             + jnp.dot(ob_ref[r, :], wob_ref[...], preferred_element_type=F32) for r in rows]
    for r, mix in zip(rows, mixes):
        x1 = x_ref[r, :] + _rms(mix, gpost_ref[...])
        x1_scr[r, :] = x1
        h = _rms(x1, gpre_ref[...]).astype(BF16)
        for c in range(d_ff // ff_chunk):
            cols = slice(c * ff_chunk, (c + 1) * ff_chunk)
            gate = jnp.dot(h, wg_ref[:, cols], preferred_element_type=F32)
            up = jnp.dot(h, wu_ref[:, cols], preferred_element_type=F32)
            act_scr[r, cols] = (_silu(gate) * up).astype(BF16)
    fs = [jnp.dot(act_scr[r, :], wd_ref[...], preferred_element_type=F32) for r in rows]
    for r, f in zip(rows, fs):
        out_ref[r, :] = x1_scr[r, :] + _rms(f, gfpost_ref[...])


def _mixffn(x2d, oa, ob, woa, wob, gpost, gpre, gfpost, wg, wu, wd, tm, sub, ff_chunk):
    T, D = x2d.shape
    d_ff = wg.shape[1]
    row = lambda n: pl.BlockSpec((tm, n), lambda i: (i, 0))
    return pl.pallas_call(
        functools.partial(_mixffn_kernel, ff_chunk=ff_chunk, n_sub=tm // sub),
        grid=(T // tm,),
        in_specs=([row(D), row(GDN_WIDTH), row(SWA_WIDTH)]
                  + [_resident(woa.shape), _resident(wob.shape)] + [_resident((1, D))] * 3
                  + [_resident(wg.shape), _resident(wu.shape), _resident(wd.shape)]),
        out_specs=row(D),
        out_shape=jax.ShapeDtypeStruct((T, D), F32),
        scratch_shapes=[pltpu.VMEM((tm, d_ff), BF16), pltpu.VMEM((tm, D), F32)],
        compiler_params=pltpu.CompilerParams(dimension_semantics=("arbitrary",),
                                             vmem_limit_bytes=VMEM_LIMIT_BYTES),
        name="mixffn",
    )(x2d, oa, ob, woa, wob, gpost, gpre, gfpost, wg, wu, wd)


def _layer(x2d, batch, seq, w_in, conv_w, a_log, dt_bias, onorm_g, rel_bias, w_out,
           g_mix_pre, g_mix_post, w_gate, w_up, w_down, g_ffn_pre, g_ffn_post):
    D = x2d.shape[1]
    gw, sw, nh = GDN_WIDTH, SWA_WIDTH, GDN_HEADS
    wb = w_in.astype(BF16)
    c_gate, c_small, c_q = 3 * gw, 4 * gw, 4 * gw + 2 * nh
    w_small = jnp.pad(wb[:, c_small:c_q], ((0, 0), (0, LANES - 2 * nh)))
    lane_pad = lambda v: jnp.pad(v.astype(F32).reshape(1, nh), ((0, 0), (nh, LANES - 2 * nh)))
    qkv, gate, qb, kb, vb, bg = _inproj(
        x2d, g_mix_pre.reshape(1, D), wb[:, :c_gate], wb[:, c_gate:c_small],
        wb[:, c_q:c_q + sw], wb[:, c_q + sw:c_q + 2 * sw], wb[:, c_q + 2 * sw:], w_small,
        conv_w.astype(F32), lane_pad(a_log), lane_pad(dt_bias), batch, seq, tm=INPROJ_ROWS)

    oa = _gdn(qkv, gate, bg, onorm_g.astype(F32).reshape(1, GDN_HEAD_DIM), batch, seq,
              tb=GDN_ROWS, n_seq=GDN_SEQS)

    ob = _swa(qb, kb, vb, _swa_bias_tables(rel_bias), batch, seq)

    wo = w_out.astype(BF16)
    return _mixffn(x2d, oa, ob, wo[:gw], wo[gw:], g_mix_post.reshape(1, D),
                   g_ffn_pre.reshape(1, D), g_ffn_post.reshape(1, D), w_gate.astype(BF16),
                   w_up.astype(BF16), w_down.astype(BF16), tm=FFN_ROWS, sub=FFN_SUB_ROWS,
                   ff_chunk=FFN_COL_CHUNK)


def kernel(x, w_in, conv_w, a_log, dt_bias, onorm_g, rel_bias, w_out, g_mix_pre, g_mix_post,
           w_gate, w_up, w_down, g_ffn_pre, g_ffn_post):
    batch, seq, d_model = x.shape
    x2d = x.reshape(batch * seq, d_model)
    for l in range(w_in.shape[0]):
        x2d = _layer(x2d, batch, seq, w_in[l], conv_w[l], a_log[l], dt_bias[l], onorm_g[l],
                     rel_bias, w_out[l], g_mix_pre[l], g_mix_post[l], w_gate[l], w_up[l],
                     w_down[l], g_ffn_pre[l], g_ffn_post[l])
    return x2d.reshape(batch, seq, d_model)
```

```python
import functools
import math

import jax
import jax.numpy as jnp
import numpy as np
from jax import lax
from jax.experimental import pallas as pl
from jax.experimental.pallas import tpu as pltpu

F32 = jnp.float32
BF16 = jnp.bfloat16

GDN_HEADS = 4
GDN_HEAD_DIM = 128
GDN_WIDTH = GDN_HEADS * GDN_HEAD_DIM
CONV_WIDTH = 4
CONV_ROW_PITCH = 2
CHUNK = 64
SWA_HEADS = 8
SWA_HEAD_DIM = 64
SWA_WIDTH = SWA_HEADS * SWA_HEAD_DIM
DILATED_PATTERNS = ((128, 1), (512, 4), (2048, 16))
SWA_RESIDUES = max(dil for _, dil in DILATED_PATTERNS)
NUM_BUCKETS = 32
MAX_DISTANCE = 2048
RMS_EPS = 1e-6

LANES = 128
SUBLANES = 8
BF16_ROWS = 16
SWA_Q_SCALE = SWA_HEAD_DIM ** -0.5 * math.log2(math.e)
SWA_TASK_UNROLL = 8

INPROJ_ROWS, INPROJ_SUBTILES = 512, 2
GDN_ROWS, GDN_SEQS = 256, 2
FFN_ROWS, FFN_SUB_ROWS, FFN_COL_CHUNK = 1024, 256, 256
VMEM_LIMIT_BYTES = 56 * 1024 * 1024

NT_DIMS = (((1,), (1,)), ((), ()))
TN_DIMS = (((0,), (0,)), ((), ()))


def _rms(x, g):
    return x * lax.rsqrt(jnp.mean(x * x, axis=-1, keepdims=True) + RMS_EPS) * g


def _silu(x):
    return x * jax.nn.sigmoid(x)


def _softplus(x):
    return jnp.maximum(x, 0.0) + jnp.log1p(jnp.exp(-jnp.abs(x)))


def _resident(shape):
    zeros = (0,) * len(shape)
    return pl.BlockSpec(shape, lambda *_: zeros, pipeline_mode=pl.Buffered(1))


def _inproj_kernel(x_ref, g_ref, wqkv_ref, wgate_ref, wq_ref, wk_ref, wv_ref, ws_ref,
                   convw_ref, alog_ref, dtb_ref,
                   qkv_ref, gate_ref, qb_ref, kb_ref, vb_ref, bg_ref,
                   stage_scr, stage2_scr, ext_scr, *, n_sub):
    tm = x_ref.shape[0]
    hd = GDN_HEAD_DIM
    n_groups = 3 * GDN_WIDTH // hd
    halo = SUBLANES
    sub = tm // n_sub
    rows = [slice(i * sub, (i + 1) * sub) for i in range(n_sub)]
    hs = [_rms(x_ref[r, :], g_ref[...]).astype(BF16) for r in rows]

    at = lambda first, n: pl.ds(CONV_ROW_PITCH * first, n, stride=CONV_ROW_PITCH)

    @pl.when(pl.program_id(1) == 0)
    def _():
        for j in range(n_groups):
            ext_scr[j, at(0, halo), :] = jnp.zeros((halo, hd), F32)

    @pl.when(pl.program_id(1) > 0)
    def _():
        for j in range(n_groups):
            ext_scr[j, at(0, halo), :] = ext_scr[j, at(tm, halo), :]

    pres = [jnp.dot(h, wqkv_ref[...], preferred_element_type=F32) for h in hs]
    for i, pre in enumerate(pres):
        for j in range(n_groups):
            ext_scr[j, at(halo + i * sub, sub), :] = pre[:, j * hd:(j + 1) * hd]

    def conv_act(i):
        for j in range(n_groups):
            cols = slice(j * hd, (j + 1) * hd)
            acc = None
            for tap in range(CONV_WIDTH):
                first = halo + i * sub - (CONV_WIDTH - 1) + tap
                term = convw_ref[tap:tap + 1, cols] * ext_scr[j, at(first, sub), :]
                acc = term if acc is None else acc + term
            y = _silu(acc)
            kind = j // GDN_HEADS
            if kind < 2:
                y = y * lax.rsqrt(jnp.sum(y * y, axis=-1, keepdims=True) + 1e-6)
            if kind == 0:
                y = y * (hd ** -0.5)
            qkv_ref[rows[i], cols] = y

    n_res = SWA_RESIDUES
    f = math.isqrt(n_res)
    assert f * f == n_res
    attn_proj = ((wq_ref, qb_ref, SWA_Q_SCALE), (wk_ref, kb_ref, None), (wv_ref, vb_ref, None))

    def split(i, t, o_ref):
        r0 = i * sub
        for slab in range(SWA_WIDTH // LANES):
            for a in range(f):
                stage2_scr[t, slab, a, r0 // f:(r0 + sub) // f, :] = (
                    stage_scr[t, slab, pl.ds(r0 + a, sub // f, stride=f), :])
        for slab in range(SWA_WIDTH // LANES):
            for a in range(f):
                for b in range(f):
                    o_ref[slab, f * b + a, r0 // n_res:(r0 + sub) // n_res, :] = stage2_scr[
                        t, slab, a, pl.ds(r0 // f + b, sub // n_res, stride=f), :].astype(o_ref.dtype)

    for i, h in enumerate(hs):
        gate_ref[rows[i], :] = _silu(jnp.dot(h, wgate_ref[...], preferred_element_type=F32))
        sm = jnp.dot(h, ws_ref[...], preferred_element_type=F32)
        is_beta = lax.broadcasted_iota(jnp.int32, sm.shape, 1) < GDN_HEADS
        bg_ref[rows[i], :] = jnp.where(is_beta, jax.nn.sigmoid(sm),
                                       -jnp.exp(alog_ref[...]) * _softplus(sm + dtb_ref[...]))
        for t, (w_ref, _, scale) in enumerate(attn_proj):
            res = jnp.dot(h, w_ref[...], preferred_element_type=F32)
            if scale is not None:
                res = res * scale
            for slab in range(SWA_WIDTH // LANES):
                stage_scr[t, slab, rows[i], :] = res[:, slab * LANES:(slab + 1) * LANES]
        conv_act(i)
        for t, (_, o_ref, _) in enumerate(attn_proj):
            split(i, t, o_ref)


def _inproj(x2d, g, wqkv, wgate, wq, wk, wv, ws, conv_w, alog_row, dtb_row, batch, seq, tm):
    T, D = x2d.shape
    nt = seq // tm
    row = lambda n: pl.BlockSpec((tm, n), lambda b, j: (b * nt + j, 0))
    n_slabs = SWA_WIDTH // LANES
    slabs = pl.BlockSpec((n_slabs, None, SWA_RESIDUES, tm // SWA_RESIDUES, LANES),
                         lambda b, j: (0, b, 0, j, 0))
    slab_shape = jax.ShapeDtypeStruct(
        (n_slabs, batch, SWA_RESIDUES, seq // SWA_RESIDUES, LANES), BF16)
    return pl.pallas_call(
        functools.partial(_inproj_kernel, n_sub=INPROJ_SUBTILES),
        grid=(batch, nt),
        in_specs=([row(D), _resident((1, D))]
                  + [_resident(w.shape) for w in (wqkv, wgate, wq, wk, wv, ws, conv_w)]
                  + [_resident((1, LANES))] * 2),
        out_specs=[row(3 * GDN_WIDTH), row(GDN_WIDTH), slabs, slabs, slabs, row(LANES)],
        out_shape=[jax.ShapeDtypeStruct((T, 3 * GDN_WIDTH), F32),
                   jax.ShapeDtypeStruct((T, GDN_WIDTH), F32), slab_shape, slab_shape, slab_shape,
                   jax.ShapeDtypeStruct((T, LANES), F32)],
        scratch_shapes=[pltpu.VMEM((3, n_slabs, tm, LANES), F32),
                        pltpu.VMEM((3, n_slabs, math.isqrt(SWA_RESIDUES),
                                    tm // math.isqrt(SWA_RESIDUES), LANES), F32),
                        pltpu.VMEM((3 * GDN_WIDTH // GDN_HEAD_DIM, CONV_ROW_PITCH * (tm + SUBLANES),
                                    GDN_HEAD_DIM), F32)],
        compiler_params=pltpu.CompilerParams(dimension_semantics=("arbitrary", "arbitrary"),
                                             vmem_limit_bytes=VMEM_LIMIT_BYTES),
        name="inproj",
    )(x2d, g, wqkv, wgate, wq, wk, wv, ws, conv_w, alog_row, dtb_row)


def _unit_lower_inverses(mats, eye, xor_ij):
    n = mats[0].shape[0]
    ds = [eye - jnp.where(xor_ij < 2, a, 0.0) for a in mats]
    mats_b = [a.astype(BF16) for a in mats]
    s = 2
    while s < n:
        band = jnp.logical_and(xor_ij >= s, xor_ij < 2 * s)
        dbs = [d.astype(BF16) for d in ds]
        eds = [jnp.dot(jnp.where(band, ab, jnp.zeros_like(ab)), db,
                       preferred_element_type=F32).astype(BF16) for ab, db in zip(mats_b, dbs)]
        ds = [d - jnp.dot(db, ed, preferred_element_type=F32) for d, db, ed in zip(ds, dbs, eds)]
        s *= 2
    return ds


def _gdn_kernel(qkv_ref, gate_ref, bg_ref, ong_ref, ltri_ref, o_ref, state_scr, *, tb, chunk):
    hd = GDN_HEAD_DIM

    @pl.when(pl.program_id(1) == 0)
    def _():
        state_scr[...] = jnp.zeros_like(state_scr)

    ii = lax.broadcasted_iota(jnp.int32, (chunk, chunk), 0)
    jj = lax.broadcasted_iota(jnp.int32, (chunk, chunk), 1)
    xor_ij = jnp.bitwise_xor(ii, jj)
    lower = ii >= jj
    strict = ii > jj
    eye = jnp.where(ii == jj, 1.0, 0.0).astype(F32)
    ltri = ltri_ref[...]
    ong = ong_ref[...]

    nc = tb // chunk
    heads = range(GDN_HEADS)
    rows = [slice(c * chunk, (c + 1) * chunk) for c in range(nc)]
    hcols = [slice(h * hd, (h + 1) * hd) for h in heads]
    qcols, kcols, vcols = ([slice(part * GDN_WIDTH + h * hd, part * GDN_WIDTH + (h + 1) * hd)
                            for h in heads] for part in range(3))
    seqs = range(qkv_ref.shape[0])
    probs = [(s, c, h) for s in seqs for c in range(nc) for h in heads]

    gcs = {}
    for s in seqs:
        for c in range(nc):
            g = bg_ref[s, rows[c], :]
            g_hi = g.astype(BF16)
            rem = g - g_hi.astype(F32)
            g_mid = rem.astype(BF16)
            g_lo = (rem - g_mid.astype(F32)).astype(BF16)
            gcs[s, c] = (jnp.dot(ltri, g_hi, preferred_element_type=F32)
                         + jnp.dot(ltri, g_mid, preferred_element_type=F32)
                         + jnp.dot(ltri, g_lo, preferred_element_type=F32))
    gcts = {sc: gc.T for sc, gc in gcs.items()}
    lane_of = lambda h: slice(GDN_HEADS + h, GDN_HEADS + h + 1)
    gcol = {(s, c, h): gcs[s, c][:, lane_of(h)] for s, c, h in probs}
    glast = {(s, c, h): gcs[s, c][chunk - 1:chunk, lane_of(h)] for s, c, h in probs}
    bcol = {(s, c, h): bg_ref[s, rows[c], h:h + 1] for s, c, h in probs}

    qks = {}
    for s, c, h in probs:
        kcb = qkv_ref[s, rows[c], kcols[h]].astype(BF16)
        qkb = jnp.concatenate([qkv_ref[s, rows[c], qcols[h]].astype(BF16), kcb], axis=0)
        qks[s, c, h] = lax.dot_general(qkb, kcb, NT_DIMS, preferred_element_type=F32)
    attn, a_mats = {}, []
    for s, c, h in probs:
        grow = gcts[s, c][lane_of(h), :]
        decay = jnp.exp(jnp.where(lower, gcol[s, c, h] - grow, -jnp.inf))
        attn[s, c, h] = (qks[s, c, h][:chunk] * decay).astype(BF16)
        a_mats.append(jnp.where(strict, qks[s, c, h][chunk:] * decay * bcol[s, c, h], 0.0))
    t_invs = _unit_lower_inverses(a_mats, eye, xor_ij)

    uws = {}
    for (s, c, h), t_inv in zip(probs, t_invs):
        kc = qkv_ref[s, rows[c], kcols[h]]
        eg = jnp.exp(gcol[s, c, h])
        rhs = jnp.concatenate([qkv_ref[s, rows[c], vcols[h]] * bcol[s, c, h],
                               kc * (bcol[s, c, h] * eg)], axis=1).astype(BF16)
        uws[s, c, h] = jnp.dot(t_inv.astype(BF16), rhs, preferred_element_type=F32).astype(BF16)
    kd_uw, gq, attn_u = {}, {}, {}
    for s, c, h in probs:
        k_dec = qkv_ref[s, rows[c], kcols[h]] * jnp.exp(glast[s, c, h] - gcol[s, c, h])
        kd_uw[s, c, h] = lax.dot_general(k_dec.astype(BF16), uws[s, c, h], TN_DIMS,
                                         preferred_element_type=F32)
    for s, c, h in probs:
        a_uw = jnp.dot(attn[s, c, h], uws[s, c, h], preferred_element_type=F32)
        q_eff = qkv_ref[s, rows[c], qcols[h]] * jnp.exp(gcol[s, c, h]) - a_uw[:, hd:]
        gq[s, c, h] = jnp.concatenate([kd_uw[s, c, h][:, hd:], q_eff], axis=0).astype(BF16)
        attn_u[s, c, h] = a_uw[:, :hd]

    chains = [(s, h) for s in seqs for h in heads]
    state = {sh: state_scr[sh] for sh in chains}
    for c in range(nc):
        gs = {(s, h): jnp.dot(gq[s, c, h], state[s, h].astype(BF16), preferred_element_type=F32)
              for s, h in chains}
        state = {(s, h): (state[s, h] * jnp.exp(glast[s, c, h]) - gs[s, h][:hd]
                          + kd_uw[s, c, h][:, :hd]) for s, h in chains}
        for s, h in chains:
            o = gs[s, h][hd:] + attn_u[s, c, h]
            gated = _rms(o, ong) * gate_ref[s, rows[c], hcols[h]]
            o_ref[s, rows[c], hcols[h]] = gated.astype(o_ref.dtype)
    for sh in chains:
        state_scr[sh] = state[sh]


def _gdn(qkv, gate, bg, onorm_g, batch, seq, tb, n_seq):
    idx = jnp.arange(CHUNK)
    ltri = (idx[:, None] >= idx[None, :]).astype(BF16)
    blk = lambda n: pl.BlockSpec((n_seq, tb, n), lambda b, t: (b, t, 0))
    per_seq = lambda a: a.reshape(batch, seq, a.shape[-1])
    kern = functools.partial(_gdn_kernel, tb=tb, chunk=CHUNK)
    out = pl.pallas_call(
        kern,
        grid=(batch // n_seq, seq // tb),
        in_specs=[blk(3 * GDN_WIDTH), blk(GDN_WIDTH), blk(LANES), _resident((1, GDN_HEAD_DIM)),
                  _resident((CHUNK, CHUNK))],
        out_specs=blk(GDN_WIDTH),
        out_shape=jax.ShapeDtypeStruct((batch, seq, GDN_WIDTH), BF16),
        scratch_shapes=[pltpu.VMEM((n_seq, GDN_HEADS, GDN_HEAD_DIM, GDN_HEAD_DIM), F32)],
        compiler_params=pltpu.CompilerParams(dimension_semantics=("arbitrary", "arbitrary"),
                                             vmem_limit_bytes=VMEM_LIMIT_BYTES),
        name="gdn",
    )(per_seq(qkv), per_seq(gate), per_seq(bg), onorm_g, ltri)
    return out.reshape(batch * seq, GDN_WIDTH)


def _swa_kernel(q_ref, k_ref, v_ref, bias_ref, sel_ref, o_ref, acc_scr, m_scr, s_scr, out_scr,
                *, w):
    n_res = SWA_RESIDUES
    tile = pl.program_id(1)
    n_slabs = SWA_WIDTH // LANES
    heads_per_slab = LANES // SWA_HEAD_DIM
    assert heads_per_slab == 2
    tasks_per_call = lambda dil: 2 if (w * dil // n_res) % BF16_ROWS else 1
    lane = lax.broadcasted_iota(jnp.int32, (w, LANES), 1)
    low_half = lane < SWA_HEAD_DIM

    def expand(packed):
        hi = packed.astype(BF16)
        lo = (packed - hi.astype(F32)).astype(BF16)
        return jnp.dot(jnp.concatenate([hi, lo], axis=1), sel_ref[...], preferred_element_type=F32)

    def task(pat, dil, t, sub):
        n_chunks = n_res // dil
        c = w // n_chunks
        per_call = tasks_per_call(dil)
        t = t * per_call + sub
        res_d = jnp.bitwise_and(t, dil - 1)
        blk = jnp.right_shift(t, dil.bit_length() - 1)
        n_glob = tile * n_chunks + blk
        q0 = pl.multiple_of(blk * c, c)
        cur0 = pl.multiple_of(n_glob * c, c)
        prev0 = pl.multiple_of(jnp.maximum(n_glob - 1, 0) * c, c)
        is_first_blk = (n_glob == 0).astype(jnp.int32)
        res_of = [a * dil + res_d for a in range(n_chunks)]

        def gather(ref, lead, start):
            return jnp.concatenate([ref[lead + (res, pl.ds(start, c), slice(None))]
                                    for res in res_of], axis=0)

        def operand(ref, slab, start, shift):
            if c % BF16_ROWS == 0:
                return gather(ref, (slab,), start)
            assert BF16_ROWS == 2 * c and dil == 1 and per_call % 2 == 0
            half = (sub + shift) % 2
            tile0 = pl.multiple_of(jnp.maximum(start - half * c, 0), BF16_ROWS)
            return jnp.concatenate(
                [ref[slab, res, pl.ds(tile0, BF16_ROWS), :].astype(F32)[half * c:(half + 1) * c]
                 for res in res_of], axis=0).astype(BF16)

        logits, vps = [], []
        for slab in range(n_slabs):
            qp = operand(q_ref, slab, q0, 0)
            kp = jnp.concatenate([operand(k_ref, slab, prev0, -1), operand(k_ref, slab, cur0, 0)],
                                 axis=0)
            vps.append(jnp.concatenate([operand(v_ref, slab, prev0, -1),
                                        operand(v_ref, slab, cur0, 0)], axis=0))
            for e in range(heads_per_slab):
                mine = low_half if e == 0 else jnp.logical_not(low_half)
                qh = jnp.where(mine, qp, jnp.zeros_like(qp))
                lg = lax.dot_general(qh, kp, NT_DIMS, preferred_element_type=F32)
                logits.append(lg + bias_ref[pat, is_first_blk, slab * heads_per_slab + e])
        m_blk = jnp.zeros((w, LANES), F32)
        s_blk = jnp.ones((w, LANES), F32)
        ps = []
        for h, lg in enumerate(logits):
            m_h = jnp.max(lg, axis=-1, keepdims=True)
            p = jnp.exp2(lg - m_h)
            ps.append(p.astype(BF16))
            m_blk = jnp.where(lane == h, m_h, m_blk)
            s_blk = jnp.where(lane == h, jnp.sum(p, axis=-1, keepdims=True), s_blk)
        nums = []
        for slab in range(n_slabs):
            h0 = slab * heads_per_slab
            outs = [jnp.dot(ps[h0 + e], vps[slab], preferred_element_type=F32)
                    for e in range(heads_per_slab)]
            nums.append(jnp.where(low_half, outs[0], outs[1]))
        for idx, res in enumerate(res_of):
            rows, part = pl.ds(q0, c), slice(idx * c, (idx + 1) * c)
            for slab in range(n_slabs):
                acc_scr[pat, slab, res, rows, :] = nums[slab][part]
            m_scr[pat, res, rows, :] = m_blk[part]
            s_scr[pat, res, rows, :] = s_blk[part]

    n_tasks = n_res
    for pat, (window, dil) in enumerate(DILATED_PATTERNS):
        assert window // dil == w and n_res % dil == 0 and (w * dil) % n_res == 0

        per_call = tasks_per_call(dil)

        def body(t, carry, pat=pat, dil=dil, per_call=per_call):
            for sub in range(per_call):
                task(pat, dil, t, sub)
            return carry
        lax.fori_loop(0, n_tasks // per_call, body, 0, unroll=SWA_TASK_UNROLL // per_call)

    n_pat = len(DILATED_PATTERNS)

    def finish(r, carry):
        ms = [m_scr[p, r] for p in range(n_pat)]
        m_max = functools.reduce(jnp.maximum, ms)
        wts = [jnp.exp2(m - m_max) for m in ms]
        den = functools.reduce(jnp.add, [wt * s_scr[p, r] for p, wt in enumerate(wts)])
        coefs = expand(jnp.concatenate([wt / den for wt in wts], axis=0))
        for slab in range(n_slabs):
            cols = slice(slab * LANES, (slab + 1) * LANES)
            out_scr[slab, pl.ds(r, w, stride=n_res), :] = functools.reduce(
                jnp.add, [coefs[p * w:(p + 1) * w, cols] * acc_scr[p, slab, r]
                          for p in range(n_pat)])
        return carry
    lax.fori_loop(0, n_res, finish, 0, unroll=8)
    for slab in range(n_slabs):
        o_ref[:, slab * LANES:(slab + 1) * LANES] = out_scr[slab].astype(o_ref.dtype)


def _swa(qb, kb, vb, bias, batch, seq):
    n_slabs = SWA_WIDTH // LANES
    n_res = SWA_RESIDUES
    n_pat = len(DILATED_PATTERNS)
    w = DILATED_PATTERNS[0][0] // DILATED_PATTERNS[0][1]
    tq = n_res * w
    nq = seq // tq
    q_blk = pl.BlockSpec((n_slabs, None, n_res, w, LANES), lambda b, i: (0, b, 0, i, 0))
    seq_blk = pl.BlockSpec((n_slabs, None, n_res, seq // n_res, LANES), lambda b, i: (0, b, 0, 0, 0))
    head_of_col = jnp.arange(SWA_WIDTH) // SWA_HEAD_DIM
    sel = (jnp.arange(LANES)[:, None] == head_of_col[None, :]).astype(BF16)
    sel = jnp.concatenate([sel, sel], axis=0)
    return pl.pallas_call(
        functools.partial(_swa_kernel, w=w),
        grid=(batch, nq),
        in_specs=[q_blk, seq_blk, seq_blk, _resident(bias.shape), _resident(sel.shape)],
        out_specs=pl.BlockSpec((tq, SWA_WIDTH), lambda b, i: (b * nq + i, 0)),
        out_shape=jax.ShapeDtypeStruct((batch * seq, SWA_WIDTH), BF16),
        scratch_shapes=[pltpu.VMEM((n_pat, n_slabs, n_res, w, LANES), F32),
                        pltpu.VMEM((n_pat, n_res, w, LANES), F32),
                        pltpu.VMEM((n_pat, n_res, w, LANES), F32),
                        pltpu.VMEM((n_slabs, tq, LANES), F32)],
        compiler_params=pltpu.CompilerParams(dimension_semantics=("arbitrary", "arbitrary"),
                                             vmem_limit_bytes=VMEM_LIMIT_BYTES),
        name="swa",
    )(qb, kb, vb, bias, sel)


def _t5_causal_bucket(dist):
    max_exact = NUM_BUCKETS // 2
    d = jnp.maximum(dist, 1).astype(F32)
    log_b = max_exact + (jnp.log(d / max_exact) / math.log(MAX_DISTANCE / max_exact)
                         * (NUM_BUCKETS - max_exact)).astype(jnp.int32)
    return jnp.where(dist < max_exact, dist, jnp.minimum(log_b, NUM_BUCKETS - 1))


def _swa_bias_tables(rel_bias):
    w = DILATED_PATTERNS[0][0] // DILATED_PATTERNS[0][1]
    rels, steps = [], []
    for window, dil in DILATED_PATTERNS:
        n_chunks = SWA_RESIDUES // dil
        c = w // n_chunks
        pos = (np.arange(w) % c) * n_chunks + np.arange(w) // c
        rels.append(pos[:, None] + w - np.concatenate([pos, w + pos])[None, :])
        steps.append(rel_bias[_t5_causal_bucket(jnp.arange(w + 1) * dil)].astype(F32))
    rel = np.stack(rels)
    one_hot = (jnp.asarray(np.clip(rel, 0, w))[..., None] == jnp.arange(w + 1)).astype(F32)
    table = jnp.einsum("prh,pijr->phij", jnp.stack(steps), one_hot,
                       precision=lax.Precision.HIGHEST) * math.log2(math.e)
    in_band = np.logical_and(rel >= 0, rel <= w)[:, None]
    no_prev = np.arange(2 * w) >= w
    keep = np.stack([in_band, np.logical_and(in_band, no_prev)], axis=1)
    return jnp.where(jnp.asarray(keep), table[:, None], -jnp.inf)


def _mixffn_kernel(x_ref, oa_ref, ob_ref, woa_ref, wob_ref, gpost_ref, gpre_ref, gfpost_ref,
                   wg_ref, wu_ref, wd_ref, out_ref, act_scr, x1_scr, *, ff_chunk, n_sub):
    sub = x_ref.shape[0] // n_sub
    rows = [slice(i * sub, (i + 1) * sub) for i in range(n_sub)]
    d_ff = wg_ref.shape[1]
    mixes = [jnp.dot(oa_ref[r, :], woa_ref[...], preferred_element_type=F32)
             + jnp.dot(ob_ref[r, :], wob_ref[...], preferred_element_type=F32) for r in rows]
    for r, mix in zip(rows, mixes):
        x1 = x_ref[r, :] + _rms(mix, gpost_ref[...])
        x1_scr[r, :] = x1
        h = _rms(x1, gpre_ref[...]).astype(BF16)
        for c in range(d_ff // ff_chunk):
            cols = slice(c * ff_chunk, (c + 1) * ff_chunk)
            gate = jnp.dot(h, wg_ref[:, cols], preferred_element_type=F32)
            up = jnp.dot(h, wu_ref[:, cols], preferred_element_type=F32)
            act_scr[r, cols] = (_silu(gate) * up).astype(BF16)
    fs = [jnp.dot(act_scr[r, :], wd_ref[...], preferred_element_type=F32) for r in rows]
    for r, f in zip(rows, fs):
        out_ref[r, :] = x1_scr[r, :] + _rms(f, gfpost_ref[...])


def _mixffn(x2d, oa, ob, woa, wob, gpost, gpre, gfpost, wg, wu, wd, tm, sub, ff_chunk):
    T, D = x2d.shape
    d_ff = wg.shape[1]
    row = lambda n: pl.BlockSpec((tm, n), lambda i: (i, 0))
    return pl.pallas_call(
        functools.partial(_mixffn_kernel, ff_chunk=ff_chunk, n_sub=tm // sub),
        grid=(T // tm,),
        in_specs=([row(D), row(GDN_WIDTH), row(SWA_WIDTH)]
                  + [_resident(woa.shape), _resident(wob.shape)] + [_resident((1, D))] * 3
                  + [_resident(wg.shape), _resident(wu.shape), _resident(wd.shape)]),
        out_specs=row(D),
        out_shape=jax.ShapeDtypeStruct((T, D), F32),
        scratch_shapes=[pltpu.VMEM((tm, d_ff), BF16), pltpu.VMEM((tm, D), F32)],
        compiler_params=pltpu.CompilerParams(dimension_semantics=("arbitrary",),
                                             vmem_limit_bytes=VMEM_LIMIT_BYTES),
        name="mixffn",
    )(x2d, oa, ob, woa, wob, gpost, gpre, gfpost, wg, wu, wd)


def _layer(x2d, batch, seq, w_in, conv_w, a_log, dt_bias, onorm_g, rel_bias, w_out,
           g_mix_pre, g_mix_post, w_gate, w_up, w_down, g_ffn_pre, g_ffn_post):
    D = x2d.shape[1]
    gw, sw, nh = GDN_WIDTH, SWA_WIDTH, GDN_HEADS
    wb = w_in.astype(BF16)
    c_gate, c_small, c_q = 3 * gw, 4 * gw, 4 * gw + 2 * nh
    w_small = jnp.pad(wb[:, c_small:c_q], ((0, 0), (0, LANES - 2 * nh)))
    lane_pad = lambda v: jnp.pad(v.astype(F32).reshape(1, nh), ((0, 0), (nh, LANES - 2 * nh)))
    qkv, gate, qb, kb, vb, bg = _inproj(
        x2d, g_mix_pre.reshape(1, D), wb[:, :c_gate], wb[:, c_gate:c_small],
        wb[:, c_q:c_q + sw], wb[:, c_q + sw:c_q + 2 * sw], wb[:, c_q + 2 * sw:], w_small,
        conv_w.astype(F32), lane_pad(a_log), lane_pad(dt_bias), batch, seq, tm=INPROJ_ROWS)

    oa = _gdn(qkv, gate, bg, onorm_g.astype(F32).reshape(1, GDN_HEAD_DIM), batch, seq,
              tb=GDN_ROWS, n_seq=GDN_SEQS)

    ob = _swa(qb, kb, vb, _swa_bias_tables(rel_bias), batch, seq)

    wo = w_out.astype(BF16)
    return _mixffn(x2d, oa, ob, wo[:gw], wo[gw:], g_mix_post.reshape(1, D),
                   g_ffn_pre.reshape(1, D), g_ffn_post.reshape(1, D), w_gate.astype(BF16),
                   w_up.astype(BF16), w_down.astype(BF16), tm=FFN_ROWS, sub=FFN_SUB_ROWS,
                   ff_chunk=FFN_COL_CHUNK)


def kernel(x, w_in, conv_w, a_log, dt_bias, onorm_g, rel_bias, w_out, g_mix_pre, g_mix_post,
           w_gate, w_up, w_down, g_ffn_pre, g_ffn_post):
    batch, seq, d_model = x.shape
    x2d = x.reshape(batch * seq, d_model)
    for l in range(w_in.shape[0]):
        x2d = _layer(x2d, batch, seq, w_in[l], conv_w[l], a_log[l], dt_bias[l], onorm_g[l],
                     rel_bias, w_out[l], g_mix_pre[l], g_mix_post[l], w_gate[l], w_up[l],
                     w_down[l], g_ffn_pre[l], g_ffn_post[l])
    return x2d.reshape(batch, seq, d_model)
```

```python
import functools
import math

import jax
import jax.numpy as jnp
import numpy as np
from jax import lax
from jax.experimental import pallas as pl
from jax.experimental.pallas import tpu as pltpu

F32 = jnp.float32
BF16 = jnp.bfloat16

GDN_HEADS = 4
GDN_HEAD_DIM = 128
GDN_WIDTH = GDN_HEADS * GDN_HEAD_DIM
CONV_WIDTH = 4
CONV_ROW_PITCH = 2
CHUNK = 64
SWA_HEADS = 8
SWA_HEAD_DIM = 64
SWA_WIDTH = SWA_HEADS * SWA_HEAD_DIM
DILATED_PATTERNS = ((128, 1), (512, 4), (2048, 16))
SWA_RESIDUES = max(dil for _, dil in DILATED_PATTERNS)
NUM_BUCKETS = 32
MAX_DISTANCE = 2048
RMS_EPS = 1e-6

LANES = 128
SUBLANES = 8
BF16_ROWS = 16
SWA_Q_SCALE = SWA_HEAD_DIM ** -0.5 * math.log2(math.e)
SWA_TASK_UNROLL = 8

INPROJ_ROWS, INPROJ_SUBTILES = 512, 2
GDN_ROWS, GDN_SEQS = 256, 2
FFN_ROWS, FFN_SUB_ROWS, FFN_COL_CHUNK = 1024, 256, 256
VMEM_LIMIT_BYTES = 56 * 1024 * 1024

NT_DIMS = (((1,), (1,)), ((), ()))
TN_DIMS = (((0,), (0,)), ((), ()))


def _rms(x, g):
    return x * lax.rsqrt(jnp.mean(x * x, axis=-1, keepdims=True) + RMS_EPS) * g


def _silu(x):
    return x * jax.nn.sigmoid(x)


def _softplus(x):
    return jnp.maximum(x, 0.0) + jnp.log1p(jnp.exp(-jnp.abs(x)))


def _resident(shape):
    zeros = (0,) * len(shape)
    return pl.BlockSpec(shape, lambda *_: zeros, pipeline_mode=pl.Buffered(1))


def _inproj_kernel(x_ref, g_ref, wqkv_ref, wgate_ref, wq_ref, wk_ref, wv_ref, ws_ref,
                   convw_ref, alog_ref, dtb_ref,
                   qkv_ref, gate_ref, qb_ref, kb_ref, vb_ref, bg_ref,
                   stage_scr, stage2_scr, ext_scr, *, n_sub):
    tm = x_ref.shape[0]
    hd = GDN_HEAD_DIM
    n_groups = 3 * GDN_WIDTH // hd
    halo = SUBLANES
    sub = tm // n_sub
    rows = [slice(i * sub, (i + 1) * sub) for i in range(n_sub)]
    hs = [_rms(x_ref[r, :], g_ref[...]).astype(BF16) for r in rows]

    at = lambda first, n: pl.ds(CONV_ROW_PITCH * first, n, stride=CONV_ROW_PITCH)

    @pl.when(pl.program_id(1) == 0)
    def _():
        for j in range(n_groups):
            ext_scr[j, at(0, halo), :] = jnp.zeros((halo, hd), F32)

    @pl.when(pl.program_id(1) > 0)
    def _():
        for j in range(n_groups):
            ext_scr[j, at(0, halo), :] = ext_scr[j, at(tm, halo), :]

    pres = [jnp.dot(h, wqkv_ref[...], preferred_element_type=F32) for h in hs]
    for i, pre in enumerate(pres):
        for j in range(n_groups):
            ext_scr[j, at(halo + i * sub, sub), :] = pre[:, j * hd:(j + 1) * hd]

    def conv_act(i):
        for j in range(n_groups):
            cols = slice(j * hd, (j + 1) * hd)
            acc = None
            for tap in range(CONV_WIDTH):
                first = halo + i * sub - (CONV_WIDTH - 1) + tap
                term = convw_ref[tap:tap + 1, cols] * ext_scr[j, at(first, sub), :]
                acc = term if acc is None else acc + term
            y = _silu(acc)
            kind = j // GDN_HEADS
            if kind < 2:
                y = y * lax.rsqrt(jnp.sum(y * y, axis=-1, keepdims=True) + 1e-6)
            if kind == 0:
                y = y * (hd ** -0.5)
            qkv_ref[rows[i], cols] = y

    n_res = SWA_RESIDUES
    f = math.isqrt(n_res)
    assert f * f == n_res
    attn_proj = ((wq_ref, qb_ref, SWA_Q_SCALE), (wk_ref, kb_ref, None), (wv_ref, vb_ref, None))

    def split(i, t, o_ref):
        r0 = i * sub
        for slab in range(SWA_WIDTH // LANES):
            for a in range(f):
                stage2_scr[t, slab, a, r0 // f:(r0 + sub) // f, :] = (
                    stage_scr[t, slab, pl.ds(r0 + a, sub // f, stride=f), :])
        for slab in range(SWA_WIDTH // LANES):
            for a in range(f):
                for b in range(f):
                    o_ref[slab, f * b + a, r0 // n_res:(r0 + sub) // n_res, :] = stage2_scr[
                        t, slab, a, pl.ds(r0 // f + b, sub // n_res, stride=f), :].astype(o_ref.dtype)

    for i, h in enumerate(hs):
        gate_ref[rows[i], :] = _silu(jnp.dot(h, wgate_ref[...], preferred_element_type=F32))
        sm = jnp.dot(h, ws_ref[...], preferred_element_type=F32)
        is_beta = lax.broadcasted_iota(jnp.int32, sm.shape, 1) < GDN_HEADS
        bg_ref[rows[i], :] = jnp.where(is_beta, jax.nn.sigmoid(sm),
                                       -jnp.exp(alog_ref[...]) * _softplus(sm + dtb_ref[...]))
        for t, (w_ref, _, scale) in enumerate(attn_proj):
            res = jnp.dot(h, w_ref[...], preferred_element_type=F32)
            if scale is not None:
                res = res * scale
            for slab in range(SWA_WIDTH // LANES):
                stage_scr[t, slab, rows[i], :] = res[:, slab * LANES:(slab + 1) * LANES]
        conv_act(i)
        for t, (_, o_ref, _) in enumerate(attn_proj):
            split(i, t, o_ref)


def _inproj(x2d, g, wqkv, wgate, wq, wk, wv, ws, conv_w, alog_row, dtb_row, batch, seq, tm):
    T, D = x2d.shape
    nt = seq // tm
    row = lambda n: pl.BlockSpec((tm, n), lambda b, j: (b * nt + j, 0))
    n_slabs = SWA_WIDTH // LANES
    slabs = pl.BlockSpec((n_slabs, None, SWA_RESIDUES, tm // SWA_RESIDUES, LANES),
                         lambda b, j: (0, b, 0, j, 0))
    slab_shape = jax.ShapeDtypeStruct(
        (n_slabs, batch, SWA_RESIDUES, seq // SWA_RESIDUES, LANES), BF16)
    return pl.pallas_call(
        functools.partial(_inproj_kernel, n_sub=INPROJ_SUBTILES),
        grid=(batch, nt),
        in_specs=([row(D), _resident((1, D))]
                  + [_resident(w.shape) for w in (wqkv, wgate, wq, wk, wv, ws, conv_w)]
                  + [_resident((1, LANES))] * 2),
        out_specs=[row(3 * GDN_WIDTH), row(GDN_WIDTH), slabs, slabs, slabs, row(LANES)],
        out_shape=[jax.ShapeDtypeStruct((T, 3 * GDN_WIDTH), F32),
                   jax.ShapeDtypeStruct((T, GDN_WIDTH), F32), slab_shape, slab_shape, slab_shape,
                   jax.ShapeDtypeStruct((T, LANES), F32)],
        scratch_shapes=[pltpu.VMEM((3, n_slabs, tm, LANES), F32),
                        pltpu.VMEM((3, n_slabs, math.isqrt(SWA_RESIDUES),
                                    tm // math.isqrt(SWA_RESIDUES), LANES), F32),
                        pltpu.VMEM((3 * GDN_WIDTH // GDN_HEAD_DIM, CONV_ROW_PITCH * (tm + SUBLANES),
                                    GDN_HEAD_DIM), F32)],
        compiler_params=pltpu.CompilerParams(dimension_semantics=("arbitrary", "arbitrary"),
                                             vmem_limit_bytes=VMEM_LIMIT_BYTES),
        name="inproj",
    )(x2d, g, wqkv, wgate, wq, wk, wv, ws, conv_w, alog_row, dtb_row)


def _unit_lower_inverses(mats, eye, xor_ij):
    n = mats[0].shape[0]
    ds = [eye - jnp.where(xor_ij < 2, a, 0.0) for a in mats]
    mats_b = [a.astype(BF16) for a in mats]
    s = 2
    while s < n:
        band = jnp.logical_and(xor_ij >= s, xor_ij < 2 * s)
        dbs = [d.astype(BF16) for d in ds]
        eds = [jnp.dot(jnp.where(band, ab, jnp.zeros_like(ab)), db,
                       preferred_element_type=F32).astype(BF16) for ab, db in zip(mats_b, dbs)]
        ds = [d - jnp.dot(db, ed, preferred_element_type=F32) for d, db, ed in zip(ds, dbs, eds)]
        s *= 2
    return ds


def _gdn_kernel(qkv_ref, gate_ref, bg_ref, ong_ref, ltri_ref, o_ref, state_scr, *, tb, chunk):
    hd = GDN_HEAD_DIM

    @pl.when(pl.program_id(1) == 0)
    def _():
        state_scr[...] = jnp.zeros_like(state_scr)

    ii = lax.broadcasted_iota(jnp.int32, (chunk, chunk), 0)
    jj = lax.broadcasted_iota(jnp.int32, (chunk, chunk), 1)
    xor_ij = jnp.bitwise_xor(ii, jj)
    lower = ii >= jj
    strict = ii > jj
    eye = jnp.where(ii == jj, 1.0, 0.0).astype(F32)
    ltri = ltri_ref[...]
    ong = ong_ref[...]

    nc = tb // chunk
    heads = range(GDN_HEADS)
    rows = [slice(c * chunk, (c + 1) * chunk) for c in range(nc)]
    hcols = [slice(h * hd, (h + 1) * hd) for h in heads]
    qcols, kcols, vcols = ([slice(part * GDN_WIDTH + h * hd, part * GDN_WIDTH + (h + 1) * hd)
                            for h in heads] for part in range(3))
    seqs = range(qkv_ref.shape[0])
    probs = [(s, c, h) for s in seqs for c in range(nc) for h in heads]

    gcs = {}
    for s in seqs:
        for c in range(nc):
            g = bg_ref[s, rows[c], :]
            g_hi = g.astype(BF16)
            rem = g - g_hi.astype(F32)
            g_mid = rem.astype(BF16)
            g_lo = (rem - g_mid.astype(F32)).astype(BF16)
            gcs[s, c] = (jnp.dot(ltri, g_hi, preferred_element_type=F32)
                         + jnp.dot(ltri, g_mid, preferred_element_type=F32)
                         + jnp.dot(ltri, g_lo, preferred_element_type=F32))
    gcts = {sc: gc.T for sc, gc in gcs.items()}
    lane_of = lambda h: slice(GDN_HEADS + h, GDN_HEADS + h + 1)
    gcol = {(s, c, h): gcs[s, c][:, lane_of(h)] for s, c, h in probs}
    glast = {(s, c, h): gcs[s, c][chunk - 1:chunk, lane_of(h)] for s, c, h in probs}
    bcol = {(s, c, h): bg_ref[s, rows[c], h:h + 1] for s, c, h in probs}

    qks = {}
    for s, c, h in probs:
        kcb = qkv_ref[s, rows[c], kcols[h]].astype(BF16)
        qkb = jnp.concatenate([qkv_ref[s, rows[c], qcols[h]].astype(BF16), kcb], axis=0)
        qks[s, c, h] = lax.dot_general(qkb, kcb, NT_DIMS, preferred_element_type=F32)
    attn, a_mats = {}, []
    for s, c, h in probs:
        grow = gcts[s, c][lane_of(h), :]
        decay = jnp.exp(jnp.where(lower, gcol[s, c, h] - grow, -jnp.inf))
        attn[s, c, h] = (qks[s, c, h][:chunk] * decay).astype(BF16)
        a_mats.append(jnp.where(strict, qks[s, c, h][chunk:] * decay * bcol[s, c, h], 0.0))
    t_invs = _unit_lower_inverses(a_mats, eye, xor_ij)

    uws = {}
    for (s, c, h), t_inv in zip(probs, t_invs):
        kc = qkv_ref[s, rows[c], kcols[h]]
        eg = jnp.exp(gcol[s, c, h])
        rhs = jnp.concatenate([qkv_ref[s, rows[c], vcols[h]] * bcol[s, c, h],
                               kc * (bcol[s, c, h] * eg)], axis=1).astype(BF16)
        uws[s, c, h] = jnp.dot(t_inv.astype(BF16), rhs, preferred_element_type=F32).astype(BF16)
    kd_uw, gq, attn_u = {}, {}, {}
    for s, c, h in probs:
        k_dec = qkv_ref[s, rows[c], kcols[h]] * jnp.exp(glast[s, c, h] - gcol[s, c, h])
        kd_uw[s, c, h] = lax.dot_general(k_dec.astype(BF16), uws[s, c, h], TN_DIMS,
                                         preferred_element_type=F32)
    for s, c, h in probs:
        a_uw = jnp.dot(attn[s, c, h], uws[s, c, h], preferred_element_type=F32)
        q_eff = qkv_ref[s, rows[c], qcols[h]] * jnp.exp(gcol[s, c, h]) - a_uw[:, hd:]
        gq[s, c, h] = jnp.concatenate([kd_uw[s, c, h][:, hd:], q_eff], axis=0).astype(BF16)
        attn_u[s, c, h] = a_uw[:, :hd]

    chains = [(s, h) for s in seqs for h in heads]
    state = {sh: state_scr[sh] for sh in chains}
    for c in range(nc):
        gs = {(s, h): jnp.dot(gq[s, c, h], state[s, h].astype(BF16), preferred_element_type=F32)
              for s, h in chains}
        state = {(s, h): (state[s, h] * jnp.exp(glast[s, c, h]) - gs[s, h][:hd]
                          + kd_uw[s, c, h][:, :hd]) for s, h in chains}
        for s, h in chains:
            o = gs[s, h][hd:] + attn_u[s, c, h]
            gated = _rms(o, ong) * gate_ref[s, rows[c], hcols[h]]
            o_ref[s, rows[c], hcols[h]] = gated.astype(o_ref.dtype)
    for sh in chains:
        state_scr[sh] = state[sh]


def _gdn(qkv, gate, bg, onorm_g, batch, seq, tb, n_seq):
    idx = jnp.arange(CHUNK)
    ltri = (idx[:, None] >= idx[None, :]).astype(BF16)
    blk = lambda n: pl.BlockSpec((n_seq, tb, n), lambda b, t: (b, t, 0))
    per_seq = lambda a: a.reshape(batch, seq, a.shape[-1])
    kern = functools.partial(_gdn_kernel, tb=tb, chunk=CHUNK)
    out = pl.pallas_call(
        kern,
        grid=(batch // n_seq, seq // tb),
        in_specs=[blk(3 * GDN_WIDTH), blk(GDN_WIDTH), blk(LANES), _resident((1, GDN_HEAD_DIM)),
                  _resident((CHUNK, CHUNK))],
        out_specs=blk(GDN_WIDTH),
        out_shape=jax.ShapeDtypeStruct((batch, seq, GDN_WIDTH), BF16),
        scratch_shapes=[pltpu.VMEM((n_seq, GDN_HEADS, GDN_HEAD_DIM, GDN_HEAD_DIM), F32)],
        compiler_params=pltpu.CompilerParams(dimension_semantics=("arbitrary", "arbitrary"),
                                             vmem_limit_bytes=VMEM_LIMIT_BYTES),
        name="gdn",
    )(per_seq(qkv), per_seq(gate), per_seq(bg), onorm_g, ltri)
    return out.reshape(batch * seq, GDN_WIDTH)


def _swa_kernel(q_ref, k_ref, v_ref, bias_ref, sel_ref, o_ref, acc_scr, m_scr, s_scr, out_scr,
                *, w):
    n_res = SWA_RESIDUES
    tile = pl.program_id(1)
    n_slabs = SWA_WIDTH // LANES
    heads_per_slab = LANES // SWA_HEAD_DIM
    assert heads_per_slab == 2
    tasks_per_call = lambda dil: 2 if (w * dil // n_res) % BF16_ROWS else 1
    lane = lax.broadcasted_iota(jnp.int32, (w, LANES), 1)
    low_half = lane < SWA_HEAD_DIM

    def expand(packed):
        hi = packed.astype(BF16)
        lo = (packed - hi.astype(F32)).astype(BF16)
        return jnp.dot(jnp.concatenate([hi, lo], axis=1), sel_ref[...], preferred_element_type=F32)

    def task(pat, dil, t, sub):
        n_chunks = n_res // dil
        c = w // n_chunks
        per_call = tasks_per_call(dil)
        t = t * per_call + sub
        res_d = jnp.bitwise_and(t, dil - 1)
        blk = jnp.right_shift(t, dil.bit_length() - 1)
        n_glob = tile * n_chunks + blk
        q0 = pl.multiple_of(blk * c, c)
        cur0 = pl.multiple_of(n_glob * c, c)
        prev0 = pl.multiple_of(jnp.maximum(n_glob - 1, 0) * c, c)
        is_first_blk = (n_glob == 0).astype(jnp.int32)
        res_of = [a * dil + res_d for a in range(n_chunks)]

        def gather(ref, lead, start):
            return jnp.concatenate([ref[lead + (res, pl.ds(start, c), slice(None))]
                                    for res in res_of], axis=0)

        def operand(ref, slab, start, shift):
            if c % BF16_ROWS == 0:
                return gather(ref, (slab,), start)
            assert BF16_ROWS == 2 * c and dil == 1 and per_call % 2 == 0
            half = (sub + shift) % 2
            tile0 = pl.multiple_of(jnp.maximum(start - half * c, 0), BF16_ROWS)
            return jnp.concatenate(
                [ref[slab, res, pl.ds(tile0, BF16_ROWS), :].astype(F32)[half * c:(half + 1) * c]
                 for res in res_of], axis=0).astype(BF16)

        logits, vps = [], []
        for slab in range(n_slabs):
            qp = operand(q_ref, slab, q0, 0)
            kp = jnp.concatenate([operand(k_ref, slab, prev0, -1), operand(k_ref, slab, cur0, 0)],
                                 axis=0)
            vps.append(jnp.concatenate([operand(v_ref, slab, prev0, -1),
                                        operand(v_ref, slab, cur0, 0)], axis=0))
            for e in range(heads_per_slab):
                mine = low_half if e == 0 else jnp.logical_not(low_half)
                qh = jnp.where(mine, qp, jnp.zeros_like(qp))
                lg = lax.dot_general(qh, kp, NT_DIMS, preferred_element_type=F32)
                logits.append(lg + bias_ref[pat, is_first_blk, slab * heads_per_slab + e])
        m_blk = jnp.zeros((w, LANES), F32)
        s_blk = jnp.ones((w, LANES), F32)
        ps = []
        for h, lg in enumerate(logits):
            m_h = jnp.max(lg, axis=-1, keepdims=True)
            p = jnp.exp2(lg - m_h)
            ps.append(p.astype(BF16))
            m_blk = jnp.where(lane == h, m_h, m_blk)
            s_blk = jnp.where(lane == h, jnp.sum(p, axis=-1, keepdims=True), s_blk)
        nums = []
        for slab in range(n_slabs):
            h0 = slab * heads_per_slab
            outs = [jnp.dot(ps[h0 + e], vps[slab], preferred_element_type=F32)
                    for e in range(heads_per_slab)]
            nums.append(jnp.where(low_half, outs[0], outs[1]))
        for idx, res in enumerate(res_of):
            rows, part = pl.ds(q0, c), slice(idx * c, (idx + 1) * c)
            for slab in range(n_slabs):
                acc_scr[pat, slab, res, rows, :] = nums[slab][part]
            m_scr[pat, res, rows, :] = m_blk[part]
            s_scr[pat, res, rows, :] = s_blk[part]

    n_tasks = n_res
    for pat, (window, dil) in enumerate(DILATED_PATTERNS):
        assert window // dil == w and n_res % dil == 0 and (w * dil) % n_res == 0

        per_call = tasks_per_call(dil)

        def body(t, carry, pat=pat, dil=dil, per_call=per_call):
            for sub in range(per_call):
                task(pat, dil, t, sub)
            return carry
        lax.fori_loop(0, n_tasks // per_call, body, 0, unroll=SWA_TASK_UNROLL // per_call)

    n_pat = len(DILATED_PATTERNS)

    def finish(r, carry):
        ms = [m_scr[p, r] for p in range(n_pat)]
        m_max = functools.reduce(jnp.maximum, ms)
        wts = [jnp.exp2(m - m_max) for m in ms]
        den = functools.reduce(jnp.add, [wt * s_scr[p, r] for p, wt in enumerate(wts)])
        coefs = expand(jnp.concatenate([wt / den for wt in wts], axis=0))
        for slab in range(n_slabs):
            cols = slice(slab * LANES, (slab + 1) * LANES)
            out_scr[slab, pl.ds(r, w, stride=n_res), :] = functools.reduce(
                jnp.add, [coefs[p * w:(p + 1) * w, cols] * acc_scr[p, slab, r]
                          for p in range(n_pat)])
        return carry
    lax.fori_loop(0, n_res, finish, 0, unroll=4)
    for slab in range(n_slabs):
        o_ref[:, slab * LANES:(slab + 1) * LANES] = out_scr[slab].astype(o_ref.dtype)


def _swa(qb, kb, vb, bias, batch, seq):
    n_slabs = SWA_WIDTH // LANES
    n_res = SWA_RESIDUES
    n_pat = len(DILATED_PATTERNS)
    w = DILATED_PATTERNS[0][0] // DILATED_PATTERNS[0][1]
    tq = n_res * w
    nq = seq // tq
    q_blk = pl.BlockSpec((n_slabs, None, n_res, w, LANES), lambda b, i: (0, b, 0, i, 0))
    seq_blk = pl.BlockSpec((n_slabs, None, n_res, seq // n_res, LANES), lambda b, i: (0, b, 0, 0, 0))
    head_of_col = jnp.arange(SWA_WIDTH) // SWA_HEAD_DIM
    sel = (jnp.arange(LANES)[:, None] == head_of_col[None, :]).astype(BF16)
    sel = jnp.concatenate([sel, sel], axis=0)
    return pl.pallas_call(
        functools.partial(_swa_kernel, w=w),
        grid=(batch, nq),
        in_specs=[q_blk, seq_blk, seq_blk, _resident(bias.shape), _resident(sel.shape)],
        out_specs=pl.BlockSpec((tq, SWA_WIDTH), lambda b, i: (b * nq + i, 0)),
        out_shape=jax.ShapeDtypeStruct((batch * seq, SWA_WIDTH), BF16),
        scratch_shapes=[pltpu.VMEM((n_pat, n_slabs, n_res, w, LANES), F32),
                        pltpu.VMEM((n_pat, n_res, w, LANES), F32),
                        pltpu.VMEM((n_pat, n_res, w, LANES), F32),
                        pltpu.VMEM((n_slabs, tq, LANES), F32)],
        compiler_params=pltpu.CompilerParams(dimension_semantics=("arbitrary", "arbitrary"),
                                             vmem_limit_bytes=VMEM_LIMIT_BYTES),
        name="swa",
    )(qb, kb, vb, bias, sel)


def _t5_causal_bucket(dist):
    max_exact = NUM_BUCKETS // 2
    d = jnp.maximum(dist, 1).astype(F32)
    log_b = max_exact + (jnp.log(d / max_exact) / math.log(MAX_DISTANCE / max_exact)
                         * (NUM_BUCKETS - max_exact)).astype(jnp.int32)
    return jnp.where(dist < max_exact, dist, jnp.minimum(log_b, NUM_BUCKETS - 1))


def _swa_bias_tables(rel_bias):
    w = DILATED_PATTERNS[0][0] // DILATED_PATTERNS[0][1]
    rels, steps = [], []
    for window, dil in DILATED_PATTERNS:
        n_chunks = SWA_RESIDUES // dil
        c = w // n_chunks
        pos = (np.arange(w) % c) * n_chunks + np.arange(w) // c
        rels.append(pos[:, None] + w - np.concatenate([pos, w + pos])[None, :])
        steps.append(rel_bias[_t5_causal_bucket(jnp.arange(w + 1) * dil)].astype(F32))
    rel = np.stack(rels)
    one_hot = (jnp.asarray(np.clip(rel, 0, w))[..., None] == jnp.arange(w + 1)).astype(F32)
    table = jnp.einsum("prh,pijr->phij", jnp.stack(steps), one_hot,
                       precision=lax.Precision.HIGHEST) * math.log2(math.e)
    in_band = np.logical_and(rel >= 0, rel <= w)[:, None]
    no_prev = np.arange(2 * w) >= w
    keep = np.stack([in_band, np.logical_and(in_band, no_prev)], axis=1)
    return jnp.where(jnp.asarray(keep), table[:, None], -jnp.inf)


def _mixffn_kernel(x_ref, oa_ref, ob_ref, woa_ref, wob_ref, gpost_ref, gpre_ref, gfpost_ref,
                   wg_ref, wu_ref, wd_ref, out_ref, act_scr, x1_scr, *, ff_chunk, n_sub):
    sub = x_ref.shape[0] // n_sub
    rows = [slice(i * sub, (i + 1) * sub) for i in range(n_sub)]
    d_ff = wg_ref.shape[1]
    mixes = [jnp.dot(oa_ref[r, :], woa_ref[...], preferred_element_type=F32)
             + jnp.dot(ob_ref[r, :], wob_ref[...], preferred_element_type=F32) for r in rows]
    for r, mix in zip(rows, mixes):
        x1 = x_ref[r, :] + _rms(mix, gpost_ref[...])
        x1_scr[r, :] = x1
        h = _rms(x1, gpre_ref[...]).astype(BF16)
        for c in range(d_ff // ff_chunk):
            cols = slice(c * ff_chunk, (c + 1) * ff_chunk)
            gate = jnp.dot(h, wg_ref[:, cols], preferred_element_type=F32)
            up = jnp.dot(h, wu_ref[:, cols], preferred_element_type=F32)
            act_scr[r, cols] = (_silu(gate) * up).astype(BF16)
    fs = [jnp.dot(act_scr[r, :], wd_ref[...], preferred_element_type=F32) for r in rows]
    for r, f in zip(rows, fs):
        out_ref[r, :] = x1_scr[r, :] + _rms(f, gfpost_ref[...])


def _mixffn(x2d, oa, ob, woa, wob, gpost, gpre, gfpost, wg, wu, wd, tm, sub, ff_chunk):
    T, D = x2d.shape
    d_ff = wg.shape[1]
    row = lambda n: pl.BlockSpec((tm, n), lambda i: (i, 0))
    return pl.pallas_call(
        functools.partial(_mixffn_kernel, ff_chunk=ff_chunk, n_sub=tm // sub),
        grid=(T // tm,),
        in_specs=([row(D), row(GDN_WIDTH), row(SWA_WIDTH)]
                  + [_resident(woa.shape), _resident(wob.shape)] + [_resident((1, D))] * 3
                  + [_resident(wg.shape), _resident(wu.shape), _resident(wd.shape)]),
        out_specs=row(D),
        out_shape=jax.ShapeDtypeStruct((T, D), F32),
        scratch_shapes=[pltpu.VMEM((tm, d_ff), BF16), pltpu.VMEM((tm, D), F32)],
        compiler_params=pltpu.CompilerParams(dimension_semantics=("arbitrary",),
                                             vmem_limit_bytes=VMEM_LIMIT_BYTES),
        name="mixffn",
    )(x2d, oa, ob, woa, wob, gpost, gpre, gfpost, wg, wu, wd)


def _layer(x2d, batch, seq, w_in, conv_w, a_log, dt_bias, onorm_g, rel_bias, w_out,
           g_mix_pre, g_mix_post, w_gate, w_up, w_down, g_ffn_pre, g_ffn_post):
    D = x2d.shape[1]
    gw, sw, nh = GDN_WIDTH, SWA_WIDTH, GDN_HEADS
    wb = w_in.astype(BF16)
    c_gate, c_small, c_q = 3 * gw, 4 * gw, 4 * gw + 2 * nh
    w_small = jnp.pad(wb[:, c_small:c_q], ((0, 0), (0, LANES - 2 * nh)))
    lane_pad = lambda v: jnp.pad(v.astype(F32).reshape(1, nh), ((0, 0), (nh, LANES - 2 * nh)))
    qkv, gate, qb, kb, vb, bg = _inproj(
        x2d, g_mix_pre.reshape(1, D), wb[:, :c_gate], wb[:, c_gate:c_small],
        wb[:, c_q:c_q + sw], wb[:, c_q + sw:c_q + 2 * sw], wb[:, c_q + 2 * sw:], w_small,
        conv_w.astype(F32), lane_pad(a_log), lane_pad(dt_bias), batch, seq, tm=INPROJ_ROWS)

    oa = _gdn(qkv, gate, bg, onorm_g.astype(F32).reshape(1, GDN_HEAD_DIM), batch, seq,
              tb=GDN_ROWS, n_seq=GDN_SEQS)

    ob = _swa(qb, kb, vb, _swa_bias_tables(rel_bias), batch, seq)

    wo = w_out.astype(BF16)
    return _mixffn(x2d, oa, ob, wo[:gw], wo[gw:], g_mix_post.reshape(1, D),
                   g_ffn_pre.reshape(1, D), g_ffn_post.reshape(1, D), w_gate.astype(BF16),
                   w_up.astype(BF16), w_down.astype(BF16), tm=FFN_ROWS, sub=FFN_SUB_ROWS,
                   ff_chunk=FFN_COL_CHUNK)


def kernel(x, w_in, conv_w, a_log, dt_bias, onorm_g, rel_bias, w_out, g_mix_pre, g_mix_post,
           w_gate, w_up, w_down, g_ffn_pre, g_ffn_post):
    batch, seq, d_model = x.shape
    x2d = x.reshape(batch * seq, d_model)
    for l in range(w_in.shape[0]):
        x2d = _layer(x2d, batch, seq, w_in[l], conv_w[l], a_log[l], dt_bias[l], onorm_g[l],
                     rel_bias, w_out[l], g_mix_pre[l], g_mix_post[l], w_gate[l], w_up[l],
                     w_down[l], g_ffn_pre[l], g_ffn_post[l])
    return x2d.reshape(batch, seq, d_model)
```
